```python
import math
import jax, jax.numpy as jnp
from jax import lax
import numpy as np

D_MODEL = 1024
BATCH = 32
SEQ = 256
DEPTH = 2
DEC_BATCH = 2
DEC_SEQ = 1024
PAST_LEN = 256

GRID_W = 64
POS_BASE = 10000.0
D_MIX = D_MODEL
H_A = 4
DK_A = 128
DV_A = 128
A_W = H_A * DV_A
H_C = 4
P_C = 64
N_C = 64
G_C = 2
C_W = H_C * P_C
B_W = D_MIX - A_W - C_W
CONV_K = 3
CHUNK = 64
N_EXPERTS = 16
N_EXPERT_GROUPS = 4
EXPERTS_PER_GROUP = N_EXPERTS // N_EXPERT_GROUPS
GROUP_TOP = 2
TOP_K = 2
D_EXPERT = 256
EPS = 1e-6
IN_SIZES = (A_W, A_W, A_W, A_W, 2 * H_A, 2 * H_A, B_W, B_W, B_W, C_W, C_W + 2 * G_C * N_C, 2 * H_C)
D_IN = sum(IN_SIZES)
IN_SPLITS = tuple(int(s) for s in np.cumsum(IN_SIZES)[:-1])

kernel_name = "hybrid_deltanet_conv_ssd_moe_flow_trunk"


def _rmsnorm(x, g):
    x32 = x.astype(jnp.float32)
    y = x32 * lax.rsqrt(jnp.mean(x32 * x32, axis=-1, keepdims=True) + EPS)
    return (y * g.astype(jnp.float32)).astype(x.dtype)


def _l2norm(x):
    return x * lax.rsqrt(jnp.sum(x * x, axis=-1, keepdims=True) + EPS)


def _dwconv(x, w):
    return lax.conv_general_dilated(
        x, w[:, None, :].astype(x.dtype), window_strides=(1,),
        padding=((CONV_K // 2, CONV_K // 2),),
        dimension_numbers=("NWC", "WIO", "NWC"), feature_group_count=x.shape[-1])


def _chunked(t):
    n, l = t.shape[:2]
    t = t.reshape(n, l // CHUNK, CHUNK, *t.shape[2:])
    return jnp.moveaxis(t, 2, 3)


def _unchunk(t):
    t = jnp.moveaxis(t, 3, 2)
    return t.reshape(t.shape[0], -1, *t.shape[3:])


def _grid_pos_embed(n_tok, dim):
    rows = n_tok // GRID_W
    rr, cc = jnp.meshgrid(jnp.arange(rows, dtype=jnp.float32), jnp.arange(GRID_W, dtype=jnp.float32), indexing="ij")
    quarter = dim // 4
    omega = 1.0 / (POS_BASE ** (jnp.arange(quarter, dtype=jnp.float32) / quarter))
    ang_r = rr.reshape(-1, 1) * omega
    ang_c = cc.reshape(-1, 1) * omega
    return jnp.concatenate([jnp.sin(ang_r), jnp.cos(ang_r), jnp.sin(ang_c), jnp.cos(ang_c)], axis=-1)


def _chunk_gated_delta(q, k, v, g, beta, s0):
    qc, kc, vc = _chunked(q), _chunked(k), _chunked(v)
    bc = _chunked(beta)[..., None]
    gc = jnp.cumsum(_chunked(g), axis=-1)
    incl = jnp.tril(jnp.ones((CHUNK, CHUNK), dtype=bool))
    strict = jnp.tril(jnp.ones((CHUNK, CHUNK), dtype=bool), -1)
    decay = jnp.exp(jnp.where(incl, gc[..., :, None] - gc[..., None, :], -jnp.inf))
    kb = kc * bc
    m = jnp.where(strict, jnp.einsum("nzhid,nzhjd->nzhij", kb, kc) * decay, 0.0)
    tri = m + jnp.eye(CHUNK, dtype=m.dtype)
    u = lax.linalg.triangular_solve(tri, vc * bc, left_side=True, lower=True, unit_diagonal=True)
    w = lax.linalg.triangular_solve(tri, kb * jnp.exp(gc)[..., None], left_side=True, lower=True, unit_diagonal=True)
    qk = jnp.einsum("nzhid,nzhjd->nzhij", qc, kc) * decay
    qd = qc * jnp.exp(gc)[..., None]
    kd = kc * jnp.exp(gc[..., -1:] - gc)[..., None]
    gl = jnp.exp(gc[..., -1])

    def step(s, inp):
        qk_z, qd_z, w_z, u_z, kd_z, gl_z = inp
        v_new = u_z - jnp.einsum("nhid,nhde->nhie", w_z, s)
        o_z = jnp.einsum("nhid,nhde->nhie", qd_z, s) + jnp.einsum("nhij,nhje->nhie", qk_z, v_new)
        s = s * gl_z[..., None, None] + jnp.einsum("nhid,nhie->nhde", kd_z, v_new)
        return s, o_z

    xs = tuple(jnp.moveaxis(t, 1, 0) for t in (qk, qd, w, u, kd, gl))
    s_fin, o = lax.scan(step, s0, xs)
    return _unchunk(jnp.moveaxis(o, 0, 1)), s_fin


def _chunk_ssd(b, c, x, a, h0):
    bc, cc, xc = _chunked(b), _chunked(c), _chunked(x)
    acum = jnp.cumsum(_chunked(a), axis=-1)
    incl = jnp.tril(jnp.ones((CHUNK, CHUNK), dtype=bool))
    lmat = jnp.exp(jnp.where(incl, acum[..., :, None] - acum[..., None, :], -jnp.inf))
    y_diag = jnp.einsum("nzhis,nzhjs,nzhij,nzhjp->nzhip", cc, bc, lmat, xc)
    st = jnp.einsum("nzhjs,nzhj,nzhjp->nzhps", bc, jnp.exp(acum[..., -1:] - acum), xc)

    def step(hs, inp):
        st_z, dec_z = inp
        return hs * dec_z[..., None, None] + st_z, hs

    h_fin, h_in = lax.scan(step, h0, (jnp.moveaxis(st, 1, 0), jnp.moveaxis(jnp.exp(acum[..., -1]), 1, 0)))
    h_in = jnp.moveaxis(h_in, 0, 1)
    y_off = jnp.einsum("nzhis,nzhps,nzhi->nzhip", cc, h_in, jnp.exp(acum))
    return _unchunk(y_diag + y_off), h_fin


def _bidir(fn, shared, per_dir, s0):
    nb = s0.shape[0]

    def merge(t):
        return t.reshape(2 * nb, *t.shape[2:])

    sh = [merge(jnp.stack([t, t[:, ::-1]])) for t in shared]
    pd = [merge(jnp.stack([t[:, :, 0], t[:, ::-1, 1]])) for t in per_dir]
    out, s_fin = fn(*sh, *pd, merge(jnp.swapaxes(s0, 0, 1)))
    out = out.reshape(2, nb, *out.shape[1:])
    s_fin = jnp.swapaxes(s_fin.reshape(2, nb, *s_fin.shape[1:]), 0, 1)
    return out[0] + out[1][:, ::-1], s_fin


def _mixer(h, s_dn0, s_ssd0, p):
    f32 = jnp.float32
    bsz, l, _ = h.shape
    (q, k, v, dn_gate, dn_beta, dn_alpha, sc_h, sc_b, sc_c, ssd_z, ssd_xbc, ssd_dt) = jnp.split(
        h @ p["w_in"], IN_SPLITS, axis=-1)

    qkv = jax.nn.silu(_dwconv(jnp.concatenate([q, k, v], axis=-1), p["dn_conv_w"])).astype(f32)
    q, k, v = (t.reshape(bsz, l, H_A, -1) for t in jnp.split(qkv, 3, axis=-1))
    q = _l2norm(q) * (DK_A ** -0.5)
    k = _l2norm(k)
    beta = jax.nn.sigmoid(dn_beta.astype(f32)).reshape(bsz, l, 2, H_A)
    g = -jnp.exp(p["dn_a_log"].astype(f32)) * jax.nn.softplus(
        dn_alpha.astype(f32).reshape(bsz, l, 2, H_A) + p["dn_dt_bias"].astype(f32))
    o, s_dn = _bidir(_chunk_gated_delta, (q, k, v), (g, beta), s_dn0)
    o = _rmsnorm(o, p["dn_norm_g"]) * jax.nn.silu(dn_gate.astype(f32)).reshape(bsz, l, H_A, DV_A)
    y_a = o.reshape(bsz, l, A_W).astype(h.dtype)

    y_b = sc_b * _dwconv(sc_c * sc_h, p["sc_conv_w"])

    xbc = jax.nn.silu(_dwconv(ssd_xbc, p["ssd_conv_w"])).astype(f32)
    xs, bm, cm = jnp.split(xbc, [C_W, C_W + G_C * N_C], axis=-1)
    xs = xs.reshape(bsz, l, H_C, P_C)
    bm = jnp.repeat(bm.reshape(bsz, l, G_C, N_C), H_C // G_C, axis=2)
    cm = jnp.repeat(cm.reshape(bsz, l, G_C, N_C), H_C // G_C, axis=2)
    dt = jax.nn.softplus(ssd_dt.astype(f32).reshape(bsz, l, 2, H_C) + p["ssd_dt_bias"].astype(f32))
    a = -jnp.exp(p["ssd_a_log"].astype(f32)) * dt
    xdt = xs[:, :, None] * dt[..., None]
    y, s_ssd = _bidir(_chunk_ssd, (bm, cm), (xdt, a), s_ssd0)
    y = y + p["ssd_d"].astype(f32)[:, None] * xs
    y = _rmsnorm(y.reshape(bsz, l, C_W) * jax.nn.silu(ssd_z.astype(f32)), p["ssd_norm_g"])
    y_c = y.astype(h.dtype)

    out = jnp.concatenate([y_a, y_b, y_c], axis=-1) @ p["w_out"]
    return out, s_dn, s_ssd


def _moe(h, router_w, router_b, w_gate, w_up, w_down):
    shp = h.shape
    t = h.reshape(-1, shp[-1])
    scores = jax.nn.sigmoid((t @ router_w).astype(jnp.float32))
    biased = (scores + router_b.astype(jnp.float32)).reshape(-1, N_EXPERT_GROUPS, EXPERTS_PER_GROUP)
    group_score = jnp.sum(lax.top_k(biased, GROUP_TOP)[0], axis=-1)
    g_sel = jnp.argmax(group_score, axis=-1)
    in_group = jnp.take_along_axis(biased, g_sel[:, None, None], axis=1)[:, 0]
    _, local = lax.top_k(in_group, TOP_K)
    idx = g_sel[:, None] * EXPERTS_PER_GROUP + local
    wts = jnp.take_along_axis(scores, idx, axis=1)
    wts = wts / jnp.sum(wts, axis=-1, keepdims=True)
    combine = jnp.sum(jax.nn.one_hot(idx, N_EXPERTS, dtype=jnp.float32) * wts[..., None], axis=1).astype(h.dtype)
    act = jax.nn.silu(jnp.einsum("td,edf->tef", t, w_gate)) * jnp.einsum("td,edf->tef", t, w_up)
    out = jnp.einsum("tef,te,efd->td", act, combine, w_down)
    return out.reshape(shp)


def _block(x, cond, s_dn0, s_ssd0, p, router_w, router_b):
    mod = (jax.nn.silu(cond) @ p["mod_w"] + p["mod_b"])[:, None, :].astype(x.dtype)
    shift1, scale1, gate1, shift2, scale2, gate2 = jnp.split(mod, 6, axis=-1)
    h = _rmsnorm(x, p["norm1_g"]) * (1 + scale1) + shift1
    y, s_dn, s_ssd = _mixer(h, s_dn0, s_ssd0, p)
    x = x + gate1 * y
    h = _rmsnorm(x, p["norm2_g"]) * (1 + scale2) + shift2
    x = x + gate2 * _moe(h, router_w, router_b, p["exp_w_gate"], p["exp_w_up"], p["exp_w_down"])
    return x, s_dn, s_ssd


def setup_inputs(seed: int = 0) -> dict:
    key = jax.random.key(seed)
    ks = jax.random.split(key, 32)
    f32 = jnp.float32

    def nrm(k, shape, s):
        return jax.random.normal(k, shape, f32) * s

    def a_log(k, nh):
        return jnp.log(jax.random.uniform(k, (DEPTH, 2, nh), f32, 1.0, 8.0))

    def dt_bias(k, nh):
        dt = jnp.exp(jax.random.uniform(k, (DEPTH, 2, nh), f32, math.log(1e-3), math.log(1e-1)))
        return dt + jnp.log(-jnp.expm1(-dt))

    return {
        "x_prompt": nrm(ks[0], (BATCH, SEQ, D_MODEL), 1.0),
        "x_sample": nrm(ks[1], (DEC_BATCH, DEC_SEQ, D_MODEL), 1.0),
        "state_delta": nrm(ks[2], (DEC_BATCH, DEPTH, 2, H_A, DK_A, DV_A), 0.3),
        "state_ssd": nrm(ks[3], (DEC_BATCH, DEPTH, 2, H_C, P_C, N_C), 0.3),
        "c": nrm(ks[4], (DEC_BATCH, D_MODEL), 1.0),
        "c_ctx": nrm(ks[5], (D_MODEL,), 1.0),
        "mod_w": nrm(ks[6], (DEPTH, D_MODEL, 6 * D_MODEL), 0.5 * D_MODEL ** -0.5),
        "mod_b": nrm(ks[7], (DEPTH, 6 * D_MODEL), 0.02),
        "norm1_g": 1.0 + nrm(ks[8], (DEPTH, D_MODEL), 0.02),
        "norm2_g": 1.0 + nrm(ks[9], (DEPTH, D_MODEL), 0.02),
        "w_in": nrm(ks[10], (DEPTH, D_MODEL, D_IN), D_MODEL ** -0.5),
        "w_out": nrm(ks[11], (DEPTH, D_MIX, D_MODEL), D_MIX ** -0.5),
        "dn_conv_w": nrm(ks[12], (DEPTH, CONV_K, 3 * A_W), CONV_K ** -0.5),
        "dn_a_log": a_log(ks[13], H_A),
        "dn_dt_bias": dt_bias(ks[14], H_A),
        "dn_norm_g": 1.0 + nrm(ks[15], (DEPTH, DV_A), 0.02),
        "sc_conv_w": nrm(ks[16], (DEPTH, CONV_K, B_W), CONV_K ** -0.5),
        "ssd_conv_w": nrm(ks[17], (DEPTH, CONV_K, C_W + 2 * G_C * N_C), CONV_K ** -0.5),
        "ssd_a_log": a_log(ks[18], H_C),
        "ssd_dt_bias": dt_bias(ks[19], H_C),
        "ssd_d": 1.0 + nrm(ks[20], (DEPTH, H_C), 0.1),
        "ssd_norm_g": 1.0 + nrm(ks[21], (DEPTH, C_W), 0.02),
        "router_w": nrm(ks[22], (D_MODEL, N_EXPERTS), D_MODEL ** -0.5),
        "router_b": nrm(ks[23], (N_EXPERTS,), 0.01),
        "exp_w_gate": nrm(ks[24], (DEPTH, N_EXPERTS, D_MODEL, D_EXPERT), D_MODEL ** -0.5),
        "exp_w_up": nrm(ks[25], (DEPTH, N_EXPERTS, D_MODEL, D_EXPERT), D_MODEL ** -0.5),
        "exp_w_down": nrm(ks[26], (DEPTH, N_EXPERTS, D_EXPERT, D_MODEL), D_EXPERT ** -0.5),
        "final_norm_g": 1.0 + nrm(ks[27], (D_MODEL,), 0.02),
    }


def reference(x_prompt, x_sample, state_delta, state_ssd, c, c_ctx, mod_w, mod_b, norm1_g, norm2_g,
              w_in, w_out, dn_conv_w, dn_a_log, dn_dt_bias, dn_norm_g, sc_conv_w, ssd_conv_w,
              ssd_a_log, ssd_dt_bias, ssd_d, ssd_norm_g, router_w, router_b, exp_w_gate, exp_w_up,
              exp_w_down, final_norm_g):
    f32 = jnp.float32
    layer_arrays = (("mod_w", mod_w), ("mod_b", mod_b), ("norm1_g", norm1_g), ("norm2_g", norm2_g),
                    ("w_in", w_in), ("w_out", w_out), ("dn_conv_w", dn_conv_w), ("dn_a_log", dn_a_log),
                    ("dn_dt_bias", dn_dt_bias), ("dn_norm_g", dn_norm_g), ("sc_conv_w", sc_conv_w),
                    ("ssd_conv_w", ssd_conv_w), ("ssd_a_log", ssd_a_log), ("ssd_dt_bias", ssd_dt_bias),
                    ("ssd_d", ssd_d), ("ssd_norm_g", ssd_norm_g), ("exp_w_gate", exp_w_gate),
                    ("exp_w_up", exp_w_up), ("exp_w_down", exp_w_down))
    n_ctx = x_prompt.shape[0]
    zero_dn = jnp.zeros((n_ctx, 2, H_A, DK_A, DV_A), f32)
    zero_ssd = jnp.zeros((n_ctx, 2, H_C, P_C, N_C), f32)
    cond_ctx = c_ctx[None, :]

    xp = x_prompt
    xs = x_sample + _grid_pos_embed(x_sample.shape[1], D_MODEL).astype(x_sample.dtype)[None]
    new_dn, new_ssd = [], []
    for l in range(DEPTH):
        p = {name: arr[l] for name, arr in layer_arrays}
        xp, s_dn, s_ssd = _block(xp, cond_ctx, zero_dn, zero_ssd, p, router_w, router_b)
        new_dn.append(s_dn)
        new_ssd.append(s_ssd)
        xs, _, _ = _block(xs, c, state_delta[:, l].astype(f32), state_ssd[:, l].astype(f32), p, router_w, router_b)

    y_prompt = _rmsnorm(xp, final_norm_g)
    y_sample = _rmsnorm(xs, final_norm_g)
    new_state_delta = jnp.stack(new_dn, axis=1)
    new_state_ssd = jnp.stack(new_ssd, axis=1)
    return (y_prompt, y_sample, new_state_delta, new_state_ssd)
```

```python
import functools
import math

import jax
import jax.numpy as jnp
import numpy as np
from jax import lax
from jax.experimental import pallas as pl
from jax.experimental.pallas import tpu as pltpu

F32 = jnp.float32
BF16 = jnp.bfloat16

D_MODEL = 1024
GRID_W = 64
POS_BASE = 10000.0
H_A, DK_A, DV_A = 4, 128, 128
A_W = H_A * DV_A
H_C, P_C, N_C, G_C = 4, 64, 64, 2
C_W = H_C * P_C
B_W = D_MODEL - A_W - C_W
XBC_W = C_W + 2 * G_C * N_C
CHUNK = 64
N_EXPERTS = 16
N_GROUPS = 4
EPG = N_EXPERTS // N_GROUPS
D_EXPERT = 256
EPS = 1e-6
LANES = 128

COL_Q, COL_K, COL_V, COL_GATE = 0, A_W, 2 * A_W, 3 * A_W
COL_SCH = 4 * A_W
COL_SCB = COL_SCH + B_W
COL_SCC = COL_SCB + B_W
COL_Z = COL_SCC + B_W
COL_XBC = COL_Z + C_W
COL_SMALL = COL_XBC + XBC_W
D_IN_PAD = COL_SMALL + LANES
LANE_BETA, LANE_ALPHA, LANE_DT = 0, 2 * H_A, 4 * H_A

VMEM_LIMIT = 56 * 1024 * 1024


def _dot(a, b):
    return jnp.dot(a.astype(BF16), b.astype(BF16), preferred_element_type=F32)


def _dot_nt(a, b):
    return lax.dot_general(a.astype(BF16), b.astype(BF16), (((1,), (1,)), ((), ())), preferred_element_type=F32)


def _dot_tn(a, b):
    return lax.dot_general(a.astype(BF16), b.astype(BF16), (((0,), (0,)), ((), ())), preferred_element_type=F32)


def _dot_f32(a, b):
    return jnp.dot(a, b, precision=lax.Precision.HIGHEST, preferred_element_type=F32)


def _dot_nt_f32(a, b):
    return lax.dot_general(a, b, (((1,), (1,)), ((), ())), precision=lax.Precision.HIGHEST,
                           preferred_element_type=F32)


def _silu(x):
    return x * (1.0 / (1.0 + jnp.exp(-x)))


def _sigmoid(x):
    return 1.0 / (1.0 + jnp.exp(-x))


def _softplus(x):
    return jnp.maximum(x, 0.0) + jnp.log1p(jnp.exp(-jnp.abs(x)))


def _rms(x, g):
    return x * lax.rsqrt(jnp.mean(x * x, axis=-1, keepdims=True) + EPS) * g


def _tile(n, pref):
    t = min(n, pref)
    while n % t:
        t -= 8
    assert t > 0 and t % 8 == 0, (n, pref)
    return t


def _params(sem):
    return pltpu.CompilerParams(dimension_semantics=sem, vmem_limit_bytes=VMEM_LIMIT)


def _mod_kernel(cond_ref, w_ref, b_ref, o_ref):
    s = _silu(cond_ref[...])
    o_ref[0] = _dot_f32(s, w_ref[0]) + b_ref[0]


def _modulation(cond_rows, mod_w, mod_b):
    depth, d, n = mod_w.shape
    r = cond_rows.shape[0]
    tn = _tile(n, 1536)
    return pl.pallas_call(
        _mod_kernel,
        grid=(depth, n // tn),
        in_specs=[pl.BlockSpec((r, d), lambda l, j: (0, 0)),
                  pl.BlockSpec((1, d, tn), lambda l, j: (l, 0, j)),
                  pl.BlockSpec((1, 1, tn), lambda l, j: (l, 0, j))],
        out_specs=pl.BlockSpec((1, r, tn), lambda l, j: (l, 0, j)),
        out_shape=jax.ShapeDtypeStruct((depth, r, n), F32),
        compiler_params=_params(("arbitrary", "arbitrary")),
        name="modulation",
    )(cond_rows, mod_w, mod_b.reshape(depth, 1, n))


def _add_kernel(x_ref, p_ref, o_ref):
    o_ref[0] = x_ref[0] + p_ref[...]


def _add_pos(x, pos):
    nb, l, d = x.shape
    tl = _tile(l, 512)
    return pl.pallas_call(
        _add_kernel,
        grid=(nb, l // tl),
        in_specs=[pl.BlockSpec((1, tl, d), lambda b, i: (b, i, 0)),
                  pl.BlockSpec((tl, d), lambda b, i: (i, 0))],
        out_specs=pl.BlockSpec((1, tl, d), lambda b, i: (b, i, 0)),
        out_shape=jax.ShapeDtypeStruct(x.shape, x.dtype),
        compiler_params=_params(("arbitrary", "arbitrary")),
        name="add_pos",
    )(x, pos)


def _grid_pos_embed(n_tok, dim):
    rows = n_tok // GRID_W
    rr, cc = jnp.meshgrid(jnp.arange(rows, dtype=F32), jnp.arange(GRID_W, dtype=F32), indexing="ij")
    quarter = dim // 4
    omega = 1.0 / (POS_BASE ** (jnp.arange(quarter, dtype=F32) / quarter))
    ang_r = rr.reshape(-1, 1) * omega
    ang_c = cc.reshape(-1, 1) * omega
    return jnp.concatenate([jnp.sin(ang_r), jnp.cos(ang_r), jnp.sin(ang_c), jnp.cos(ang_c)], axis=-1)


def _inproj_kernel(x_ref, mod_ref, g_ref, w_ref, o_ref):
    m = mod_ref[0]
    shift, scale = m[0:1, :], m[1:2, :]
    h = _rms(x_ref[...], g_ref[...]) * (1.0 + scale) + shift
    o_ref[...] = _dot(h, w_ref[...])


def _inproj(x, mod6, seq_rows, norm_g, w_in_r):
    t, d = x.shape
    tm = _tile(seq_rows, 256)
    return pl.pallas_call(
        _inproj_kernel,
        grid=(t // tm,),
        in_specs=[pl.BlockSpec((tm, d), lambda i: (i, 0)),
                  pl.BlockSpec((1, 6, d), lambda i: ((i * tm) // seq_rows, 0, 0)),
                  pl.BlockSpec((1, d), lambda i: (0, 0)),
                  pl.BlockSpec((d, D_IN_PAD), lambda i: (0, 0))],
        out_specs=pl.BlockSpec((tm, D_IN_PAD), lambda i: (i, 0)),
        out_shape=jax.ShapeDtypeStruct((t, D_IN_PAD), F32),
        compiler_params=_params(("arbitrary",)),
        name="inproj",
    )(x, mod6, norm_g.reshape(1, d), w_in_r)


def _conv3(load, r0, z, nc, seq_len, w):
    cur = load(pl.ds(r0, CHUNK))
    prev = load(pl.ds(pl.multiple_of(jnp.maximum(r0 - 8, 0), 8), 8))[7:8, :]
    nxt = load(pl.ds(pl.multiple_of(jnp.minimum(r0 + CHUNK, seq_len - 8), 8), 8))[0:1, :]
    prev = jnp.where(z > 0, prev, 0.0)
    nxt = jnp.where(z < nc - 1, nxt, 0.0)
    ri = lax.broadcasted_iota(jnp.int32, cur.shape, 0)
    x_prev = jnp.where(ri == 0, prev, pltpu.roll(cur, 1, 0))
    x_next = jnp.where(ri == CHUNK - 1, nxt, pltpu.roll(cur, CHUNK - 1, 0))
    return w[0:1, :] * x_prev + w[1:2, :] * cur + w[2:3, :] * x_next


def _chunk_gates(sm, alog, bias, tril):
    sp = _softplus(sm + bias)
    g = -jnp.exp(alog) * sp
    pre = _dot_f32(tril, g)
    tot = pre[CHUNK - 1:CHUNK, :]
    return sp, g, pre, tot


def _mixer_kernel(*refs, seq_len, zero_init, emit_state):
    refs = list(refs)
    proj, dnw, scw, ssw, lanev, ssdv = refs[:6]
    k = 6
    if not zero_init:
        sdn0, sssd0 = refs[k:k + 2]
        k += 2
    ycat = refs[k]
    k += 1
    if emit_state:
        sdn_out, sssd_out = refs[k:k + 2]
        k += 2
    qkv_s, xbc_s, o_s, y_s, st_s, hs_s = refs[k:]

    nc = seq_len // CHUNK
    ri = lax.broadcasted_iota(jnp.int32, (CHUNK, CHUNK), 0)
    ci = lax.broadcasted_iota(jnp.int32, (CHUNK, CHUNK), 1)
    tril = (ri >= ci).astype(F32)
    eye = (ri == ci).astype(F32)
    r128 = lax.broadcasted_iota(jnp.int32, (LANES, LANES), 0)
    c128 = lax.broadcasted_iota(jnp.int32, (LANES, LANES), 1)
    eye128 = (r128 == c128).astype(F32)
    incl = (ri >= ci, ri <= ci)
    strict = (ri > ci, ri < ci)
    n_lvl = int(math.log2(CHUNK))
    lvl = [((ri >> s) == (ci >> s)) & ((ri >> (s - 1)) != (ci >> (s - 1))) for s in range(1, n_lvl + 1)]
    alog = lanev[0:1, :]
    bias = lanev[1:2, :]

    def prep(z, carry):
        r0 = pl.multiple_of(z * CHUNK, CHUNK)
        rows = pl.ds(r0, CHUNK)
        for j in range(3 * H_A):
            c0 = j * LANES
            a = _silu(_conv3(lambda rs: proj[0, rs, c0:c0 + LANES], r0, z, nc, seq_len, dnw[:, c0:c0 + LANES]))
            if j < 2 * H_A:
                a = a * lax.rsqrt(jnp.sum(a * a, axis=-1, keepdims=True) + EPS)
            if j < H_A:
                a = a * (DK_A ** -0.5)
            qkv_s[rows, c0:c0 + LANES] = a
        for j in range(B_W // LANES):
            c0 = j * LANES
            cv = _conv3(lambda rs: proj[0, rs, COL_SCC + c0:COL_SCC + c0 + LANES]
                        * proj[0, rs, COL_SCH + c0:COL_SCH + c0 + LANES],
                        r0, z, nc, seq_len, scw[:, c0:c0 + LANES])
            yb = proj[0, rows, COL_SCB + c0:COL_SCB + c0 + LANES] * cv
            ycat[0, rows, A_W + c0:A_W + c0 + LANES] = yb.astype(ycat.dtype)
        for j in range(XBC_W // LANES):
            c0 = j * LANES
            a = _silu(_conv3(lambda rs: proj[0, rs, COL_XBC + c0:COL_XBC + c0 + LANES], r0, z, nc, seq_len,
                             ssw[:, c0:c0 + LANES]))
            xbc_s[rows, c0:c0 + LANES] = a
        return carry

    lax.fori_loop(0, nc, prep, 0)

    for d in range(2):
        for h in range(H_A):
            st_s[d * H_A + h] = jnp.zeros((DK_A, DV_A), F32) if zero_init else sdn0[0, d, h]
        for h in range(H_C):
            hs_s[d * H_C + h] = jnp.zeros((P_C, N_C), F32) if zero_init else sssd0[0, d, h]

    def dir_chunk(z, d):
        zc = z if d == 0 else nc - 1 - z
        r0 = pl.multiple_of(zc * CHUNK, CHUNK)
        rows = pl.ds(r0, CHUNK)
        sm = proj[0, rows, COL_SMALL:COL_SMALL + LANES]
        sp, g, pre, tot = _chunk_gates(sm, alog, bias, tril)
        gc = pre if d == 0 else tot - pre + g
        gc_rows = _dot_nt_f32(eye128, gc)
        return rows, sm, sp, gc, gc_rows, tot

    def delta(z, carry):
        for d in range(2):
            rows, sm, _, gc, gc_rows, tot = dir_chunk(z, d)
            beta = _sigmoid(sm)
            for h in range(H_A):
                ln = LANE_ALPHA + d * H_A + h
                a_col = gc[:, ln:ln + 1]
                a_row = gc_rows[ln:ln + 1, :]
                t_col = tot[:, ln:ln + 1]
                decay = jnp.exp(jnp.where(incl[d], a_col - a_row, -1e30))
                q_h = qkv_s[rows, COL_Q + h * DK_A:COL_Q + (h + 1) * DK_A]
                k_h = qkv_s[rows, COL_K + h * DK_A:COL_K + (h + 1) * DK_A]
                v_h = qkv_s[rows, COL_V + h * DV_A:COL_V + (h + 1) * DV_A]
                b_col = beta[:, LANE_BETA + d * H_A + h:LANE_BETA + d * H_A + h + 1]
                kb = k_h * b_col
                m = jnp.where(strict[d], _dot_nt(kb, k_h) * decay, 0.0)
                t_inv = eye - jnp.where(lvl[0], m, 0.0)
                for s in range(1, n_lvl):
                    t_inv = t_inv - _dot(t_inv, _dot(jnp.where(lvl[s], m, 0.0), t_inv))
                eg = jnp.exp(a_col)
                uw = _dot(t_inv, jnp.concatenate([v_h * b_col, kb * eg], axis=1))
                u, w = uw[:, :DV_A], uw[:, DV_A:]
                qk = _dot_nt(q_h, k_h) * decay
                kd = k_h * jnp.exp(t_col - a_col)
                s_prev = st_s[d * H_A + h]
                ws_qs = _dot(jnp.concatenate([w, q_h * eg], axis=0), s_prev)
                v_new = u - ws_qs[:CHUNK]
                o_s[d, rows, h * DV_A:(h + 1) * DV_A] = ws_qs[CHUNK:] + _dot(qk, v_new)
                st_s[d * H_A + h] = s_prev * jnp.exp(t_col) + _dot_tn(kd, v_new)
        return carry

    lax.fori_loop(0, nc, delta, 0)

    def ssd(z, carry):
        for d in range(2):
            rows, _, sp, gc, gc_rows, tot = dir_chunk(z, d)
            for grp in range(G_C):
                b_g = xbc_s[rows, C_W + grp * N_C:C_W + (grp + 1) * N_C]
                c_g = xbc_s[rows, C_W + G_C * N_C + grp * N_C:C_W + G_C * N_C + (grp + 1) * N_C]
                cb = _dot_nt(c_g, b_g)
                for h in range(grp * (H_C // G_C), (grp + 1) * (H_C // G_C)):
                    ln = LANE_DT + d * H_C + h
                    a_col = gc[:, ln:ln + 1]
                    a_row = gc_rows[ln:ln + 1, :]
                    t_col = tot[:, ln:ln + 1]
                    lmat = jnp.exp(jnp.where(incl[d], a_col - a_row, -1e30))
                    xdt = xbc_s[rows, h * P_C:(h + 1) * P_C] * sp[:, ln:ln + 1]
                    h_prev = hs_s[d * H_C + h]
                    y = _dot(cb * lmat, xdt) + _dot_nt(c_g, h_prev) * jnp.exp(a_col)
                    y_s[d, rows, h * P_C:(h + 1) * P_C] = y
                    hs_s[d * H_C + h] = h_prev * jnp.exp(t_col) + _dot_tn(xdt, b_g * jnp.exp(t_col - a_col))
        return carry

    lax.fori_loop(0, nc, ssd, 0)

    def finish(z, carry):
        r0 = pl.multiple_of(z * CHUNK, CHUNK)
        rows = pl.ds(r0, CHUNK)
        for h in range(H_A):
            cs = slice(h * DV_A, (h + 1) * DV_A)
            o = _rms(o_s[0, rows, cs] + o_s[1, rows, cs], lanev[2:3, :])
            o = o * _silu(proj[0, rows, COL_GATE + h * DV_A:COL_GATE + (h + 1) * DV_A])
            ycat[0, rows, cs] = o.astype(ycat.dtype)
        y = y_s[0, rows, :] + y_s[1, rows, :] + ssdv[0:1, :] * xbc_s[rows, 0:C_W]
        y = _rms(y * _silu(proj[0, rows, COL_Z:COL_Z + C_W]), ssdv[1:2, :])
        ycat[0, rows, A_W + B_W:] = y.astype(ycat.dtype)
        return carry

    lax.fori_loop(0, nc, finish, 0)

    if emit_state:
        for d in range(2):
            for h in range(H_A):
                sdn_out[0, d, h] = st_s[d * H_A + h]
            for h in range(H_C):
                sssd_out[0, d, h] = hs_s[d * H_C + h]


def _mixer(proj, dn_conv_w, sc_conv_w, ssd_conv_w, lanev, ssdv, s_dn0, s_ssd0, emit_state):
    nb, seq_len, _ = proj.shape
    zero_init = s_dn0 is None
    full = lambda a: pl.BlockSpec(a.shape, lambda b: (0,) * a.ndim)
    args = [proj, dn_conv_w, sc_conv_w, ssd_conv_w, lanev, ssdv]
    in_specs = [pl.BlockSpec((1, seq_len, D_IN_PAD), lambda b: (b, 0, 0), pipeline_mode=pl.Buffered(1))
                if seq_len > 512 else pl.BlockSpec((1, seq_len, D_IN_PAD), lambda b: (b, 0, 0)),
                full(dn_conv_w), full(sc_conv_w), full(ssd_conv_w), full(lanev), full(ssdv)]
    if not zero_init:
        args += [s_dn0, s_ssd0]
        in_specs += [pl.BlockSpec((1, 2, H_A, DK_A, DV_A), lambda b: (b, 0, 0, 0, 0)),
                     pl.BlockSpec((1, 2, H_C, P_C, N_C), lambda b: (b, 0, 0, 0, 0))]
    out_shape = [jax.ShapeDtypeStruct((nb, seq_len, D_MODEL), BF16)]
    out_specs = [pl.BlockSpec((1, seq_len, D_MODEL), lambda b: (b, 0, 0))]
    if emit_state:
        out_shape += [jax.ShapeDtypeStruct((nb, 2, H_A, DK_A, DV_A), F32),
                      jax.ShapeDtypeStruct((nb, 2, H_C, P_C, N_C), F32)]
        out_specs += [pl.BlockSpec((1, 2, H_A, DK_A, DV_A), lambda b: (b, 0, 0, 0, 0)),
                      pl.BlockSpec((1, 2, H_C, P_C, N_C), lambda b: (b, 0, 0, 0, 0))]
    scratch = [pltpu.VMEM((seq_len, 3 * A_W), F32), pltpu.VMEM((seq_len, XBC_W), F32),
               pltpu.VMEM((2, seq_len, A_W), F32), pltpu.VMEM((2, seq_len, C_W), F32),
               pltpu.VMEM((2 * H_A, DK_A, DV_A), F32), pltpu.VMEM((2 * H_C, P_C, N_C), F32)]
    return pl.pallas_call(
        functools.partial(_mixer_kernel, seq_len=seq_len, zero_init=zero_init, emit_state=emit_state),
        grid=(nb,),
        in_specs=in_specs,
        out_specs=out_specs,
        out_shape=out_shape,
        scratch_shapes=scratch,
        compiler_params=_params(("arbitrary",)),
        name="mixer",
    )(*args)


def _top2_sum(a, b, c, d):
    hi1, lo1 = jnp.maximum(a, b), jnp.minimum(a, b)
    hi2, lo2 = jnp.maximum(c, d), jnp.minimum(c, d)
    return jnp.maximum(hi1, hi2) + jnp.maximum(jnp.minimum(hi1, hi2), jnp.maximum(lo1, lo2))


def _outproj_kernel(y_ref, x_ref, mod_ref, w_ref, g_ref, rwt_ref, rb_ref, x1_ref, h2_ref, comb_ref):
    m = mod_ref[0]
    gate1, shift2, scale2 = m[2:3, :], m[3:4, :], m[4:5, :]
    x1 = x_ref[...] + gate1 * _dot(y_ref[...], w_ref[...])
    x1_ref[...] = x1
    h2 = _rms(x1, g_ref[...]) * (1.0 + scale2) + shift2
    h2_ref[...] = h2.astype(h2_ref.dtype)

    scores = _sigmoid(_dot_nt_f32(rwt_ref[...], h2))
    biased = scores + rb_ref[...]
    sc = [scores[e:e + 1, :] for e in range(N_EXPERTS)]
    bi = [biased[e:e + 1, :] for e in range(N_EXPERTS)]
    gs = [_top2_sum(*bi[EPG * g:EPG * (g + 1)]) for g in range(N_GROUPS)]
    gmax = functools.reduce(jnp.maximum, gs)
    first = []
    taken = None
    for g in range(N_GROUPS):
        hit = gs[g] == gmax
        if taken is None:
            first.append(hit)
            taken = hit
        else:
            first.append(hit & jnp.logical_not(taken))
            taken = taken | hit

    def pick(vals, j):
        out = vals[EPG * (N_GROUPS - 1) + j]
        for g in range(N_GROUPS - 2, -1, -1):
            out = jnp.where(first[g], vals[EPG * g + j], out)
        return out

    ib = [pick(bi, j) for j in range(EPG)]
    isc = [pick(sc, j) for j in range(EPG)]
    sel = []
    for j in range(EPG):
        cnt = jnp.zeros_like(ib[j])
        for i in range(EPG):
            if i == j:
                continue
            ahead = (ib[i] > ib[j]) | ((ib[i] == ib[j]) if i < j else False)
            cnt = cnt + jnp.where(ahead, 1.0, 0.0)
        sel.append(cnt < 2.0)
    wj = [jnp.where(sel[j], isc[j], 0.0) for j in range(EPG)]
    denom = functools.reduce(lambda a, b: a + b, wj)
    for g in range(N_GROUPS):
        for j in range(EPG):
            comb_ref[EPG * g + j:EPG * g + j + 1, :] = jnp.where(first[g], wj[j] / denom, 0.0)


def _outproj(ycat, x, mod6, seq_rows, w_out, norm_g, router_wt, router_b):
    t, d = x.shape
    tm = _tile(seq_rows, 256)
    return pl.pallas_call(
        _outproj_kernel,
        grid=(t // tm,),
        in_specs=[pl.BlockSpec((tm, d), lambda i: (i, 0)),
                  pl.BlockSpec((tm, d), lambda i: (i, 0)),
                  pl.BlockSpec((1, 6, d), lambda i: ((i * tm) // seq_rows, 0, 0)),
                  pl.BlockSpec((d, d), lambda i: (0, 0)),
                  pl.BlockSpec((1, d), lambda i: (0, 0)),
                  pl.BlockSpec((N_EXPERTS, d), lambda i: (0, 0)),
                  pl.BlockSpec((N_EXPERTS, 1), lambda i: (0, 0))],
        out_specs=[pl.BlockSpec((tm, d), lambda i: (i, 0)),
                   pl.BlockSpec((tm, d), lambda i: (i, 0)),
                   pl.BlockSpec((N_EXPERTS, tm), lambda i: (0, i))],
        out_shape=[jax.ShapeDtypeStruct((t, d), F32),
                   jax.ShapeDtypeStruct((t, d), BF16),
                   jax.ShapeDtypeStruct((N_EXPERTS, t), F32)],
        compiler_params=_params(("arbitrary",)),
        name="outproj_route",
    )(ycat, x, mod6, w_out, norm_g.reshape(1, d), router_wt, router_b.reshape(N_EXPERTS, 1))


def _moe_kernel(h_ref, comb_ref, wg_ref, wu_ref, wd_ref, x1_ref, mod_ref, fg_ref, o_ref, acc_ref, *, final):
    e = pl.program_id(1)

    @pl.when(e == 0)
    def _():
        acc_ref[...] = jnp.zeros_like(acc_ref)

    h = h_ref[...]
    comb = comb_ref[...]
    lane = lax.broadcasted_iota(jnp.int32, comb.shape, 1)
    ce = jnp.sum(jnp.where(lane == e, comb, 0.0), axis=1, keepdims=True)
    act = _silu(_dot(h, wg_ref[0])) * _dot(h, wu_ref[0]) * ce
    acc_ref[...] += _dot(act, wd_ref[0])

    @pl.when(e == pl.num_programs(1) - 1)
    def _():
        gate2 = mod_ref[0][5:6, :]
        x2 = x1_ref[...] + gate2 * acc_ref[...]
        o_ref[...] = _rms(x2, fg_ref[...]) if final else x2


def _moe(h2, comb, w_gate, w_up, w_down, x1, mod6, seq_rows, final_g, final):
    t, d = x1.shape
    tm = _tile(seq_rows, 1024)
    return pl.pallas_call(
        functools.partial(_moe_kernel, final=final),
        grid=(t // tm, N_EXPERTS),
        in_specs=[pl.BlockSpec((tm, d), lambda i, e: (i, 0)),
                  pl.BlockSpec((tm, N_EXPERTS), lambda i, e: (i, 0)),
                  pl.BlockSpec((1, d, D_EXPERT), lambda i, e: (e, 0, 0)),
                  pl.BlockSpec((1, d, D_EXPERT), lambda i, e: (e, 0, 0)),
                  pl.BlockSpec((1, D_EXPERT, d), lambda i, e: (e, 0, 0)),
                  pl.BlockSpec((tm, d), lambda i, e: (i, 0)),
                  pl.BlockSpec((1, 6, d), lambda i, e: ((i * tm) // seq_rows, 0, 0)),
                  pl.BlockSpec((1, d), lambda i, e: (0, 0))],
        out_specs=pl.BlockSpec((tm, d), lambda i, e: (i, 0)),
        out_shape=jax.ShapeDtypeStruct((t, d), F32),
        scratch_shapes=[pltpu.VMEM((tm, d), F32)],
        compiler_params=_params(("arbitrary", "arbitrary")),
        name="experts",
    )(h2, comb, w_gate, w_up, w_down, x1, mod6, final_g.reshape(1, d))


def _lane_row(*pieces):
    row = jnp.zeros((LANES,), F32)
    for lane, vals in pieces:
        row = lax.dynamic_update_slice(row, vals.reshape(-1).astype(F32), (lane,))
    return row


def kernel(x_prompt, x_sample, state_delta, state_ssd, c, c_ctx, mod_w, mod_b, norm1_g, norm2_g, w_in, w_out,
           dn_conv_w, dn_a_log, dn_dt_bias, dn_norm_g, sc_conv_w, ssd_conv_w, ssd_a_log, ssd_dt_bias, ssd_d,
           ssd_norm_g, router_w, router_b, exp_w_gate, exp_w_up, exp_w_down, final_norm_g):
    depth = mod_w.shape[0]
    n_ctx, seq, d = x_prompt.shape
    n_dec, dec_seq, _ = x_sample.shape
    assert seq % CHUNK == 0 and dec_seq % CHUNK == 0 and d == D_MODEL

    n_rows = -(-(1 + n_dec) // 8) * 8
    cond_rows = jnp.zeros((n_rows, d), F32).at[0].set(c_ctx).at[1:1 + n_dec].set(c)
    mod = _modulation(cond_rows, mod_w, mod_b).reshape(depth, n_rows, 6, d)

    sp = np.cumsum((A_W, A_W, A_W, A_W, 2 * H_A, 2 * H_A, B_W, B_W, B_W, C_W, XBC_W, 2 * H_C))
    w_in_r = jnp.concatenate(
        [w_in[:, :, :sp[3]], w_in[:, :, sp[5]:sp[10]], w_in[:, :, sp[3]:sp[5]], w_in[:, :, sp[10]:],
         jnp.zeros((depth, d, D_IN_PAD - int(sp[-1])), w_in.dtype)], axis=-1).astype(BF16)
    w_out_b = w_out.astype(BF16)
    wg_b, wu_b, wd_b = exp_w_gate.astype(BF16), exp_w_up.astype(BF16), exp_w_down.astype(BF16)
    router_wt = router_w.T

    xp = x_prompt.reshape(n_ctx * seq, d)
    xs = _add_pos(x_sample, _grid_pos_embed(dec_seq, d).astype(x_sample.dtype)).reshape(n_dec * dec_seq, d)

    new_dn, new_ssd = [], []
    for l in range(depth):
        lanev = jnp.zeros((8, LANES), F32)
        lanev = lanev.at[0].set(_lane_row((LANE_ALPHA, dn_a_log[l]), (LANE_DT, ssd_a_log[l])))
        lanev = lanev.at[1].set(_lane_row((LANE_ALPHA, dn_dt_bias[l]), (LANE_DT, ssd_dt_bias[l])))
        lanev = lanev.at[2].set(dn_norm_g[l].astype(F32))
        ssdv = jnp.zeros((8, C_W), F32).at[0].set(jnp.repeat(ssd_d[l].astype(F32), P_C)).at[1].set(ssd_norm_g[l])
        final = l == depth - 1

        def block(x, mod6, nb, seq_len, seq_rows, s_dn0, s_ssd0, emit_state):
            proj = _inproj(x, mod6, seq_rows, norm1_g[l], w_in_r[l]).reshape(nb, seq_len, D_IN_PAD)
            outs = _mixer(proj, dn_conv_w[l], sc_conv_w[l], ssd_conv_w[l], lanev, ssdv, s_dn0, s_ssd0, emit_state)
            ycat = outs[0].reshape(nb * seq_len, d)
            x1, h2, comb_t = _outproj(ycat, x, mod6, seq_rows, w_out_b[l], norm2_g[l], router_wt, router_b)
            x2 = _moe(h2, comb_t.T, wg_b[l], wu_b[l], wd_b[l], x1, mod6, seq_rows, final_norm_g, final)
            return x2, outs[1:]

        xp, (s_dn, s_ssd) = block(xp, mod[l, 0:1], n_ctx, seq, n_ctx * seq, None, None, True)
        new_dn.append(s_dn)
        new_ssd.append(s_ssd)
        xs, _ = block(xs, mod[l, 1:1 + n_dec], n_dec, dec_seq, dec_seq,
                      state_delta[:, l].astype(F32), state_ssd[:, l].astype(F32), False)

    return (xp.reshape(n_ctx, seq, d), xs.reshape(n_dec, dec_seq, d),
            jnp.stack(new_dn, axis=1), jnp.stack(new_ssd, axis=1))
```

```python
import functools
import math

import jax
import jax.numpy as jnp
import numpy as np
from jax import lax
from jax.experimental import pallas as pl
from jax.experimental.pallas import tpu as pltpu

F32 = jnp.float32
BF16 = jnp.bfloat16

D_MODEL = 1024
GRID_W = 64
POS_BASE = 10000.0
H_A, DK_A, DV_A = 4, 128, 128
A_W = H_A * DV_A
H_C, P_C, N_C, G_C = 4, 64, 64, 2
HPG = H_C // G_C
C_W = H_C * P_C
B_W = D_MODEL - A_W - C_W
XBC_W = C_W + 2 * G_C * N_C
CHUNK = 64
N_EXPERTS = 16
N_GROUPS = 4
EPG = N_EXPERTS // N_GROUPS
D_EXPERT = 256
EPS = 1e-6
LANES = 128

COL_Q, COL_K, COL_V, COL_GATE = 0, A_W, 2 * A_W, 3 * A_W
COL_SCH = 4 * A_W
COL_SCB = COL_SCH + B_W
COL_SCC = COL_SCB + B_W
COL_Z = COL_SCC + B_W
COL_XBC = COL_Z + C_W
COL_SMALL = COL_XBC + XBC_W
D_IN_PAD = COL_SMALL + LANES
LANE_BETA, LANE_ALPHA, LANE_DT = 0, 2 * H_A, 4 * H_A

VMEM_LIMIT = 56 * 1024 * 1024


def _dot(a, b):
    return jnp.dot(a.astype(BF16), b.astype(BF16), preferred_element_type=F32)


def _dot_nt(a, b):
    return lax.dot_general(a.astype(BF16), b.astype(BF16), (((1,), (1,)), ((), ())), preferred_element_type=F32)


def _dot_tn(a, b):
    return lax.dot_general(a.astype(BF16), b.astype(BF16), (((0,), (0,)), ((), ())), preferred_element_type=F32)


def _dot_f32(a, b):
    return jnp.dot(a, b, precision=lax.Precision.HIGHEST, preferred_element_type=F32)


def _dot_nt_f32(a, b):
    return lax.dot_general(a, b, (((1,), (1,)), ((), ())), precision=lax.Precision.HIGHEST,
                           preferred_element_type=F32)


def _silu(x):
    return x * (1.0 / (1.0 + jnp.exp(-x)))


def _sigmoid(x):
    return 1.0 / (1.0 + jnp.exp(-x))


def _softplus(x):
    return jnp.maximum(x, 0.0) + jnp.log1p(jnp.exp(-jnp.abs(x)))


def _rms(x, g):
    return x * lax.rsqrt(jnp.mean(x * x, axis=-1, keepdims=True) + EPS) * g


def _tile(n, pref):
    t = min(n, pref)
    while n % t:
        t -= 8
    assert t > 0 and t % 8 == 0, (n, pref)
    return t


def _params(sem):
    return pltpu.CompilerParams(dimension_semantics=sem, vmem_limit_bytes=VMEM_LIMIT)


def _mod_kernel(cond_ref, w_ref, b_ref, o_ref):
    s = _silu(cond_ref[...])
    o_ref[0] = _dot_f32(s, w_ref[0]) + b_ref[0]


def _modulation(cond_rows, mod_w, mod_b):
    depth, d, n = mod_w.shape
    r = cond_rows.shape[0]
    tn = _tile(n, 1536)
    return pl.pallas_call(
        _mod_kernel,
        grid=(depth, n // tn),
        in_specs=[pl.BlockSpec((r, d), lambda l, j: (0, 0)),
                  pl.BlockSpec((1, d, tn), lambda l, j: (l, 0, j)),
                  pl.BlockSpec((1, 1, tn), lambda l, j: (l, 0, j))],
        out_specs=pl.BlockSpec((1, r, tn), lambda l, j: (l, 0, j)),
        out_shape=jax.ShapeDtypeStruct((depth, r, n), F32),
        compiler_params=_params(("arbitrary", "arbitrary")),
        name="modulation",
    )(cond_rows, mod_w, mod_b.reshape(depth, 1, n))


def _add_kernel(x_ref, p_ref, o_ref):
    o_ref[0] = x_ref[0] + p_ref[...]


def _add_pos(x, pos):
    nb, l, d = x.shape
    tl = _tile(l, 512)
    return pl.pallas_call(
        _add_kernel,
        grid=(nb, l // tl),
        in_specs=[pl.BlockSpec((1, tl, d), lambda b, i: (b, i, 0)),
                  pl.BlockSpec((tl, d), lambda b, i: (i, 0))],
        out_specs=pl.BlockSpec((1, tl, d), lambda b, i: (b, i, 0)),
        out_shape=jax.ShapeDtypeStruct(x.shape, x.dtype),
        compiler_params=_params(("arbitrary", "arbitrary")),
        name="add_pos",
    )(x, pos)


def _grid_pos_embed(n_tok, dim):
    rows = n_tok // GRID_W
    rr, cc = jnp.meshgrid(jnp.arange(rows, dtype=F32), jnp.arange(GRID_W, dtype=F32), indexing="ij")
    quarter = dim // 4
    omega = 1.0 / (POS_BASE ** (jnp.arange(quarter, dtype=F32) / quarter))
    ang_r = rr.reshape(-1, 1) * omega
    ang_c = cc.reshape(-1, 1) * omega
    return jnp.concatenate([jnp.sin(ang_r), jnp.cos(ang_r), jnp.sin(ang_c), jnp.cos(ang_c)], axis=-1)


def _inproj_kernel(x_ref, mod_ref, g_ref, w_ref, o_ref):
    m = mod_ref[0]
    shift, scale = m[0:1, :], m[1:2, :]
    h = _rms(x_ref[...], g_ref[...]) * (1.0 + scale) + shift
    o_ref[...] = _dot(h, w_ref[...])


def _inproj(x, mod6, seq_rows, norm_g, w_in_r):
    t, d = x.shape
    tm = _tile(seq_rows, 256)
    return pl.pallas_call(
        _inproj_kernel,
        grid=(t // tm,),
        in_specs=[pl.BlockSpec((tm, d), lambda i: (i, 0)),
                  pl.BlockSpec((1, 6, d), lambda i: ((i * tm) // seq_rows, 0, 0)),
                  pl.BlockSpec((1, d), lambda i: (0, 0)),
                  pl.BlockSpec((d, D_IN_PAD), lambda i: (0, 0))],
        out_specs=pl.BlockSpec((tm, D_IN_PAD), lambda i: (i, 0)),
        out_shape=jax.ShapeDtypeStruct((t, D_IN_PAD), F32),
        compiler_params=_params(("arbitrary",)),
        name="inproj",
    )(x, mod6, norm_g.reshape(1, d), w_in_r)


def _conv3(load, r0, z, nc, seq_len, w):
    cur = load(pl.ds(r0, CHUNK))
    prev = load(pl.ds(pl.multiple_of(jnp.maximum(r0 - 8, 0), 8), 8))[7:8, :]
    nxt = load(pl.ds(pl.multiple_of(jnp.minimum(r0 + CHUNK, seq_len - 8), 8), 8))[0:1, :]
    prev = jnp.where(z > 0, prev, 0.0)
    nxt = jnp.where(z < nc - 1, nxt, 0.0)
    ri = lax.broadcasted_iota(jnp.int32, cur.shape, 0)
    x_prev = jnp.where(ri == 0, prev, pltpu.roll(cur, 1, 0))
    x_next = jnp.where(ri == CHUNK - 1, nxt, pltpu.roll(cur, CHUNK - 1, 0))
    return w[0:1, :] * x_prev + w[1:2, :] * cur + w[2:3, :] * x_next


def _mixer_kernel(*refs, seq_len, zero_init, emit_state):
    refs = list(refs)
    proj, dnw, scw, ssw, lanev, ssdv = refs[:6]
    k = 6
    if not zero_init:
        sdn0, sssd0 = refs[k:k + 2]
        k += 2
    ycat = refs[k]
    k += 1
    if emit_state:
        sdn_out, sssd_out = refs[k:k + 2]
        k += 2
    (qkv_s, xbc_s, gc_s, gcrow_s, tot_s, sp_s, beta_s, u_s, wq_s, kd_s, qk_s, o_s, y_s, st_s, hs_s) = refs[k:]

    nc = seq_len // CHUNK
    ri = lax.broadcasted_iota(jnp.int32, (CHUNK, CHUNK), 0)
    ci = lax.broadcasted_iota(jnp.int32, (CHUNK, CHUNK), 1)
    tril = (ri >= ci).astype(F32)
    eye = (ri == ci).astype(F32)
    r128 = lax.broadcasted_iota(jnp.int32, (LANES, LANES), 0)
    c128 = lax.broadcasted_iota(jnp.int32, (LANES, LANES), 1)
    eye128 = (r128 == c128).astype(F32)
    incl = (ri >= ci, ri <= ci)
    strict = (ri > ci, ri < ci)
    n_lvl = int(math.log2(CHUNK))
    lvl = [((ri >> s) == (ci >> s)) & ((ri >> (s - 1)) != (ci >> (s - 1))) for s in range(1, n_lvl + 1)]
    alog = lanev[0:1, :]
    bias = lanev[1:2, :]

    def chunk_rows(z):
        return pl.ds(pl.multiple_of(z * CHUNK, CHUNK), CHUNK)

    def prep(z, carry):
        r0 = pl.multiple_of(z * CHUNK, CHUNK)
        rows = pl.ds(r0, CHUNK)
        for j in range(3 * H_A):
            c0 = j * LANES
            a = _silu(_conv3(lambda rs: proj[0, rs, c0:c0 + LANES], r0, z, nc, seq_len, dnw[:, c0:c0 + LANES]))
            if j < 2 * H_A:
                a = a * lax.rsqrt(jnp.sum(a * a, axis=-1, keepdims=True) + EPS)
            if j < H_A:
                a = a * (DK_A ** -0.5)
            qkv_s[rows, c0:c0 + LANES] = a
        for j in range(B_W // LANES):
            c0 = j * LANES
            cv = _conv3(lambda rs: proj[0, rs, COL_SCC + c0:COL_SCC + c0 + LANES]
                        * proj[0, rs, COL_SCH + c0:COL_SCH + c0 + LANES],
                        r0, z, nc, seq_len, scw[:, c0:c0 + LANES])
            yb = proj[0, rows, COL_SCB + c0:COL_SCB + c0 + LANES] * cv
            ycat[0, rows, A_W + c0:A_W + c0 + LANES] = yb.astype(ycat.dtype)
        for j in range(XBC_W // LANES):
            c0 = j * LANES
            a = _silu(_conv3(lambda rs: proj[0, rs, COL_XBC + c0:COL_XBC + c0 + LANES], r0, z, nc, seq_len,
                             ssw[:, c0:c0 + LANES]))
            xbc_s[rows, c0:c0 + LANES] = a
        sm = proj[0, rows, COL_SMALL:COL_SMALL + LANES]
        sp = _softplus(sm + bias)
        g = -jnp.exp(alog) * sp
        pre = _dot_f32(tril, g)
        tot = pre[CHUNK - 1:CHUNK, :]
        suf = tot - pre + g
        sp_s[rows, :] = sp
        beta_s[rows, :] = _sigmoid(sm)
        gc_s[0, rows, :] = pre
        gc_s[1, rows, :] = suf
        gcrow_s[0, z] = _dot_nt_f32(eye128, pre)
        gcrow_s[1, z] = _dot_nt_f32(eye128, suf)
        tot_s[z] = jnp.broadcast_to(tot, (8, LANES))
        o_s[rows, :] = jnp.zeros((CHUNK, A_W), F32)
        y_s[rows, :] = jnp.zeros((CHUNK, C_W), F32)
        return carry

    lax.fori_loop(0, nc, prep, 0)

    for d in range(2):
        for h in range(H_A):
            st_s[d * H_A + h] = jnp.zeros((DK_A, DV_A), F32) if zero_init else sdn0[0, d, h]
        for h in range(H_C):
            hs_s[d, h // HPG, (h % HPG) * P_C:(h % HPG + 1) * P_C, :] = (
                jnp.zeros((P_C, N_C), F32) if zero_init else sssd0[0, d, h])

    def delta_prep(i, carry):
        units = []
        for zz in range(2):
            z = 2 * i + zz
            rows = chunk_rows(z)
            tot = tot_s[z][0:1, :]
            beta = beta_s[rows, :]
            gcs = [gc_s[d, rows, :] for d in range(2)]
            grs = [gcrow_s[d, z] for d in range(2)]
            for h in range(H_A):
                q_h = qkv_s[rows, COL_Q + h * DK_A:COL_Q + (h + 1) * DK_A]
                k_h = qkv_s[rows, COL_K + h * DK_A:COL_K + (h + 1) * DK_A]
                v_h = qkv_s[rows, COL_V + h * DV_A:COL_V + (h + 1) * DV_A]
                units.append(dict(z=z, h=h, q=q_h, k=k_h, v=v_h, tot=tot, beta=beta, gcs=gcs, grs=grs))
        qkk = [_dot_nt(jnp.concatenate([p["q"], p["k"]], axis=0), p["k"]) for p in units]
        dus = []
        for p, qk_kk in zip(units, qkk):
            for d in range(2):
                h = p["h"]
                ln = LANE_ALPHA + d * H_A + h
                a_col = p["gcs"][d][:, ln:ln + 1]
                a_row = p["grs"][d][ln:ln + 1, :]
                t_col = p["tot"][:, ln:ln + 1]
                decay = jnp.exp(jnp.where(incl[d], a_col - a_row, -1e30))
                b_col = p["beta"][:, LANE_BETA + d * H_A + h:LANE_BETA + d * H_A + h + 1]
                eg = jnp.exp(a_col)
                m = jnp.where(strict[d], qk_kk[CHUNK:] * b_col * decay, 0.0)
                rhs = jnp.concatenate([p["v"] * b_col, p["k"] * (b_col * eg)], axis=1)
                idx = (d, p["z"], h)
                qk_s[idx] = (qk_kk[:CHUNK] * decay).astype(qk_s.dtype)
                kd_s[idx] = (p["k"] * jnp.exp(t_col - a_col)).astype(kd_s.dtype)
                wq_s[d, p["z"], h, CHUNK:, :] = (p["q"] * eg).astype(wq_s.dtype)
                dus.append(dict(idx=idx, m=m, rhs=rhs))
        t_inv = [eye - jnp.where(lvl[0], p["m"], 0.0) for p in dus]
        for s in range(1, n_lvl):
            x = [_dot(jnp.where(lvl[s], p["m"], 0.0), t) for p, t in zip(dus, t_inv)]
            t_inv = [t - _dot(t, xx) for t, xx in zip(t_inv, x)]
        uw = [_dot(t, p["rhs"]) for p, t in zip(dus, t_inv)]
        for p, r in zip(dus, uw):
            d, z, h = p["idx"]
            u_s[p["idx"]] = r[:, :DV_A]
            wq_s[d, z, h, :CHUNK, :] = r[:, DV_A:].astype(wq_s.dtype)
        return carry

    lax.fori_loop(0, nc // 2, delta_prep, 0)

    def delta_scan(z, carry):
        us = [(d, (z if d == 0 else nc - 1 - z), h) for d in range(2) for h in range(H_A)]
        s_prev = [st_s[d * H_A + h] for d, _, h in us]
        ws_qs = [_dot(wq_s[idx], s) for idx, s in zip(us, s_prev)]
        v_new = [u_s[idx] - r[:CHUNK] for idx, r in zip(us, ws_qs)]
        o_in = [_dot(qk_s[idx], v) for idx, v in zip(us, v_new)]
        s_add = [_dot_tn(kd_s[idx], v) for idx, v in zip(us, v_new)]
        for (d, zc, h), r, oi, sa, s in zip(us, ws_qs, o_in, s_add, s_prev):
            ln = LANE_ALPHA + d * H_A + h
            cs = slice(h * DV_A, (h + 1) * DV_A)
            o_s[chunk_rows(zc), cs] = o_s[chunk_rows(zc), cs] + r[CHUNK:] + oi
            st_s[d * H_A + h] = s * jnp.exp(tot_s[zc][0:1, ln:ln + 1]) + sa
        return carry

    lax.fori_loop(0, nc, delta_scan, 0)

    row_hi = lax.broadcasted_iota(jnp.int32, (HPG * P_C, N_C), 0) >= P_C

    def ssd_scan(z, carry):
        us, gs = [], []
        for d in range(2):
            zc = z if d == 0 else nc - 1 - z
            rows = chunk_rows(zc)
            sp = sp_s[rows, :]
            tot = tot_s[zc][0:1, :]
            gc = gc_s[d, rows, :]
            gr = gcrow_s[d, zc]
            for g in range(G_C):
                b_g = xbc_s[rows, C_W + g * N_C:C_W + (g + 1) * N_C]
                c_g = xbc_s[rows, C_W + (G_C + g) * N_C:C_W + (G_C + g + 1) * N_C]
                gs.append(dict(d=d, g=g, rows=rows, b=b_g, c=c_g, tot=tot, h_prev=hs_s[d, g]))
                for h in range(g * HPG, (g + 1) * HPG):
                    ln = LANE_DT + d * H_C + h
                    a_col = gc[:, ln:ln + 1]
                    lmat = jnp.exp(jnp.where(incl[d], a_col - gr[ln:ln + 1, :], -1e30))
                    xdt = xbc_s[rows, h * P_C:(h + 1) * P_C] * sp[:, ln:ln + 1]
                    bdec = b_g * jnp.exp(tot[:, ln:ln + 1] - a_col)
                    us.append(dict(d=d, h=h, rows=rows, gi=len(gs) - 1, lmat=lmat, xdt=xdt, bdec=bdec,
                                   ea=jnp.exp(a_col)))
        st = [_dot_tn(p["xdt"], p["bdec"]) for p in us]
        cb = [_dot_nt(p["c"], p["b"]) for p in gs]
        y_off = [_dot_nt(p["c"], p["h_prev"]) for p in gs]
        y_diag = [_dot(cb[p["gi"]] * p["lmat"], p["xdt"]) for p in us]
        for p, yd in zip(us, y_diag):
            h = p["h"]
            cs = slice(h * P_C, (h + 1) * P_C)
            yo = y_off[p["gi"]][:, (h % HPG) * P_C:(h % HPG + 1) * P_C]
            y_s[p["rows"], cs] = y_s[p["rows"], cs] + yd + yo * p["ea"]
        for gi, p in enumerate(gs):
            ln = LANE_DT + p["d"] * H_C + p["g"] * HPG
            dec = jnp.where(row_hi, jnp.exp(p["tot"][:, ln + 1:ln + 2]), jnp.exp(p["tot"][:, ln:ln + 1]))
            hs_s[p["d"], p["g"]] = p["h_prev"] * dec + jnp.concatenate([st[HPG * gi], st[HPG * gi + 1]], axis=0)
        return carry

    lax.fori_loop(0, nc, ssd_scan, 0)


    def finish(z, carry):
        rows = chunk_rows(z)
        for h in range(H_A):
            cs = slice(h * DV_A, (h + 1) * DV_A)
            o = _rms(o_s[rows, cs], lanev[2:3, :])
            o = o * _silu(proj[0, rows, COL_GATE + h * DV_A:COL_GATE + (h + 1) * DV_A])
            ycat[0, rows, cs] = o.astype(ycat.dtype)
        y = y_s[rows, :] + ssdv[0:1, :] * xbc_s[rows, 0:C_W]
        y = _rms(y * _silu(proj[0, rows, COL_Z:COL_Z + C_W]), ssdv[1:2, :])
        ycat[0, rows, A_W + B_W:] = y.astype(ycat.dtype)
        return carry

    lax.fori_loop(0, nc, finish, 0)

    if emit_state:
        for d in range(2):
            for h in range(H_A):
                sdn_out[0, d, h] = st_s[d * H_A + h]
            for h in range(H_C):
                sssd_out[0, d, h] = hs_s[d, h // HPG, (h % HPG) * P_C:(h % HPG + 1) * P_C, :]


def _mixer(proj, dn_conv_w, sc_conv_w, ssd_conv_w, lanev, ssdv, s_dn0, s_ssd0, emit_state):
    nb, seq_len, _ = proj.shape
    nc = seq_len // CHUNK
    assert nc % 2 == 0
    zero_init = s_dn0 is None
    full = lambda a: pl.BlockSpec(a.shape, lambda b: (0,) * a.ndim)
    args = [proj, dn_conv_w, sc_conv_w, ssd_conv_w, lanev, ssdv]
    in_specs = [pl.BlockSpec((1, seq_len, D_IN_PAD), lambda b: (b, 0, 0), pipeline_mode=pl.Buffered(1))
                if seq_len > 512 else pl.BlockSpec((1, seq_len, D_IN_PAD), lambda b: (b, 0, 0)),
                full(dn_conv_w), full(sc_conv_w), full(ssd_conv_w), full(lanev), full(ssdv)]
    if not zero_init:
        args += [s_dn0, s_ssd0]
        in_specs += [pl.BlockSpec((1, 2, H_A, DK_A, DV_A), lambda b: (b, 0, 0, 0, 0)),
                     pl.BlockSpec((1, 2, H_C, P_C, N_C), lambda b: (b, 0, 0, 0, 0))]
    out_shape = [jax.ShapeDtypeStruct((nb, seq_len, D_MODEL), BF16)]
    out_specs = [pl.BlockSpec((1, seq_len, D_MODEL), lambda b: (b, 0, 0))]
    if emit_state:
        out_shape += [jax.ShapeDtypeStruct((nb, 2, H_A, DK_A, DV_A), F32),
                      jax.ShapeDtypeStruct((nb, 2, H_C, P_C, N_C), F32)]
        out_specs += [pl.BlockSpec((1, 2, H_A, DK_A, DV_A), lambda b: (b, 0, 0, 0, 0)),
                      pl.BlockSpec((1, 2, H_C, P_C, N_C), lambda b: (b, 0, 0, 0, 0))]
    scratch = [pltpu.VMEM((seq_len, 3 * A_W), F32),
               pltpu.VMEM((seq_len, XBC_W), F32),
               pltpu.VMEM((2, seq_len, LANES), F32),
               pltpu.VMEM((2, nc, LANES, CHUNK), F32),
               pltpu.VMEM((nc, 8, LANES), F32),
               pltpu.VMEM((seq_len, LANES), F32),
               pltpu.VMEM((seq_len, LANES), F32),
               pltpu.VMEM((2, nc, H_A, CHUNK, DV_A), F32),
               pltpu.VMEM((2, nc, H_A, 2 * CHUNK, DK_A), BF16),
               pltpu.VMEM((2, nc, H_A, CHUNK, DK_A), BF16),
               pltpu.VMEM((2, nc, H_A, CHUNK, CHUNK), BF16),
               pltpu.VMEM((seq_len, A_W), F32),
               pltpu.VMEM((seq_len, C_W), F32),
               pltpu.VMEM((2 * H_A, DK_A, DV_A), F32),
               pltpu.VMEM((2, G_C, HPG * P_C, N_C), F32)]
    return pl.pallas_call(
        functools.partial(_mixer_kernel, seq_len=seq_len, zero_init=zero_init, emit_state=emit_state),
        grid=(nb,),
        in_specs=in_specs,
        out_specs=out_specs,
        out_shape=out_shape,
        scratch_shapes=scratch,
        compiler_params=_params(("arbitrary",)),
        name="mixer",
    )(*args)


def _top2_sum(a, b, c, d):
    hi1, lo1 = jnp.maximum(a, b), jnp.minimum(a, b)
    hi2, lo2 = jnp.maximum(c, d), jnp.minimum(c, d)
    return jnp.maximum(hi1, hi2) + jnp.maximum(jnp.minimum(hi1, hi2), jnp.maximum(lo1, lo2))


def _outproj_kernel(y_ref, x_ref, mod_ref, w_ref, g_ref, rwt_ref, rb_ref, x1_ref, h2_ref, comb_ref):
    m = mod_ref[0]
    gate1, shift2, scale2 = m[2:3, :], m[3:4, :], m[4:5, :]
    x1 = x_ref[...] + gate1 * _dot(y_ref[...], w_ref[...])
    x1_ref[...] = x1
    h2 = _rms(x1, g_ref[...]) * (1.0 + scale2) + shift2
    h2_ref[...] = h2.astype(h2_ref.dtype)

    scores = _sigmoid(_dot_nt_f32(rwt_ref[...], h2))
    biased = scores + rb_ref[...]
    sc = [scores[e:e + 1, :] for e in range(N_EXPERTS)]
    bi = [biased[e:e + 1, :] for e in range(N_EXPERTS)]
    gs = [_top2_sum(*bi[EPG * g:EPG * (g + 1)]) for g in range(N_GROUPS)]
    gmax = functools.reduce(jnp.maximum, gs)
    first = []
    taken = None
    for g in range(N_GROUPS):
        hit = gs[g] == gmax
        if taken is None:
            first.append(hit)
            taken = hit
        else:
            first.append(hit & jnp.logical_not(taken))
            taken = taken | hit

    def pick(vals, j):
        out = vals[EPG * (N_GROUPS - 1) + j]
        for g in range(N_GROUPS - 2, -1, -1):
            out = jnp.where(first[g], vals[EPG * g + j], out)
        return out

    ib = [pick(bi, j) for j in range(EPG)]
    isc = [pick(sc, j) for j in range(EPG)]
    sel = []
    for j in range(EPG):
        cnt = jnp.zeros_like(ib[j])
        for i in range(EPG):
            if i == j:
                continue
            ahead = (ib[i] > ib[j]) | ((ib[i] == ib[j]) if i < j else False)
            cnt = cnt + jnp.where(ahead, 1.0, 0.0)
        sel.append(cnt < 2.0)
    wj = [jnp.where(sel[j], isc[j], 0.0) for j in range(EPG)]
    denom = functools.reduce(lambda a, b: a + b, wj)
    for g in range(N_GROUPS):
        for j in range(EPG):
            comb_ref[EPG * g + j:EPG * g + j + 1, :] = jnp.where(first[g], wj[j] / denom, 0.0)


def _outproj(ycat, x, mod6, seq_rows, w_out, norm_g, router_wt, router_b):
    t, d = x.shape
    tm = _tile(seq_rows, 256)
    return pl.pallas_call(
        _outproj_kernel,
        grid=(t // tm,),
        in_specs=[pl.BlockSpec((tm, d), lambda i: (i, 0)),
                  pl.BlockSpec((tm, d), lambda i: (i, 0)),
                  pl.BlockSpec((1, 6, d), lambda i: ((i * tm) // seq_rows, 0, 0)),
                  pl.BlockSpec((d, d), lambda i: (0, 0)),
                  pl.BlockSpec((1, d), lambda i: (0, 0)),
                  pl.BlockSpec((N_EXPERTS, d), lambda i: (0, 0)),
                  pl.BlockSpec((N_EXPERTS, 1), lambda i: (0, 0))],
        out_specs=[pl.BlockSpec((tm, d), lambda i: (i, 0)),
                   pl.BlockSpec((tm, d), lambda i: (i, 0)),
                   pl.BlockSpec((N_EXPERTS, tm), lambda i: (0, i))],
        out_shape=[jax.ShapeDtypeStruct((t, d), F32),
                   jax.ShapeDtypeStruct((t, d), BF16),
                   jax.ShapeDtypeStruct((N_EXPERTS, t), F32)],
        compiler_params=_params(("arbitrary",)),
        name="outproj_route",
    )(ycat, x, mod6, w_out, norm_g.reshape(1, d), router_wt, router_b.reshape(N_EXPERTS, 1))


def _moe_kernel(h_ref, comb_ref, wg_ref, wu_ref, wd_ref, x1_ref, mod_ref, fg_ref, o_ref, acc_ref, *, final):
    e = pl.program_id(1)

    @pl.when(e == 0)
    def _():
        acc_ref[...] = jnp.zeros_like(acc_ref)

    h = h_ref[...]
    comb = comb_ref[...]
    lane = lax.broadcasted_iota(jnp.int32, comb.shape, 1)
    ce = jnp.sum(jnp.where(lane == e, comb, 0.0), axis=1, keepdims=True)
    act = _silu(_dot(h, wg_ref[0])) * _dot(h, wu_ref[0]) * ce
    acc_ref[...] += _dot(act, wd_ref[0])

    @pl.when(e == pl.num_programs(1) - 1)
    def _():
        gate2 = mod_ref[0][5:6, :]
        x2 = x1_ref[...] + gate2 * acc_ref[...]
        o_ref[...] = _rms(x2, fg_ref[...]) if final else x2


def _moe(h2, comb, w_gate, w_up, w_down, x1, mod6, seq_rows, final_g, final):
    t, d = x1.shape
    tm = _tile(seq_rows, 1024)
    return pl.pallas_call(
        functools.partial(_moe_kernel, final=final),
        grid=(t // tm, N_EXPERTS),
        in_specs=[pl.BlockSpec((tm, d), lambda i, e: (i, 0)),
                  pl.BlockSpec((tm, N_EXPERTS), lambda i, e: (i, 0)),
                  pl.BlockSpec((1, d, D_EXPERT), lambda i, e: (e, 0, 0)),
                  pl.BlockSpec((1, d, D_EXPERT), lambda i, e: (e, 0, 0)),
                  pl.BlockSpec((1, D_EXPERT, d), lambda i, e: (e, 0, 0)),
                  pl.BlockSpec((tm, d), lambda i, e: (i, 0)),
                  pl.BlockSpec((1, 6, d), lambda i, e: ((i * tm) // seq_rows, 0, 0)),
                  pl.BlockSpec((1, d), lambda i, e: (0, 0))],
        out_specs=pl.BlockSpec((tm, d), lambda i, e: (i, 0)),
        out_shape=jax.ShapeDtypeStruct((t, d), F32),
        scratch_shapes=[pltpu.VMEM((tm, d), F32)],
        compiler_params=_params(("arbitrary", "arbitrary")),
        name="experts",
    )(h2, comb, w_gate, w_up, w_down, x1, mod6, final_g.reshape(1, d))


def _lane_row(*pieces):
    row = jnp.zeros((LANES,), F32)
    for lane, vals in pieces:
        row = lax.dynamic_update_slice(row, vals.reshape(-1).astype(F32), (lane,))
    return row


def kernel(x_prompt, x_sample, state_delta, state_ssd, c, c_ctx, mod_w, mod_b, norm1_g, norm2_g, w_in, w_out,
           dn_conv_w, dn_a_log, dn_dt_bias, dn_norm_g, sc_conv_w, ssd_conv_w, ssd_a_log, ssd_dt_bias, ssd_d,
           ssd_norm_g, router_w, router_b, exp_w_gate, exp_w_up, exp_w_down, final_norm_g):
    depth = mod_w.shape[0]
    n_ctx, seq, d = x_prompt.shape
    n_dec, dec_seq, _ = x_sample.shape
    assert seq % CHUNK == 0 and dec_seq % CHUNK == 0 and d == D_MODEL

    n_rows = -(-(1 + n_dec) // 8) * 8
    cond_rows = jnp.zeros((n_rows, d), F32).at[0].set(c_ctx).at[1:1 + n_dec].set(c)
    mod = _modulation(cond_rows, mod_w, mod_b).reshape(depth, n_rows, 6, d)

    sp = np.cumsum((A_W, A_W, A_W, A_W, 2 * H_A, 2 * H_A, B_W, B_W, B_W, C_W, XBC_W, 2 * H_C))
    w_in_r = jnp.concatenate(
        [w_in[:, :, :sp[3]], w_in[:, :, sp[5]:sp[10]], w_in[:, :, sp[3]:sp[5]], w_in[:, :, sp[10]:],
         jnp.zeros((depth, d, D_IN_PAD - int(sp[-1])), w_in.dtype)], axis=-1).astype(BF16)
    w_out_b = w_out.astype(BF16)
    wg_b, wu_b, wd_b = exp_w_gate.astype(BF16), exp_w_up.astype(BF16), exp_w_down.astype(BF16)
    router_wt = router_w.T

    xp = x_prompt.reshape(n_ctx * seq, d)
    xs = _add_pos(x_sample, _grid_pos_embed(dec_seq, d).astype(x_sample.dtype)).reshape(n_dec * dec_seq, d)

    new_dn, new_ssd = [], []
    for l in range(depth):
        lanev = jnp.zeros((8, LANES), F32)
        lanev = lanev.at[0].set(_lane_row((LANE_ALPHA, dn_a_log[l]), (LANE_DT, ssd_a_log[l])))
        lanev = lanev.at[1].set(_lane_row((LANE_ALPHA, dn_dt_bias[l]), (LANE_DT, ssd_dt_bias[l])))
        lanev = lanev.at[2].set(dn_norm_g[l].astype(F32))
        ssdv = jnp.zeros((8, C_W), F32).at[0].set(jnp.repeat(ssd_d[l].astype(F32), P_C)).at[1].set(ssd_norm_g[l])
        final = l == depth - 1

        def block(x, mod6, nb, seq_len, seq_rows, s_dn0, s_ssd0, emit_state):
            proj = _inproj(x, mod6, seq_rows, norm1_g[l], w_in_r[l]).reshape(nb, seq_len, D_IN_PAD)
            outs = _mixer(proj, dn_conv_w[l], sc_conv_w[l], ssd_conv_w[l], lanev, ssdv, s_dn0, s_ssd0, emit_state)
            ycat = outs[0].reshape(nb * seq_len, d)
            x1, h2, comb_t = _outproj(ycat, x, mod6, seq_rows, w_out_b[l], norm2_g[l], router_wt, router_b)
            x2 = _moe(h2, comb_t.T, wg_b[l], wu_b[l], wd_b[l], x1, mod6, seq_rows, final_norm_g, final)
            return x2, outs[1:]

        xp, (s_dn, s_ssd) = block(xp, mod[l, 0:1], n_ctx, seq, n_ctx * seq, None, None, True)
        new_dn.append(s_dn)
        new_ssd.append(s_ssd)
        xs, _ = block(xs, mod[l, 1:1 + n_dec], n_dec, dec_seq, dec_seq,
                      state_delta[:, l].astype(F32), state_ssd[:, l].astype(F32), False)

    return (xp.reshape(n_ctx, seq, d), xs.reshape(n_dec, dec_seq, d),
            jnp.stack(new_dn, axis=1), jnp.stack(new_ssd, axis=1))
```

```python
import functools
import math

import jax
import jax.numpy as jnp
import numpy as np
from jax import lax
from jax.experimental import pallas as pl
from jax.experimental.pallas import tpu as pltpu

F32 = jnp.float32
BF16 = jnp.bfloat16

D_MODEL = 1024
GRID_W = 64
POS_BASE = 10000.0
H_A, DK_A, DV_A = 4, 128, 128
A_W = H_A * DV_A
H_C, P_C, N_C, G_C = 4, 64, 64, 2
HPG = H_C // G_C
C_W = H_C * P_C
B_W = D_MODEL - A_W - C_W
XBC_W = C_W + 2 * G_C * N_C
CHUNK = 64
N_EXPERTS = 16
N_GROUPS = 4
EPG = N_EXPERTS // N_GROUPS
D_EXPERT = 256
EPS = 1e-6
LANES = 128

COL_Q, COL_K, COL_V, COL_GATE = 0, A_W, 2 * A_W, 3 * A_W
COL_SCH = 4 * A_W
COL_SCB = COL_SCH + B_W
COL_SCC = COL_SCB + B_W
COL_Z = COL_SCC + B_W
COL_XBC = COL_Z + C_W
COL_SMALL = COL_XBC + XBC_W
D_IN_PAD = COL_SMALL + LANES
LANE_BETA, LANE_ALPHA, LANE_DT = 0, 2 * H_A, 4 * H_A
SRC_SMALL_A = 4 * A_W
SRC_SCH = SRC_SMALL_A + 4 * H_A
SRC_SMALL_B = SRC_SCH + 3 * B_W + C_W + XBC_W
D_IN = SRC_SMALL_B + 2 * H_C

assert HPG == 2 and P_C == N_C == CHUNK and HPG * P_C == LANES and G_C * N_C == LANES and DK_A == DV_A == LANES

VMEM_LIMIT = 56 * 1024 * 1024


def _dot(a, b):
    return jnp.dot(a.astype(BF16), b.astype(BF16), preferred_element_type=F32)


def _dot_nt(a, b):
    return lax.dot_general(a.astype(BF16), b.astype(BF16), (((1,), (1,)), ((), ())), preferred_element_type=F32)


def _dot_tn(a, b):
    return lax.dot_general(a.astype(BF16), b.astype(BF16), (((0,), (0,)), ((), ())), preferred_element_type=F32)


def _dot_f32(a, b):
    return jnp.dot(a, b, precision=lax.Precision.HIGHEST, preferred_element_type=F32)


def _dot_nt_f32(a, b):
    return lax.dot_general(a, b, (((1,), (1,)), ((), ())), precision=lax.Precision.HIGHEST,
                           preferred_element_type=F32)


def _silu(x):
    return x * (1.0 / (1.0 + jnp.exp(-x)))


def _sigmoid(x):
    return 1.0 / (1.0 + jnp.exp(-x))


def _softplus(x):
    return jnp.maximum(x, 0.0) + jnp.log1p(jnp.exp(-jnp.abs(x)))


def _rms(x, g):
    return x * lax.rsqrt(jnp.mean(x * x, axis=-1, keepdims=True) + EPS) * g


def _tile(n, pref):
    t = min(n, pref)
    while n % t:
        t -= 8
    assert t > 0 and t % 8 == 0, (n, pref)
    return t


def _params(sem):
    return pltpu.CompilerParams(dimension_semantics=sem, vmem_limit_bytes=VMEM_LIMIT)


def _mod_kernel(cond_ref, w_ref, b_ref, o_ref):
    s = _silu(cond_ref[...])
    o_ref[0] = _dot_f32(s, w_ref[0]) + b_ref[0]


def _modulation(cond_rows, mod_w, mod_b):
    depth, d, n = mod_w.shape
    r = cond_rows.shape[0]
    tn = _tile(n, 1536)
    return pl.pallas_call(
        _mod_kernel,
        grid=(depth, n // tn),
        in_specs=[pl.BlockSpec((r, d), lambda l, j: (0, 0)),
                  pl.BlockSpec((1, d, tn), lambda l, j: (l, 0, j)),
                  pl.BlockSpec((1, 1, tn), lambda l, j: (l, 0, j))],
        out_specs=pl.BlockSpec((1, r, tn), lambda l, j: (l, 0, j)),
        out_shape=jax.ShapeDtypeStruct((depth, r, n), F32),
        compiler_params=_params(("arbitrary", "arbitrary")),
        name="modulation",
    )(cond_rows, mod_w, mod_b.reshape(depth, 1, n))


def _add_kernel(x_ref, p_ref, o_ref):
    o_ref[0] = x_ref[0] + p_ref[...]


def _add_pos(x, pos):
    nb, l, d = x.shape
    tl = _tile(l, 512)
    return pl.pallas_call(
        _add_kernel,
        grid=(nb, l // tl),
        in_specs=[pl.BlockSpec((1, tl, d), lambda b, i: (b, i, 0)),
                  pl.BlockSpec((tl, d), lambda b, i: (i, 0))],
        out_specs=pl.BlockSpec((1, tl, d), lambda b, i: (b, i, 0)),
        out_shape=jax.ShapeDtypeStruct(x.shape, x.dtype),
        compiler_params=_params(("arbitrary", "arbitrary")),
        name="add_pos",
    )(x, pos)


def _grid_pos_embed(n_tok, dim):
    rows = n_tok // GRID_W
    rr, cc = jnp.meshgrid(jnp.arange(rows, dtype=F32), jnp.arange(GRID_W, dtype=F32), indexing="ij")
    quarter = dim // 4
    omega = 1.0 / (POS_BASE ** (jnp.arange(quarter, dtype=F32) / quarter))
    ang_r = rr.reshape(-1, 1) * omega
    ang_c = cc.reshape(-1, 1) * omega
    return jnp.concatenate([jnp.sin(ang_r), jnp.cos(ang_r), jnp.sin(ang_c), jnp.cos(ang_c)], axis=-1)


def _inproj_kernel(x_ref, mod_ref, g_ref, w_ref, o_ref):
    m = mod_ref[0]
    shift, scale = m[0:1, :], m[1:2, :]
    h = _rms(x_ref[...], g_ref[...]) * (1.0 + scale) + shift
    o_ref[...] = _dot(h, w_ref[...])


def _win_kernel(w_ref, o_ref):
    w = w_ref[...]
    rows = w.shape[0]
    o_ref[:, :COL_SCH] = w[:, :SRC_SMALL_A].astype(o_ref.dtype)
    o_ref[:, COL_SCH:COL_SMALL] = w[:, SRC_SCH:SRC_SMALL_B].astype(o_ref.dtype)
    small = jnp.concatenate([w[:, SRC_SMALL_A:SRC_SCH], w[:, SRC_SMALL_B:],
                             jnp.zeros((rows, LANES - 4 * H_A - 2 * H_C), w.dtype)], axis=1)
    o_ref[:, COL_SMALL:] = small.astype(o_ref.dtype)


def _reorder_w_in(w_in):
    depth, d, n = w_in.shape
    assert n == D_IN
    tr = _tile(d, 256)
    return pl.pallas_call(
        _win_kernel,
        grid=(depth, d // tr),
        in_specs=[pl.BlockSpec((None, tr, n), lambda l, i: (l, i, 0))],
        out_specs=pl.BlockSpec((None, tr, D_IN_PAD), lambda l, i: (l, i, 0)),
        out_shape=jax.ShapeDtypeStruct((depth, d, D_IN_PAD), BF16),
        compiler_params=_params(("arbitrary", "arbitrary")),
        name="reorder_w_in",
    )(w_in)


def _inproj(x, mod6, seq_rows, norm_g, w_in_r, layer):
    t, d = x.shape
    tm = _tile(seq_rows, 256)
    return pl.pallas_call(
        _inproj_kernel,
        grid=(t // tm,),
        in_specs=[pl.BlockSpec((tm, d), lambda i: (i, 0)),
                  pl.BlockSpec((1, 6, d), lambda i: ((i * tm) // seq_rows, 0, 0)),
                  pl.BlockSpec((1, d), lambda i: (0, 0)),
                  pl.BlockSpec((None, d, D_IN_PAD), lambda i: (layer, 0, 0))],
        out_specs=pl.BlockSpec((tm, D_IN_PAD), lambda i: (i, 0)),
        out_shape=jax.ShapeDtypeStruct((t, D_IN_PAD), F32),
        compiler_params=_params(("arbitrary",)),
        name="inproj",
    )(x, mod6, norm_g.reshape(1, d), w_in_r)


def _conv3(load, r0, z, nc, seq_len, w):
    cur = load(pl.ds(r0, CHUNK))
    prev = load(pl.ds(pl.multiple_of(jnp.maximum(r0 - 8, 0), 8), 8))[7:8, :]
    nxt = load(pl.ds(pl.multiple_of(jnp.minimum(r0 + CHUNK, seq_len - 8), 8), 8))[0:1, :]
    prev = jnp.where(z > 0, prev, 0.0)
    nxt = jnp.where(z < nc - 1, nxt, 0.0)
    ri = lax.broadcasted_iota(jnp.int32, cur.shape, 0)
    x_prev = jnp.where(ri == 0, prev, pltpu.roll(cur, 1, 0))
    x_next = jnp.where(ri == CHUNK - 1, nxt, pltpu.roll(cur, CHUNK - 1, 0))
    return w[0:1, :] * x_prev + w[1:2, :] * cur + w[2:3, :] * x_next


def _mixer_kernel(*refs, seq_len, zero_init, emit_state):
    refs = list(refs)
    proj, dnw, scw, ssw, lanev, ssdv = refs[:6]
    k = 6
    if not zero_init:
        sdn0, sssd0 = refs[k:k + 2]
        k += 2
    ycat = refs[k]
    k += 1
    if emit_state:
        sdn_out, sssd_out = refs[k:k + 2]
        k += 2
    (qkv_s, xbc_s, bdup_s, cdup_s, gc_s, gcrow_s, gcpair_s, tot_s, sp_s, beta_s, u_s, wq_s, kd_s, qk_s, o_s, y_s,
     st_s, hs_s) = refs[k:]

    nc = seq_len // CHUNK
    ri = lax.broadcasted_iota(jnp.int32, (CHUNK, CHUNK), 0)
    ci = lax.broadcasted_iota(jnp.int32, (CHUNK, CHUNK), 1)
    tril = (ri >= ci).astype(F32)
    eye = (ri == ci).astype(F32)
    r128 = lax.broadcasted_iota(jnp.int32, (LANES, LANES), 0)
    c128 = lax.broadcasted_iota(jnp.int32, (LANES, LANES), 1)
    eye128 = (r128 == c128).astype(F32)
    incl = (ri >= ci, ri <= ci)
    strict = (ri > ci, ri < ci)
    n_lvl = int(math.log2(CHUNK))
    lvl = [((ri >> s) == (ci >> s)) & ((ri >> (s - 1)) != (ci >> (s - 1))) for s in range(1, n_lvl + 1)]
    alog = lanev[0:1, :]
    bias = lanev[1:2, :]

    def chunk_rows(z):
        return pl.ds(pl.multiple_of(z * CHUNK, CHUNK), CHUNK)

    def prep(z, carry):
        r0 = pl.multiple_of(z * CHUNK, CHUNK)
        rows = pl.ds(r0, CHUNK)
        for j in range(3 * H_A):
            c0 = j * LANES
            a = _silu(_conv3(lambda rs: proj[0, rs, c0:c0 + LANES], r0, z, nc, seq_len, dnw[:, c0:c0 + LANES]))
            if j < 2 * H_A:
                a = a * lax.rsqrt(jnp.sum(a * a, axis=-1, keepdims=True) + EPS)
            if j < H_A:
                a = a * (DK_A ** -0.5)
            qkv_s[rows, c0:c0 + LANES] = a
        for j in range(B_W // LANES):
            c0 = j * LANES
            cv = _conv3(lambda rs: proj[0, rs, COL_SCC + c0:COL_SCC + c0 + LANES]
                        * proj[0, rs, COL_SCH + c0:COL_SCH + c0 + LANES],
                        r0, z, nc, seq_len, scw[:, c0:c0 + LANES])
            yb = proj[0, rows, COL_SCB + c0:COL_SCB + c0 + LANES] * cv
            ycat[0, rows, A_W + c0:A_W + c0 + LANES] = yb.astype(ycat.dtype)
        for j in range(XBC_W // LANES):
            c0 = j * LANES
            a = _silu(_conv3(lambda rs: proj[0, rs, COL_XBC + c0:COL_XBC + c0 + LANES], r0, z, nc, seq_len,
                             ssw[:, c0:c0 + LANES]))
            if c0 < C_W:
                xbc_s[rows, c0:c0 + LANES] = a
            else:
                swapped = pltpu.roll(a, N_C, 1)
                lo = lax.broadcasted_iota(jnp.int32, a.shape, 1) < N_C
                dup_s = bdup_s if c0 < C_W + G_C * N_C else cdup_s
                dup_s[rows, 0:LANES] = jnp.where(lo, a, swapped)
                dup_s[rows, LANES:2 * LANES] = jnp.where(lo, swapped, a)
        sm = proj[0, rows, COL_SMALL:COL_SMALL + LANES]
        sp = _softplus(sm + bias)
        g = -jnp.exp(alog) * sp
        pre = _dot_f32(tril, g)
        tot = pre[CHUNK - 1:CHUNK, :]
        suf = tot - pre + g
        sp_s[rows, :] = sp
        beta_s[rows, :] = _sigmoid(sm)
        gc_s[0, rows, :] = pre
        gc_s[1, rows, :] = suf
        for d, gcd in enumerate((pre, suf)):
            gr = _dot_nt_f32(eye128, gcd)
            gcrow_s[d, z] = gr
            for g in range(G_C):
                ln = LANE_DT + d * H_C + g * HPG
                gcpair_s[d, z, g:g + 1, :] = jnp.concatenate([gr[ln:ln + 1, :], gr[ln + 1:ln + 2, :]], axis=1)
        tot_s[z] = jnp.broadcast_to(tot, (8, LANES))
        o_s[rows, :] = jnp.zeros((CHUNK, A_W), F32)
        y_s[rows, :] = jnp.zeros((CHUNK, C_W), F32)
        return carry

    lax.fori_loop(0, nc, prep, 0)

    for d in range(2):
        for h in range(H_A):
            st_s[d * H_A + h] = jnp.zeros((DK_A, DV_A), F32) if zero_init else sdn0[0, d, h]
        for g in range(G_C):
            if zero_init:
                hs_s[d, g] = jnp.zeros((HPG * P_C, HPG * N_C), F32)
            else:
                zero = jnp.zeros((P_C, N_C), F32)
                hs_s[d, g] = jnp.concatenate(
                    [jnp.concatenate([sssd0[0, d, g * HPG], zero], axis=1),
                     jnp.concatenate([zero, sssd0[0, d, g * HPG + 1]], axis=1)], axis=0)

    cpi = 4 if nc % 4 == 0 else 2

    def delta_prep(i, carry):
        units = []
        for zz in range(cpi):
            z = cpi * i + zz
            rows = chunk_rows(z)
            tot = tot_s[z][0:1, :]
            beta = beta_s[rows, :]
            gcs = [gc_s[d, rows, :] for d in range(2)]
            grs = [gcrow_s[d, z] for d in range(2)]
            for h in range(H_A):
                q_h = qkv_s[rows, COL_Q + h * DK_A:COL_Q + (h + 1) * DK_A]
                k_h = qkv_s[rows, COL_K + h * DK_A:COL_K + (h + 1) * DK_A]
                v_h = qkv_s[rows, COL_V + h * DV_A:COL_V + (h + 1) * DV_A]
                units.append(dict(z=z, h=h, q=q_h, k=k_h, v=v_h, tot=tot, beta=beta, gcs=gcs, grs=grs))
        qkk = [_dot_nt(jnp.concatenate([p["q"], p["k"]], axis=0), p["k"]) for p in units]
        dus = []
        for p, qk_kk in zip(units, qkk):
            for d in range(2):
                h = p["h"]
                ln = LANE_ALPHA + d * H_A + h
                a_col = p["gcs"][d][:, ln:ln + 1]
                a_row = p["grs"][d][ln:ln + 1, :]
                t_col = p["tot"][:, ln:ln + 1]
                decay = jnp.exp(jnp.where(incl[d], a_col - a_row, -1e30))
                b_col = p["beta"][:, LANE_BETA + d * H_A + h:LANE_BETA + d * H_A + h + 1]
                eg = jnp.exp(a_col)
                m = jnp.where(strict[d], qk_kk[CHUNK:] * b_col * decay, 0.0)
                rhs = jnp.concatenate([p["v"] * b_col, p["k"] * (b_col * eg)], axis=1)
                idx = (d, p["z"], h)
                qk_s[idx] = (qk_kk[:CHUNK] * decay).astype(qk_s.dtype)
                kd_s[idx] = (p["k"] * jnp.exp(t_col - a_col)).astype(kd_s.dtype)
                wq_s[d, p["z"], h, CHUNK:, :] = (p["q"] * eg).astype(wq_s.dtype)
                dus.append(dict(idx=idx, m=m, rhs=rhs))
        t_inv = [eye - jnp.where(lvl[0], p["m"], 0.0) for p in dus]
        for s in range(1, n_lvl):
            x = [_dot(jnp.where(lvl[s], p["m"], 0.0), t) for p, t in zip(dus, t_inv)]
            t_inv = [t - _dot(t, xx) for t, xx in zip(t_inv, x)]
        uw = [_dot(t, p["rhs"]) for p, t in zip(dus, t_inv)]
        for p, r in zip(dus, uw):
            d, z, h = p["idx"]
            u_s[p["idx"]] = r[:, :DV_A]
            wq_s[d, z, h, :CHUNK, :] = r[:, DV_A:].astype(wq_s.dtype)
        return carry

    lax.fori_loop(0, nc // cpi, delta_prep, 0)

    pw = HPG * P_C
    ri2 = lax.broadcasted_iota(jnp.int32, (CHUNK, pw), 0)
    ci2 = lax.broadcasted_iota(jnp.int32, (CHUNK, pw), 1)
    lane_hi = ci2 >= P_C
    tj = ci2 & (CHUNK - 1)
    incl2 = (ri2 >= tj, ri2 <= tj)
    rb = lax.broadcasted_iota(jnp.int32, (pw, pw), 0) >= P_C
    cbk = lax.broadcasted_iota(jnp.int32, (pw, pw), 1) >= N_C
    diag_blk = rb == cbk

    def scan(z, carry):
        dus = [(d, (z if d == 0 else nc - 1 - z), h) for d in range(2) for h in range(H_A)]
        s_prev = [st_s[d * H_A + h] for d, _, h in dus]
        us = []
        for d in range(2):
            zc = z if d == 0 else nc - 1 - z
            rows = chunk_rows(zc)
            sp = sp_s[rows, :]
            tot = tot_s[zc][0:1, :]
            gc = gc_s[d, rows, :]
            for g in range(G_C):
                ln = LANE_DT + d * H_C + g * HPG
                gsl = slice(g * pw, (g + 1) * pw)
                a_pair = jnp.where(lane_hi, gc[:, ln + 1:ln + 2], gc[:, ln:ln + 1])
                t_pair = jnp.where(lane_hi, tot[:, ln + 1:ln + 2], tot[:, ln:ln + 1])
                lmat = jnp.exp(jnp.where(incl2[d], a_pair - gcpair_s[d, zc, g:g + 1, :], -1e30))
                xdt = xbc_s[rows, gsl] * jnp.where(lane_hi, sp[:, ln + 1:ln + 2], sp[:, ln:ln + 1])
                b_dup = bdup_s[rows, gsl]
                c_dup = cdup_s[rows, gsl]
                us.append(dict(d=d, g=g, rows=rows, gsl=gsl, lmat=lmat, xdt=xdt, c_dup=c_dup,
                               c_lo=jnp.where(lane_hi, 0.0, c_dup), b_st=jnp.concatenate([b_dup, b_dup], axis=0),
                               bdec=b_dup * jnp.exp(t_pair - a_pair), ea=jnp.exp(a_pair),
                               dec=jnp.where(rb, jnp.exp(tot[:, ln + 1:ln + 2]), jnp.exp(tot[:, ln:ln + 1])),
                               x_bd=jnp.concatenate([jnp.where(lane_hi, 0.0, xdt), jnp.where(lane_hi, xdt, 0.0)],
                                                    axis=0),
                               h_prev=hs_s[d, g]))
        ws_qs = [_dot(wq_s[idx], s) for idx, s in zip(dus, s_prev)]
        st = [_dot_tn(p["xdt"], p["bdec"]) for p in us]
        cb = [_dot_nt(p["c_lo"], p["b_st"]) for p in us]
        y_off = [_dot_nt(p["c_dup"], p["h_prev"]) for p in us]
        v_new = [u_s[idx] - r[:CHUNK] for idx, r in zip(dus, ws_qs)]
        o_in = [_dot(qk_s[idx], v) for idx, v in zip(dus, v_new)]
        s_add = [_dot_tn(kd_s[idx], v) for idx, v in zip(dus, v_new)]
        y_diag = [_dot(cbd * p["lmat"], p["x_bd"]) for p, cbd in zip(us, cb)]
        for p, yd, yo, s in zip(us, y_diag, y_off, st):
            y_s[p["rows"], p["gsl"]] = y_s[p["rows"], p["gsl"]] + yd + yo * p["ea"]
            hs_s[p["d"], p["g"]] = p["h_prev"] * p["dec"] + jnp.where(diag_blk, s, 0.0)
        for (d, zc, h), r, oi, sa, s in zip(dus, ws_qs, o_in, s_add, s_prev):
            ln = LANE_ALPHA + d * H_A + h
            cs = slice(h * DV_A, (h + 1) * DV_A)
            o_s[chunk_rows(zc), cs] = o_s[chunk_rows(zc), cs] + r[CHUNK:] + oi
            st_s[d * H_A + h] = s * jnp.exp(tot_s[zc][0:1, ln:ln + 1]) + sa
        return carry

    lax.fori_loop(0, nc, scan, 0)


    def finish(z, carry):
        rows = chunk_rows(z)
        for h in range(H_A):
            cs = slice(h * DV_A, (h + 1) * DV_A)
            o = _rms(o_s[rows, cs], lanev[2:3, :])
            o = o * _silu(proj[0, rows, COL_GATE + h * DV_A:COL_GATE + (h + 1) * DV_A])
            ycat[0, rows, cs] = o.astype(ycat.dtype)
        y = y_s[rows, :] + ssdv[0:1, :] * xbc_s[rows, 0:C_W]
        y = _rms(y * _silu(proj[0, rows, COL_Z:COL_Z + C_W]), ssdv[1:2, :])
        ycat[0, rows, A_W + B_W:] = y.astype(ycat.dtype)
        return carry

    lax.fori_loop(0, nc, finish, 0)

    if emit_state:
        for d in range(2):
            for h in range(H_A):
                sdn_out[0, d, h] = st_s[d * H_A + h]
            for h in range(H_C):
                k0 = (h % HPG) * P_C
                sssd_out[0, d, h] = hs_s[d, h // HPG][k0:k0 + P_C, k0:k0 + N_C]


def _mixer(proj, dn_conv_w, sc_conv_w, ssd_conv_w, lanev, ssdv, s_dn0, s_ssd0, emit_state):
    nb, seq_len, _ = proj.shape
    nc = seq_len // CHUNK
    assert nc % 2 == 0
    zero_init = s_dn0 is None
    full = lambda a: pl.BlockSpec(a.shape, lambda b: (0,) * a.ndim)
    args = [proj, dn_conv_w, sc_conv_w, ssd_conv_w, lanev, ssdv]
    in_specs = [pl.BlockSpec((1, seq_len, D_IN_PAD), lambda b: (b, 0, 0), pipeline_mode=pl.Buffered(1))
                if seq_len > 512 else pl.BlockSpec((1, seq_len, D_IN_PAD), lambda b: (b, 0, 0)),
                full(dn_conv_w), full(sc_conv_w), full(ssd_conv_w), full(lanev), full(ssdv)]
    if not zero_init:
        args += [s_dn0, s_ssd0]
        in_specs += [pl.BlockSpec((1, 2, H_A, DK_A, DV_A), lambda b: (b, 0, 0, 0, 0)),
                     pl.BlockSpec((1, 2, H_C, P_C, N_C), lambda b: (b, 0, 0, 0, 0))]
    out_shape = [jax.ShapeDtypeStruct((nb, seq_len, D_MODEL), BF16)]
    out_specs = [pl.BlockSpec((1, seq_len, D_MODEL), lambda b: (b, 0, 0))]
    if emit_state:
        out_shape += [jax.ShapeDtypeStruct((nb, 2, H_A, DK_A, DV_A), F32),
                      jax.ShapeDtypeStruct((nb, 2, H_C, P_C, N_C), F32)]
        out_specs += [pl.BlockSpec((1, 2, H_A, DK_A, DV_A), lambda b: (b, 0, 0, 0, 0)),
                      pl.BlockSpec((1, 2, H_C, P_C, N_C), lambda b: (b, 0, 0, 0, 0))]
    scratch = [pltpu.VMEM((seq_len, 3 * A_W), F32),
               pltpu.VMEM((seq_len, C_W), F32),
               pltpu.VMEM((seq_len, G_C * LANES), F32),
               pltpu.VMEM((seq_len, G_C * LANES), F32),
               pltpu.VMEM((2, seq_len, LANES), F32),
               pltpu.VMEM((2, nc, LANES, CHUNK), F32),
               pltpu.VMEM((2, nc, 8, LANES), F32),
               pltpu.VMEM((nc, 8, LANES), F32),
               pltpu.VMEM((seq_len, LANES), F32),
               pltpu.VMEM((seq_len, LANES), F32),
               pltpu.VMEM((2, nc, H_A, CHUNK, DV_A), F32),
               pltpu.VMEM((2, nc, H_A, 2 * CHUNK, DK_A), BF16),
               pltpu.VMEM((2, nc, H_A, CHUNK, DK_A), BF16),
               pltpu.VMEM((2, nc, H_A, CHUNK, CHUNK), BF16),
               pltpu.VMEM((seq_len, A_W), F32),
               pltpu.VMEM((seq_len, C_W), F32),
               pltpu.VMEM((2 * H_A, DK_A, DV_A), F32),
               pltpu.VMEM((2, G_C, HPG * P_C, HPG * N_C), F32)]
    return pl.pallas_call(
        functools.partial(_mixer_kernel, seq_len=seq_len, zero_init=zero_init, emit_state=emit_state),
        grid=(nb,),
        in_specs=in_specs,
        out_specs=out_specs,
        out_shape=out_shape,
        scratch_shapes=scratch,
        compiler_params=_params(("arbitrary",)),
        name="mixer",
    )(*args)


def _top2_sum(a, b, c, d):
    hi1, lo1 = jnp.maximum(a, b), jnp.minimum(a, b)
    hi2, lo2 = jnp.maximum(c, d), jnp.minimum(c, d)
    return jnp.maximum(hi1, hi2) + jnp.maximum(jnp.minimum(hi1, hi2), jnp.maximum(lo1, lo2))


def _outproj_kernel(y_ref, x_ref, mod_ref, w_ref, g_ref, rw_ref, rb_ref, x1_ref, h2_ref, comb_ref, wb_ref):
    @pl.when(pl.program_id(0) == 0)
    def _():
        wb_ref[...] = w_ref[...].astype(wb_ref.dtype)

    m = mod_ref[0]
    gate1, shift2, scale2 = m[2:3, :], m[3:4, :], m[4:5, :]
    x1 = x_ref[...] + gate1 * _dot(y_ref[...], wb_ref[...])
    x1_ref[...] = x1
    h2 = _rms(x1, g_ref[...]) * (1.0 + scale2) + shift2
    h2_ref[...] = h2.astype(h2_ref.dtype)

    h_hi = h2.astype(BF16)
    h_lo = (h2 - h_hi.astype(F32)).astype(BF16)
    rw_hi, rw_lo = rw_ref[0], rw_ref[1]
    hl = jnp.dot(jnp.concatenate([h_hi, h_lo], axis=0), rw_hi, preferred_element_type=F32)
    tm = h2.shape[0]
    logits = hl[:tm] + hl[tm:] + jnp.dot(h_hi, rw_lo, preferred_element_type=F32)
    scores = _sigmoid(logits.T[:N_EXPERTS, :])
    biased = scores + rb_ref[...]
    sc = [scores[e:e + 1, :] for e in range(N_EXPERTS)]
    bi = [biased[e:e + 1, :] for e in range(N_EXPERTS)]
    gs = [_top2_sum(*bi[EPG * g:EPG * (g + 1)]) for g in range(N_GROUPS)]
    gmax = functools.reduce(jnp.maximum, gs)
    first = []
    taken = None
    for g in range(N_GROUPS):
        hit = gs[g] == gmax
        if taken is None:
            first.append(hit)
            taken = hit
        else:
            first.append(hit & jnp.logical_not(taken))
            taken = taken | hit

    def pick(vals, j):
        out = vals[EPG * (N_GROUPS - 1) + j]
        for g in range(N_GROUPS - 2, -1, -1):
            out = jnp.where(first[g], vals[EPG * g + j], out)
        return out

    ib = [pick(bi, j) for j in range(EPG)]
    isc = [pick(sc, j) for j in range(EPG)]
    sel = []
    for j in range(EPG):
        cnt = jnp.zeros_like(ib[j])
        for i in range(EPG):
            if i == j:
                continue
            ahead = (ib[i] > ib[j]) | ((ib[i] == ib[j]) if i < j else False)
            cnt = cnt + jnp.where(ahead, 1.0, 0.0)
        sel.append(cnt < 2.0)
    wj = [jnp.where(sel[j], isc[j], 0.0) for j in range(EPG)]
    denom = functools.reduce(lambda a, b: a + b, wj)
    for g in range(N_GROUPS):
        for j in range(EPG):
            comb_ref[EPG * g + j:EPG * g + j + 1, :] = jnp.where(first[g], wj[j] / denom, 0.0)


def _outproj(ycat, x, mod6, seq_rows, w_out, layer, norm_g, router_w_pad, router_b):
    t, d = x.shape
    tm = _tile(seq_rows, 256)
    return pl.pallas_call(
        _outproj_kernel,
        grid=(t // tm,),
        in_specs=[pl.BlockSpec((tm, d), lambda i: (i, 0)),
                  pl.BlockSpec((tm, d), lambda i: (i, 0)),
                  pl.BlockSpec((1, 6, d), lambda i: ((i * tm) // seq_rows, 0, 0)),
                  pl.BlockSpec((None, d, d), lambda i: (layer, 0, 0)),
                  pl.BlockSpec((1, d), lambda i: (0, 0)),
                  pl.BlockSpec((2, d, LANES), lambda i: (0, 0, 0)),
                  pl.BlockSpec((N_EXPERTS, 1), lambda i: (0, 0))],
        out_specs=[pl.BlockSpec((tm, d), lambda i: (i, 0)),
                   pl.BlockSpec((tm, d), lambda i: (i, 0)),
                   pl.BlockSpec((N_EXPERTS, tm), lambda i: (0, i))],
        out_shape=[jax.ShapeDtypeStruct((t, d), F32),
                   jax.ShapeDtypeStruct((t, d), BF16),
                   jax.ShapeDtypeStruct((N_EXPERTS, t), F32)],
        scratch_shapes=[pltpu.VMEM((d, d), BF16)],
        compiler_params=_params(("arbitrary",)),
        name="outproj_route",
    )(ycat, x, mod6, w_out, norm_g.reshape(1, d), router_w_pad, router_b.reshape(N_EXPERTS, 1))


def _moe_kernel(h_ref, comb_ref, wg_ref, wu_ref, wd_ref, x1_ref, mod_ref, fg_ref, o_ref, *, final):
    g = pl.program_id(1)
    h = h_ref[...]
    comb = comb_ref[...]
    acts = [(_silu(_dot(h, wg_ref[j])) * _dot(h, wu_ref[j]) * comb[:, j:j + 1]).astype(BF16) for j in range(EPG)]
    y = _dot(jnp.concatenate(acts, axis=1), wd_ref[...])

    @pl.when(g == 0)
    def _():
        o_ref[...] = y

    @pl.when(g > 0)
    def _():
        o_ref[...] += y

    @pl.when(g == pl.num_programs(1) - 1)
    def _():
        x2 = x1_ref[...] + mod_ref[0][5:6, :] * o_ref[...]
        o_ref[...] = _rms(x2, fg_ref[...]) if final else x2


def _moe(h2, comb, w_gate, w_up, w_down, layer, x1, mod6, seq_rows, final_g, final):
    t, d = x1.shape
    tm = _tile(seq_rows, 1024)
    return pl.pallas_call(
        functools.partial(_moe_kernel, final=final),
        grid=(t // tm, N_GROUPS),
        in_specs=[pl.BlockSpec((tm, d), lambda i, g: (i, 0)),
                  pl.BlockSpec((None, tm, EPG), lambda i, g: (g, i, 0)),
                  pl.BlockSpec((None, EPG, d, D_EXPERT), lambda i, g: (layer, g, 0, 0)),
                  pl.BlockSpec((None, EPG, d, D_EXPERT), lambda i, g: (layer, g, 0, 0)),
                  pl.BlockSpec((None, None, EPG * D_EXPERT, d), lambda i, g: (layer, g, 0, 0)),
                  pl.BlockSpec((tm, d), lambda i, g: (i, 0)),
                  pl.BlockSpec((1, 6, d), lambda i, g: ((i * tm) // seq_rows, 0, 0)),
                  pl.BlockSpec((1, d), lambda i, g: (0, 0))],
        out_specs=pl.BlockSpec((tm, d), lambda i, g: (i, 0)),
        out_shape=jax.ShapeDtypeStruct((t, d), F32),
        compiler_params=_params(("arbitrary", "arbitrary")),
        name="experts",
    )(h2, comb, w_gate, w_up, w_down, x1, mod6, final_g.reshape(1, d))


def _lane_row(*pieces):
    row = jnp.zeros((LANES,), F32)
    for lane, vals in pieces:
        row = lax.dynamic_update_slice(row, vals.reshape(-1).astype(F32), (lane,))
    return row


def kernel(x_prompt, x_sample, state_delta, state_ssd, c, c_ctx, mod_w, mod_b, norm1_g, norm2_g, w_in, w_out,
           dn_conv_w, dn_a_log, dn_dt_bias, dn_norm_g, sc_conv_w, ssd_conv_w, ssd_a_log, ssd_dt_bias, ssd_d,
           ssd_norm_g, router_w, router_b, exp_w_gate, exp_w_up, exp_w_down, final_norm_g):
    depth = mod_w.shape[0]
    n_ctx, seq, d = x_prompt.shape
    n_dec, dec_seq, _ = x_sample.shape
    assert seq % CHUNK == 0 and dec_seq % CHUNK == 0 and d == D_MODEL

    n_rows = -(-(1 + n_dec) // 8) * 8
    cond_rows = jnp.zeros((n_rows, d), F32).at[0].set(c_ctx).at[1:1 + n_dec].set(c)
    mod = _modulation(cond_rows, mod_w, mod_b).reshape(depth, n_rows, 6, d)

    w_in_r = _reorder_w_in(w_in)
    w_down_g = exp_w_down.reshape(depth, N_GROUPS, EPG * D_EXPERT, d)
    rw = jnp.pad(router_w.astype(F32), ((0, 0), (0, LANES - N_EXPERTS)))
    rw_hi = rw.astype(BF16)
    router_w_pad = jnp.stack([rw_hi, (rw - rw_hi.astype(F32)).astype(BF16)])

    xp = x_prompt.reshape(n_ctx * seq, d)
    xs = _add_pos(x_sample, _grid_pos_embed(dec_seq, d).astype(x_sample.dtype)).reshape(n_dec * dec_seq, d)

    new_dn, new_ssd = [], []
    for l in range(depth):
        lanev = jnp.zeros((8, LANES), F32)
        lanev = lanev.at[0].set(_lane_row((LANE_ALPHA, dn_a_log[l]), (LANE_DT, ssd_a_log[l])))
        lanev = lanev.at[1].set(_lane_row((LANE_ALPHA, dn_dt_bias[l]), (LANE_DT, ssd_dt_bias[l])))
        lanev = lanev.at[2].set(dn_norm_g[l].astype(F32))
        ssdv = jnp.zeros((8, C_W), F32).at[0].set(jnp.repeat(ssd_d[l].astype(F32), P_C)).at[1].set(ssd_norm_g[l])
        final = l == depth - 1

        def block(x, mod6, nb, seq_len, seq_rows, s_dn0, s_ssd0, emit_state):
            proj = _inproj(x, mod6, seq_rows, norm1_g[l], w_in_r, l).reshape(nb, seq_len, D_IN_PAD)
            outs = _mixer(proj, dn_conv_w[l], sc_conv_w[l], ssd_conv_w[l], lanev, ssdv, s_dn0, s_ssd0, emit_state)
            ycat = outs[0].reshape(nb * seq_len, d)
            x1, h2, comb_t = _outproj(ycat, x, mod6, seq_rows, w_out, l, norm2_g[l], router_w_pad, router_b)
            comb = comb_t.reshape(N_GROUPS, EPG, -1).transpose(0, 2, 1)
            x2 = _moe(h2, comb, exp_w_gate, exp_w_up, w_down_g, l, x1, mod6, seq_rows, final_norm_g, final)
            return x2, outs[1:]

        xp, (s_dn, s_ssd) = block(xp, mod[l, 0:1], n_ctx, seq, n_ctx * seq, None, None, True)
        new_dn.append(s_dn)
        new_ssd.append(s_ssd)
        xs, _ = block(xs, mod[l, 1:1 + n_dec], n_dec, dec_seq, dec_seq,
                      state_delta[:, l].astype(F32), state_ssd[:, l].astype(F32), False)

    return (xp.reshape(n_ctx, seq, d), xs.reshape(n_dec, dec_seq, d),
            jnp.stack(new_dn, axis=1), jnp.stack(new_ssd, axis=1))
```

```python
import functools
import math

import jax
import jax.numpy as jnp
import numpy as np
from jax import lax
from jax.experimental import pallas as pl
from jax.experimental.pallas import tpu as pltpu

F32 = jnp.float32
BF16 = jnp.bfloat16

D_MODEL = 1024
GRID_W = 64
POS_BASE = 10000.0
H_A, DK_A, DV_A = 4, 128, 128
A_W = H_A * DV_A
H_C, P_C, N_C, G_C = 4, 64, 64, 2
HPG = H_C // G_C
C_W = H_C * P_C
B_W = D_MODEL - A_W - C_W
XBC_W = C_W + 2 * G_C * N_C
CHUNK = 64
N_EXPERTS = 16
N_GROUPS = 4
EPG = N_EXPERTS // N_GROUPS
D_EXPERT = 256
EPS = 1e-6
LANES = 128

COL_Q, COL_K, COL_V, COL_GATE = 0, A_W, 2 * A_W, 3 * A_W
COL_SCH = 4 * A_W
COL_SCB = COL_SCH + B_W
COL_SCC = COL_SCB + B_W
COL_Z = COL_SCC + B_W
COL_XBC = COL_Z + C_W
COL_SMALL = COL_XBC + XBC_W
D_IN_PAD = COL_SMALL + LANES
LANE_BETA, LANE_ALPHA, LANE_DT = 0, 2 * H_A, 4 * H_A
SRC_SMALL_A = 4 * A_W
SRC_SCH = SRC_SMALL_A + 4 * H_A
SRC_SMALL_B = SRC_SCH + 3 * B_W + C_W + XBC_W
D_IN = SRC_SMALL_B + 2 * H_C

assert HPG == 2 and P_C == N_C == CHUNK and HPG * P_C == LANES and G_C * N_C == LANES and DK_A == DV_A == LANES

VMEM_LIMIT = 56 * 1024 * 1024


def _dot(a, b):
    return jnp.dot(a.astype(BF16), b.astype(BF16), preferred_element_type=F32)


def _dot_nt(a, b):
    return lax.dot_general(a.astype(BF16), b.astype(BF16), (((1,), (1,)), ((), ())), preferred_element_type=F32)


def _dot_tn(a, b):
    return lax.dot_general(a.astype(BF16), b.astype(BF16), (((0,), (0,)), ((), ())), preferred_element_type=F32)


def _dot_f32(a, b):
    return jnp.dot(a, b, precision=lax.Precision.HIGHEST, preferred_element_type=F32)


def _dot_nt_f32(a, b):
    return lax.dot_general(a, b, (((1,), (1,)), ((), ())), precision=lax.Precision.HIGHEST,
                           preferred_element_type=F32)


def _silu(x):
    return x * (1.0 / (1.0 + jnp.exp(-x)))


def _sigmoid(x):
    return 1.0 / (1.0 + jnp.exp(-x))


def _softplus(x):
    return jnp.maximum(x, 0.0) + jnp.log1p(jnp.exp(-jnp.abs(x)))


def _rms(x, g):
    return x * lax.rsqrt(jnp.mean(x * x, axis=-1, keepdims=True) + EPS) * g


def _tile(n, pref):
    t = min(n, pref)
    while n % t:
        t -= 8
    assert t > 0 and t % 8 == 0, (n, pref)
    return t


def _params(sem):
    return pltpu.CompilerParams(dimension_semantics=sem, vmem_limit_bytes=VMEM_LIMIT)


def _mod_kernel(cond_ref, w_ref, b_ref, o_ref):
    s = _silu(cond_ref[...])
    o_ref[0] = _dot_f32(s, w_ref[0]) + b_ref[0]


def _modulation(cond_rows, mod_w, mod_b):
    depth, d, n = mod_w.shape
    r = cond_rows.shape[0]
    tn = _tile(n, 1536)
    return pl.pallas_call(
        _mod_kernel,
        grid=(depth, n // tn),
        in_specs=[pl.BlockSpec((r, d), lambda l, j: (0, 0)),
                  pl.BlockSpec((1, d, tn), lambda l, j: (l, 0, j)),
                  pl.BlockSpec((1, 1, tn), lambda l, j: (l, 0, j))],
        out_specs=pl.BlockSpec((1, r, tn), lambda l, j: (l, 0, j)),
        out_shape=jax.ShapeDtypeStruct((depth, r, n), F32),
        compiler_params=_params(("arbitrary", "arbitrary")),
        name="modulation",
    )(cond_rows, mod_w, mod_b.reshape(depth, 1, n))


def _add_kernel(x_ref, p_ref, o_ref):
    o_ref[0] = x_ref[0] + p_ref[...]


def _add_pos(x, pos):
    nb, l, d = x.shape
    tl = _tile(l, 512)
    return pl.pallas_call(
        _add_kernel,
        grid=(nb, l // tl),
        in_specs=[pl.BlockSpec((1, tl, d), lambda b, i: (b, i, 0)),
                  pl.BlockSpec((tl, d), lambda b, i: (i, 0))],
        out_specs=pl.BlockSpec((1, tl, d), lambda b, i: (b, i, 0)),
        out_shape=jax.ShapeDtypeStruct(x.shape, x.dtype),
        compiler_params=_params(("arbitrary", "arbitrary")),
        name="add_pos",
    )(x, pos)


def _grid_pos_embed(n_tok, dim):
    rows = n_tok // GRID_W
    rr, cc = np.meshgrid(np.arange(rows, dtype=np.float64), np.arange(GRID_W, dtype=np.float64), indexing="ij")
    quarter = dim // 4
    omega = 1.0 / (POS_BASE ** (np.arange(quarter, dtype=np.float64) / quarter))
    ang_r = rr.reshape(-1, 1) * omega
    ang_c = cc.reshape(-1, 1) * omega
    return np.concatenate([np.sin(ang_r), np.cos(ang_r), np.sin(ang_c), np.cos(ang_c)], axis=-1)


def _inproj_kernel(x_ref, mod_ref, g_ref, w_ref, o_ref):
    m = mod_ref[0]
    shift, scale = m[0:1, :], m[1:2, :]
    h = _rms(x_ref[...], g_ref[...]) * (1.0 + scale) + shift
    o_ref[...] = _dot(h, w_ref[...])


def _inproj(x, mod6, seq_rows, norm_g, w_in_r, layer):
    t, d = x.shape
    tm = _tile(seq_rows, 256)
    return pl.pallas_call(
        _inproj_kernel,
        grid=(t // tm,),
        in_specs=[pl.BlockSpec((tm, d), lambda i: (i, 0)),
                  pl.BlockSpec((1, 6, d), lambda i: ((i * tm) // seq_rows, 0, 0)),
                  pl.BlockSpec((1, d), lambda i: (0, 0)),
                  pl.BlockSpec((None, d, D_IN_PAD), lambda i: (layer, 0, 0))],
        out_specs=pl.BlockSpec((tm, D_IN_PAD), lambda i: (i, 0)),
        out_shape=jax.ShapeDtypeStruct((t, D_IN_PAD), F32),
        compiler_params=_params(("arbitrary",)),
        name="inproj",
    )(x, mod6, norm_g.reshape(1, d), w_in_r)


def _conv3(load, r0, z, nc, seq_len, w):
    cur = load(pl.ds(r0, CHUNK))
    prev = load(pl.ds(pl.multiple_of(jnp.maximum(r0 - 8, 0), 8), 8))[7:8, :]
    nxt = load(pl.ds(pl.multiple_of(jnp.minimum(r0 + CHUNK, seq_len - 8), 8), 8))[0:1, :]
    prev = jnp.where(z > 0, prev, 0.0)
    nxt = jnp.where(z < nc - 1, nxt, 0.0)
    ri = lax.broadcasted_iota(jnp.int32, cur.shape, 0)
    x_prev = jnp.where(ri == 0, prev, pltpu.roll(cur, 1, 0))
    x_next = jnp.where(ri == CHUNK - 1, nxt, pltpu.roll(cur, CHUNK - 1, 0))
    return w[0:1, :] * x_prev + w[1:2, :] * cur + w[2:3, :] * x_next


def _mixer_kernel(*refs, seq_len, zero_init, state_layer, state_layers, state_aliased):
    refs = list(refs)
    proj, dnw, scw, ssw, lanev, ssdv = refs[:6]
    k = 6
    if not zero_init:
        sdn0, sssd0 = refs[k:k + 2]
        k += 2
    emit_state = state_layer is not None
    if state_aliased:
        k += 2
    ycat = refs[k]
    k += 1
    if emit_state:
        sdn_out, sssd_out = refs[k:k + 2]
        k += 2
    (qkv_s, xbc_s, bdup_s, cdup_s, gc_s, gcrow_s, gcpair_s, tot_s, sp_s, beta_s, u_s, wq_s, kd_s, qk_s, o_s, y_s,
     st_s, hs_s) = refs[k:]

    nc = seq_len // CHUNK
    ri = lax.broadcasted_iota(jnp.int32, (CHUNK, CHUNK), 0)
    ci = lax.broadcasted_iota(jnp.int32, (CHUNK, CHUNK), 1)
    tril = (ri >= ci).astype(F32)
    eye = (ri == ci).astype(F32)
    r128 = lax.broadcasted_iota(jnp.int32, (LANES, LANES), 0)
    c128 = lax.broadcasted_iota(jnp.int32, (LANES, LANES), 1)
    eye128 = (r128 == c128).astype(F32)
    incl = (ri >= ci, ri <= ci)
    strict = (ri > ci, ri < ci)
    n_lvl = int(math.log2(CHUNK))
    lvl = [((ri >> s) == (ci >> s)) & ((ri >> (s - 1)) != (ci >> (s - 1))) for s in range(1, n_lvl + 1)]
    alog = lanev[0:1, :]
    bias = lanev[1:2, :]

    def chunk_rows(z):
        return pl.ds(pl.multiple_of(z * CHUNK, CHUNK), CHUNK)

    def prep(z, carry):
        r0 = pl.multiple_of(z * CHUNK, CHUNK)
        rows = pl.ds(r0, CHUNK)
        for j in range(3 * H_A):
            c0 = j * LANES
            a = _silu(_conv3(lambda rs: proj[0, rs, c0:c0 + LANES], r0, z, nc, seq_len, dnw[:, c0:c0 + LANES]))
            if j < 2 * H_A:
                a = a * lax.rsqrt(jnp.sum(a * a, axis=-1, keepdims=True) + EPS)
            if j < H_A:
                a = a * (DK_A ** -0.5)
            qkv_s[rows, c0:c0 + LANES] = a
        for j in range(B_W // LANES):
            c0 = j * LANES
            cv = _conv3(lambda rs: proj[0, rs, COL_SCC + c0:COL_SCC + c0 + LANES]
                        * proj[0, rs, COL_SCH + c0:COL_SCH + c0 + LANES],
                        r0, z, nc, seq_len, scw[:, c0:c0 + LANES])
            yb = proj[0, rows, COL_SCB + c0:COL_SCB + c0 + LANES] * cv
            ycat[0, rows, A_W + c0:A_W + c0 + LANES] = yb.astype(ycat.dtype)
        for j in range(XBC_W // LANES):
            c0 = j * LANES
            a = _silu(_conv3(lambda rs: proj[0, rs, COL_XBC + c0:COL_XBC + c0 + LANES], r0, z, nc, seq_len,
                             ssw[:, c0:c0 + LANES]))
            if c0 < C_W:
                xbc_s[rows, c0:c0 + LANES] = a
            else:
                swapped = pltpu.roll(a, N_C, 1)
                lo = lax.broadcasted_iota(jnp.int32, a.shape, 1) < N_C
                dup_s = bdup_s if c0 < C_W + G_C * N_C else cdup_s
                dup_s[rows, 0:LANES] = jnp.where(lo, a, swapped)
                dup_s[rows, LANES:2 * LANES] = jnp.where(lo, swapped, a)
        sm = proj[0, rows, COL_SMALL:COL_SMALL + LANES]
        sp = _softplus(sm + bias)
        g = -jnp.exp(alog) * sp
        pre = _dot_f32(tril, g)
        tot = pre[CHUNK - 1:CHUNK, :]
        suf = tot - pre + g
        sp_s[rows, :] = sp
        beta_s[rows, :] = _sigmoid(sm)
        gc_s[0, rows, :] = pre
        gc_s[1, rows, :] = suf
        for d, gcd in enumerate((pre, suf)):
            gr = _dot_nt_f32(eye128, gcd)
            gcrow_s[d, z] = gr
            for g in range(G_C):
                ln = LANE_DT + d * H_C + g * HPG
                gcpair_s[d, z, g:g + 1, :] = jnp.concatenate([gr[ln:ln + 1, :], gr[ln + 1:ln + 2, :]], axis=1)
        tot_s[z] = jnp.broadcast_to(tot, (8, LANES))
        o_s[rows, :] = jnp.zeros((CHUNK, A_W), F32)
        y_s[rows, :] = jnp.zeros((CHUNK, C_W), F32)
        return carry

    lax.fori_loop(0, nc, prep, 0, unroll=2)

    for d in range(2):
        for h in range(H_A):
            st_s[d * H_A + h] = jnp.zeros((DK_A, DV_A), F32) if zero_init else sdn0[0, d, h]
        for g in range(G_C):
            if zero_init:
                hs_s[d, g] = jnp.zeros((HPG * P_C, HPG * N_C), F32)
            else:
                zero = jnp.zeros((P_C, N_C), F32)
                hs_s[d, g] = jnp.concatenate(
                    [jnp.concatenate([sssd0[0, d, g * HPG], zero], axis=1),
                     jnp.concatenate([zero, sssd0[0, d, g * HPG + 1]], axis=1)], axis=0)

    cpi = 4 if nc % 4 == 0 else 2

    def delta_prep(i, carry):
        units = []
        for zz in range(cpi):
            z = cpi * i + zz
            rows = chunk_rows(z)
            tot = tot_s[z][0:1, :]
            beta = beta_s[rows, :]
            gcs = [gc_s[d, rows, :] for d in range(2)]
            grs = [gcrow_s[d, z] for d in range(2)]
            for h in range(H_A):
                q_h = qkv_s[rows, COL_Q + h * DK_A:COL_Q + (h + 1) * DK_A]
                k_h = qkv_s[rows, COL_K + h * DK_A:COL_K + (h + 1) * DK_A]
                v_h = qkv_s[rows, COL_V + h * DV_A:COL_V + (h + 1) * DV_A]
                units.append(dict(z=z, h=h, q=q_h, k=k_h, v=v_h, tot=tot, beta=beta, gcs=gcs, grs=grs))
        qkk = [_dot_nt(jnp.concatenate([p["q"], p["k"]], axis=0), p["k"]) for p in units]
        dus = []
        for p, qk_kk in zip(units, qkk):
            for d in range(2):
                h = p["h"]
                ln = LANE_ALPHA + d * H_A + h
                a_col = p["gcs"][d][:, ln:ln + 1]
                a_row = p["grs"][d][ln:ln + 1, :]
                t_col = p["tot"][:, ln:ln + 1]
                decay = jnp.exp(jnp.where(incl[d], a_col - a_row, -1e30))
                b_col = p["beta"][:, LANE_BETA + d * H_A + h:LANE_BETA + d * H_A + h + 1]
                eg = jnp.exp(a_col)
                m = jnp.where(strict[d], qk_kk[CHUNK:] * b_col * decay, 0.0)
                rhs = jnp.concatenate([p["v"] * b_col, p["k"] * (b_col * eg)], axis=1)
                idx = (d, p["z"], h)
                qk_s[idx] = (qk_kk[:CHUNK] * decay).astype(qk_s.dtype)
                kd_s[idx] = (p["k"] * jnp.exp(t_col - a_col)).astype(kd_s.dtype)
                wq_s[d, p["z"], h, CHUNK:, :] = (p["q"] * eg).astype(wq_s.dtype)
                dus.append(dict(idx=idx, m=m, rhs=rhs))
        t_inv = [eye - jnp.where(lvl[0], p["m"], 0.0) for p in dus]
        for s in range(1, n_lvl):
            x = [_dot(jnp.where(lvl[s], p["m"], 0.0), t) for p, t in zip(dus, t_inv)]
            t_inv = [t - _dot(t, xx) for t, xx in zip(t_inv, x)]
        uw = [_dot(t, p["rhs"]) for p, t in zip(dus, t_inv)]
        for p, r in zip(dus, uw):
            d, z, h = p["idx"]
            u_s[p["idx"]] = r[:, :DV_A]
            wq_s[d, z, h, :CHUNK, :] = r[:, DV_A:].astype(wq_s.dtype)
        return carry

    lax.fori_loop(0, nc // cpi, delta_prep, 0)

    pw = HPG * P_C
    ri2 = lax.broadcasted_iota(jnp.int32, (CHUNK, pw), 0)
    ci2 = lax.broadcasted_iota(jnp.int32, (CHUNK, pw), 1)
    lane_hi = ci2 >= P_C
    tj = ci2 & (CHUNK - 1)
    incl2 = (ri2 >= tj, ri2 <= tj)
    rb = lax.broadcasted_iota(jnp.int32, (pw, pw), 0) >= P_C
    cbk = lax.broadcasted_iota(jnp.int32, (pw, pw), 1) >= N_C
    diag_blk = rb == cbk

    def scan(z, carry):
        dus = [(d, (z if d == 0 else nc - 1 - z), h) for d in range(2) for h in range(H_A)]
        s_prev = [st_s[d * H_A + h] for d, _, h in dus]
        us = []
        for d in range(2):
            zc = z if d == 0 else nc - 1 - z
            rows = chunk_rows(zc)
            sp = sp_s[rows, :]
            tot = tot_s[zc][0:1, :]
            gc = gc_s[d, rows, :]
            for g in range(G_C):
                ln = LANE_DT + d * H_C + g * HPG
                gsl = slice(g * pw, (g + 1) * pw)
                a_pair = jnp.where(lane_hi, gc[:, ln + 1:ln + 2], gc[:, ln:ln + 1])
                t_pair = jnp.where(lane_hi, tot[:, ln + 1:ln + 2], tot[:, ln:ln + 1])
                lmat = jnp.exp(jnp.where(incl2[d], a_pair - gcpair_s[d, zc, g:g + 1, :], -1e30))
                xdt = xbc_s[rows, gsl] * jnp.where(lane_hi, sp[:, ln + 1:ln + 2], sp[:, ln:ln + 1])
                b_dup = bdup_s[rows, gsl]
                c_dup = cdup_s[rows, gsl]
                us.append(dict(d=d, g=g, rows=rows, gsl=gsl, lmat=lmat, xdt=xdt, c_dup=c_dup,
                               c_lo=jnp.where(lane_hi, 0.0, c_dup), b_st=jnp.concatenate([b_dup, b_dup], axis=0),
                               bdec=b_dup * jnp.exp(t_pair - a_pair), ea=jnp.exp(a_pair),
                               dec=jnp.where(rb, jnp.exp(tot[:, ln + 1:ln + 2]), jnp.exp(tot[:, ln:ln + 1])),
                               x_bd=jnp.concatenate([jnp.where(lane_hi, 0.0, xdt), jnp.where(lane_hi, xdt, 0.0)],
                                                    axis=0),
                               h_prev=hs_s[d, g]))
        ws_qs = [_dot(wq_s[idx], s) for idx, s in zip(dus, s_prev)]
        st = [_dot_tn(p["xdt"], p["bdec"]) for p in us]
        cb = [_dot_nt(p["c_lo"], p["b_st"]) for p in us]
        y_off = [_dot_nt(p["c_dup"], p["h_prev"]) for p in us]
        v_new = [u_s[idx] - r[:CHUNK] for idx, r in zip(dus, ws_qs)]
        o_in = [_dot(qk_s[idx], v) for idx, v in zip(dus, v_new)]
        s_add = [_dot_tn(kd_s[idx], v) for idx, v in zip(dus, v_new)]
        y_diag = [_dot(cbd * p["lmat"], p["x_bd"]) for p, cbd in zip(us, cb)]
        for p, yd, yo, s in zip(us, y_diag, y_off, st):
            y_s[p["rows"], p["gsl"]] = y_s[p["rows"], p["gsl"]] + yd + yo * p["ea"]
            hs_s[p["d"], p["g"]] = p["h_prev"] * p["dec"] + jnp.where(diag_blk, s, 0.0)
        for (d, zc, h), r, oi, sa, s in zip(dus, ws_qs, o_in, s_add, s_prev):
            ln = LANE_ALPHA + d * H_A + h
            cs = slice(h * DV_A, (h + 1) * DV_A)
            o_s[chunk_rows(zc), cs] = o_s[chunk_rows(zc), cs] + r[CHUNK:] + oi
            st_s[d * H_A + h] = s * jnp.exp(tot_s[zc][0:1, ln:ln + 1]) + sa
        return carry

    lax.fori_loop(0, nc, scan, 0, unroll=4)


    def finish(z, carry):
        rows = chunk_rows(z)
        for h in range(H_A):
            cs = slice(h * DV_A, (h + 1) * DV_A)
            o = _rms(o_s[rows, cs], lanev[2:3, :])
            o = o * _silu(proj[0, rows, COL_GATE + h * DV_A:COL_GATE + (h + 1) * DV_A])
            ycat[0, rows, cs] = o.astype(ycat.dtype)
        y = y_s[rows, :] + ssdv[0:1, :] * xbc_s[rows, 0:C_W]
        y = _rms(y * _silu(proj[0, rows, COL_Z:COL_Z + C_W]), ssdv[1:2, :])
        ycat[0, rows, A_W + B_W:] = y.astype(ycat.dtype)
        return carry

    lax.fori_loop(0, nc, finish, 0, unroll=2)

    if emit_state:
        slots = (None,) if state_aliased else range(state_layers)
        for slot in slots:
            dn_slot = sdn_out.at[0] if slot is None else sdn_out.at[0, slot]
            ssd_slot = sssd_out.at[0] if slot is None else sssd_out.at[0, slot]
            mine = slot is None or slot == state_layer
            for d in range(2):
                for h in range(H_A):
                    dn_slot[d, h] = st_s[d * H_A + h] if mine else jnp.zeros((DK_A, DV_A), F32)
                for h in range(H_C):
                    k0 = (h % HPG) * P_C
                    ssd_slot[d, h] = (hs_s[d, h // HPG][k0:k0 + P_C, k0:k0 + N_C] if mine
                                      else jnp.zeros((P_C, N_C), F32))


def _mixer(proj, dn_conv_w, sc_conv_w, ssd_conv_w, lanev, ssdv, s_dn0, s_ssd0, state_out):
    nb, seq_len, _ = proj.shape
    nc = seq_len // CHUNK
    assert nc % 2 == 0
    zero_init = s_dn0 is None
    full = lambda a: pl.BlockSpec(a.shape, lambda b: (0,) * a.ndim)
    args = [proj, dn_conv_w, sc_conv_w, ssd_conv_w, lanev, ssdv]
    in_specs = [pl.BlockSpec((1, seq_len, D_IN_PAD), lambda b: (b, 0, 0), pipeline_mode=pl.Buffered(1))
                if seq_len > 512 else pl.BlockSpec((1, seq_len, D_IN_PAD), lambda b: (b, 0, 0)),
                full(dn_conv_w), full(sc_conv_w), full(ssd_conv_w), full(lanev), full(ssdv)]
    if not zero_init:
        args += [s_dn0, s_ssd0]
        in_specs += [pl.BlockSpec((1, 2, H_A, DK_A, DV_A), lambda b: (b, 0, 0, 0, 0)),
                     pl.BlockSpec((1, 2, H_C, P_C, N_C), lambda b: (b, 0, 0, 0, 0))]
    out_shape = [jax.ShapeDtypeStruct((nb, seq_len, D_MODEL), BF16)]
    out_specs = [pl.BlockSpec((1, seq_len, D_MODEL), lambda b: (b, 0, 0))]
    aliases = {}
    layer, depth, prev = state_out if state_out is not None else (None, None, None)
    if state_out is not None:
        out_shape += [jax.ShapeDtypeStruct((nb, depth, 2, H_A, DK_A, DV_A), F32),
                      jax.ShapeDtypeStruct((nb, depth, 2, H_C, P_C, N_C), F32)]
        if prev is None:
            out_specs += [pl.BlockSpec((1, depth, 2, H_A, DK_A, DV_A), lambda b: (b, 0, 0, 0, 0, 0)),
                          pl.BlockSpec((1, depth, 2, H_C, P_C, N_C), lambda b: (b, 0, 0, 0, 0, 0))]
        else:
            aliases = {len(args): 1, len(args) + 1: 2}
            args += list(prev)
            in_specs += [pl.BlockSpec(memory_space=pl.ANY), pl.BlockSpec(memory_space=pl.ANY)]
            out_specs += [pl.BlockSpec((1, None, 2, H_A, DK_A, DV_A), lambda b: (b, layer, 0, 0, 0, 0)),
                          pl.BlockSpec((1, None, 2, H_C, P_C, N_C), lambda b: (b, layer, 0, 0, 0, 0))]
    scratch = [pltpu.VMEM((seq_len, 3 * A_W), F32),
               pltpu.VMEM((seq_len, C_W), F32),
               pltpu.VMEM((seq_len, G_C * LANES), F32),
               pltpu.VMEM((seq_len, G_C * LANES), F32),
               pltpu.VMEM((2, seq_len, LANES), F32),
               pltpu.VMEM((2, nc, LANES, CHUNK), F32),
               pltpu.VMEM((2, nc, 8, LANES), F32),
               pltpu.VMEM((nc, 8, LANES), F32),
               pltpu.VMEM((seq_len, LANES), F32),
               pltpu.VMEM((seq_len, LANES), F32),
               pltpu.VMEM((2, nc, H_A, CHUNK, DV_A), F32),
               pltpu.VMEM((2, nc, H_A, 2 * CHUNK, DK_A), BF16),
               pltpu.VMEM((2, nc, H_A, CHUNK, DK_A), BF16),
               pltpu.VMEM((2, nc, H_A, CHUNK, CHUNK), BF16),
               pltpu.VMEM((seq_len, A_W), F32),
               pltpu.VMEM((seq_len, C_W), F32),
               pltpu.VMEM((2 * H_A, DK_A, DV_A), F32),
               pltpu.VMEM((2, G_C, HPG * P_C, HPG * N_C), F32)]
    return pl.pallas_call(
        functools.partial(_mixer_kernel, seq_len=seq_len, zero_init=zero_init, state_layer=layer,
                          state_layers=depth, state_aliased=prev is not None),
        grid=(nb,),
        in_specs=in_specs,
        out_specs=out_specs,
        out_shape=out_shape,
        input_output_aliases=aliases,
        scratch_shapes=scratch,
        compiler_params=_params(("arbitrary",)),
        name="mixer",
    )(*args)


def _top2_sum(a, b, c, d):
    hi1, lo1 = jnp.maximum(a, b), jnp.minimum(a, b)
    hi2, lo2 = jnp.maximum(c, d), jnp.minimum(c, d)
    return jnp.maximum(hi1, hi2) + jnp.maximum(jnp.minimum(hi1, hi2), jnp.maximum(lo1, lo2))


def _outproj_kernel(y_ref, x_ref, mod_ref, w_ref, g_ref, rw_ref, rb_ref, x1_ref, h2_ref, comb_ref, wb_ref):
    @pl.when(pl.program_id(0) == 0)
    def _():
        wb_ref[...] = w_ref[...].astype(wb_ref.dtype)

    m = mod_ref[0]
    gate1, shift2, scale2 = m[2:3, :], m[3:4, :], m[4:5, :]
    x1 = x_ref[...] + gate1 * _dot(y_ref[...], wb_ref[...])
    x1_ref[...] = x1
    h2 = _rms(x1, g_ref[...]) * (1.0 + scale2) + shift2
    h2_ref[...] = h2.astype(h2_ref.dtype)

    h_hi = h2.astype(BF16)
    h_lo = (h2 - h_hi.astype(F32)).astype(BF16)
    rw_hi, rw_lo = rw_ref[0], rw_ref[1]
    hl = jnp.dot(jnp.concatenate([h_hi, h_lo], axis=0), rw_hi, preferred_element_type=F32)
    tm = h2.shape[0]
    logits = hl[:tm] + hl[tm:] + jnp.dot(h_hi, rw_lo, preferred_element_type=F32)
    scores = _sigmoid(logits.T[:N_EXPERTS, :])
    biased = scores + rb_ref[...]
    sc = [scores[e:e + 1, :] for e in range(N_EXPERTS)]
    bi = [biased[e:e + 1, :] for e in range(N_EXPERTS)]
    gs = [_top2_sum(*bi[EPG * g:EPG * (g + 1)]) for g in range(N_GROUPS)]
    gmax = functools.reduce(jnp.maximum, gs)
    first = []
    taken = None
    for g in range(N_GROUPS):
        hit = gs[g] == gmax
        if taken is None:
            first.append(hit)
            taken = hit
        else:
            first.append(hit & jnp.logical_not(taken))
            taken = taken | hit

    def pick(vals, j):
        out = vals[EPG * (N_GROUPS - 1) + j]
        for g in range(N_GROUPS - 2, -1, -1):
            out = jnp.where(first[g], vals[EPG * g + j], out)
        return out

    ib = [pick(bi, j) for j in range(EPG)]
    isc = [pick(sc, j) for j in range(EPG)]
    sel = []
    for j in range(EPG):
        cnt = jnp.zeros_like(ib[j])
        for i in range(EPG):
            if i == j:
                continue
            ahead = (ib[i] > ib[j]) | ((ib[i] == ib[j]) if i < j else False)
            cnt = cnt + jnp.where(ahead, 1.0, 0.0)
        sel.append(cnt < 2.0)
    wj = [jnp.where(sel[j], isc[j], 0.0) for j in range(EPG)]
    denom = functools.reduce(lambda a, b: a + b, wj)
    for g in range(N_GROUPS):
        for j in range(EPG):
            comb_ref[EPG * g + j:EPG * g + j + 1, :] = jnp.where(first[g], wj[j] / denom, 0.0)


def _outproj(ycat, x, mod6, seq_rows, w_out, layer, norm_g, router_w_pad, router_b):
    t, d = x.shape
    tm = _tile(seq_rows, 256)
    return pl.pallas_call(
        _outproj_kernel,
        grid=(t // tm,),
        in_specs=[pl.BlockSpec((tm, d), lambda i: (i, 0)),
                  pl.BlockSpec((tm, d), lambda i: (i, 0)),
                  pl.BlockSpec((1, 6, d), lambda i: ((i * tm) // seq_rows, 0, 0)),
                  pl.BlockSpec((None, d, d), lambda i: (layer, 0, 0)),
                  pl.BlockSpec((1, d), lambda i: (0, 0)),
                  pl.BlockSpec((2, d, LANES), lambda i: (0, 0, 0)),
                  pl.BlockSpec((N_EXPERTS, 1), lambda i: (0, 0))],
        out_specs=[pl.BlockSpec((tm, d), lambda i: (i, 0)),
                   pl.BlockSpec((tm, d), lambda i: (i, 0)),
                   pl.BlockSpec((N_EXPERTS, tm), lambda i: (0, i))],
        out_shape=[jax.ShapeDtypeStruct((t, d), F32),
                   jax.ShapeDtypeStruct((t, d), BF16),
                   jax.ShapeDtypeStruct((N_EXPERTS, t), F32)],
        scratch_shapes=[pltpu.VMEM((d, d), BF16)],
        compiler_params=_params(("arbitrary",)),
        name="outproj_route",
    )(ycat, x, mod6, w_out, norm_g.reshape(1, d), router_w_pad, router_b.reshape(N_EXPERTS, 1))


def _moe_kernel(h_ref, comb_ref, wg_ref, wu_ref, wd_ref, x1_ref, mod_ref, fg_ref, o_ref, *, final):
    g = pl.program_id(1)
    h = h_ref[...]
    comb = comb_ref[...]
    acts = [(_silu(_dot(h, wg_ref[j])) * _dot(h, wu_ref[j]) * comb[:, j:j + 1]).astype(BF16) for j in range(EPG)]
    y = _dot(jnp.concatenate(acts, axis=1), wd_ref[...])

    @pl.when(g == 0)
    def _():
        o_ref[...] = y

    @pl.when(g > 0)
    def _():
        o_ref[...] += y

    @pl.when(g == pl.num_programs(1) - 1)
    def _():
        x2 = x1_ref[...] + mod_ref[0][5:6, :] * o_ref[...]
        o_ref[...] = _rms(x2, fg_ref[...]) if final else x2


def _moe(h2, comb, w_gate, w_up, w_down, layer, x1, mod6, seq_rows, final_g, final):
    t, d = x1.shape
    tm = _tile(seq_rows, 1024)
    return pl.pallas_call(
        functools.partial(_moe_kernel, final=final),
        grid=(t // tm, N_GROUPS),
        in_specs=[pl.BlockSpec((tm, d), lambda i, g: (i, 0)),
                  pl.BlockSpec((None, tm, EPG), lambda i, g: (g, i, 0)),
                  pl.BlockSpec((None, EPG, d, D_EXPERT), lambda i, g: (layer, g, 0, 0)),
                  pl.BlockSpec((None, EPG, d, D_EXPERT), lambda i, g: (layer, g, 0, 0)),
                  pl.BlockSpec((None, None, EPG * D_EXPERT, d), lambda i, g: (layer, g, 0, 0)),
                  pl.BlockSpec((tm, d), lambda i, g: (i, 0)),
                  pl.BlockSpec((1, 6, d), lambda i, g: ((i * tm) // seq_rows, 0, 0)),
                  pl.BlockSpec((1, d), lambda i, g: (0, 0))],
        out_specs=pl.BlockSpec((tm, d), lambda i, g: (i, 0)),
        out_shape=jax.ShapeDtypeStruct((t, d), F32),
        compiler_params=_params(("arbitrary", "arbitrary")),
        name="experts",
    )(h2, comb, w_gate, w_up, w_down, x1, mod6, final_g.reshape(1, d))


def _lane_row(*pieces):
    row = jnp.zeros((LANES,), F32)
    for lane, vals in pieces:
        row = lax.dynamic_update_slice(row, vals.reshape(-1).astype(F32), (lane,))
    return row


def kernel(x_prompt, x_sample, state_delta, state_ssd, c, c_ctx, mod_w, mod_b, norm1_g, norm2_g, w_in, w_out,
           dn_conv_w, dn_a_log, dn_dt_bias, dn_norm_g, sc_conv_w, ssd_conv_w, ssd_a_log, ssd_dt_bias, ssd_d,
           ssd_norm_g, router_w, router_b, exp_w_gate, exp_w_up, exp_w_down, final_norm_g):
    depth = mod_w.shape[0]
    n_ctx, seq, d = x_prompt.shape
    n_dec, dec_seq, _ = x_sample.shape
    assert seq % CHUNK == 0 and dec_seq % CHUNK == 0 and d == D_MODEL

    n_rows = -(-(1 + n_dec) // 8) * 8
    cond_rows = jnp.zeros((n_rows, d), F32).at[0].set(c_ctx).at[1:1 + n_dec].set(c)
    mod = _modulation(cond_rows, mod_w, mod_b).reshape(depth, n_rows, 6, d)

    w_in_r = jnp.concatenate(
        [w_in[:, :, :SRC_SMALL_A], w_in[:, :, SRC_SCH:SRC_SMALL_B], w_in[:, :, SRC_SMALL_A:SRC_SCH],
         w_in[:, :, SRC_SMALL_B:], jnp.zeros((depth, d, D_IN_PAD - D_IN), w_in.dtype)], axis=-1).astype(BF16)
    w_down_g = exp_w_down.reshape(depth, N_GROUPS, EPG * D_EXPERT, d)
    rw = jnp.pad(router_w.astype(F32), ((0, 0), (0, LANES - N_EXPERTS)))
    rw_hi = rw.astype(BF16)
    router_w_pad = jnp.stack([rw_hi, (rw - rw_hi.astype(F32)).astype(BF16)])

    xp = x_prompt.reshape(n_ctx * seq, d)
    pos = jnp.asarray(_grid_pos_embed(dec_seq, d), dtype=x_sample.dtype)
    xs = _add_pos(x_sample, pos).reshape(n_dec * dec_seq, d)

    states = None
    for l in range(depth):
        lanev = jnp.zeros((8, LANES), F32)
        lanev = lanev.at[0].set(_lane_row((LANE_ALPHA, dn_a_log[l]), (LANE_DT, ssd_a_log[l])))
        lanev = lanev.at[1].set(_lane_row((LANE_ALPHA, dn_dt_bias[l]), (LANE_DT, ssd_dt_bias[l])))
        lanev = lanev.at[2].set(dn_norm_g[l].astype(F32))
        ssdv = jnp.zeros((8, C_W), F32).at[0].set(jnp.repeat(ssd_d[l].astype(F32), P_C)).at[1].set(ssd_norm_g[l])
        final = l == depth - 1

        def block(x, mod6, nb, seq_len, seq_rows, s_dn0, s_ssd0, state_out):
            proj = _inproj(x, mod6, seq_rows, norm1_g[l], w_in_r, l).reshape(nb, seq_len, D_IN_PAD)
            outs = _mixer(proj, dn_conv_w[l], sc_conv_w[l], ssd_conv_w[l], lanev, ssdv, s_dn0, s_ssd0, state_out)
            ycat = outs[0].reshape(nb * seq_len, d)
            x1, h2, comb_t = _outproj(ycat, x, mod6, seq_rows, w_out, l, norm2_g[l], router_w_pad, router_b)
            comb = comb_t.reshape(N_GROUPS, EPG, -1).transpose(0, 2, 1)
            x2 = _moe(h2, comb, exp_w_gate, exp_w_up, w_down_g, l, x1, mod6, seq_rows, final_norm_g, final)
            return x2, outs[1:]

        xp, states = block(xp, mod[l, 0:1], n_ctx, seq, n_ctx * seq, None, None, (l, depth, states))
        xs, _ = block(xs, mod[l, 1:1 + n_dec], n_dec, dec_seq, dec_seq,
                      state_delta[:, l].astype(F32), state_ssd[:, l].astype(F32), None)

    return (xp.reshape(n_ctx, seq, d), xs.reshape(n_dec, dec_seq, d), states[0], states[1])
```

```python
import functools
import math

import jax
import jax.numpy as jnp
import numpy as np
from jax import lax
from jax.experimental import pallas as pl
from jax.experimental.pallas import tpu as pltpu

F32 = jnp.float32
BF16 = jnp.bfloat16

D_MODEL = 1024
GRID_W = 64
POS_BASE = 10000.0
H_A, DK_A, DV_A = 4, 128, 128
A_W = H_A * DV_A
H_C, P_C, N_C, G_C = 4, 64, 64, 2
HPG = H_C // G_C
C_W = H_C * P_C
B_W = D_MODEL - A_W - C_W
XBC_W = C_W + 2 * G_C * N_C
CHUNK = 64
N_EXPERTS = 16
N_GROUPS = 4
EPG = N_EXPERTS // N_GROUPS
D_EXPERT = 256
EPS = 1e-6
LANES = 128

COL_Q, COL_K, COL_V, COL_GATE = 0, A_W, 2 * A_W, 3 * A_W
COL_SCH = 4 * A_W
COL_SCB = COL_SCH + B_W
COL_SCC = COL_SCB + B_W
COL_Z = COL_SCC + B_W
COL_XBC = COL_Z + C_W
COL_SMALL = COL_XBC + XBC_W
D_IN_PAD = COL_SMALL + LANES
LANE_BETA, LANE_ALPHA, LANE_DT = 0, 2 * H_A, 4 * H_A
SRC_SMALL_A = 4 * A_W
SRC_SCH = SRC_SMALL_A + 4 * H_A
SRC_SMALL_B = SRC_SCH + 3 * B_W + C_W + XBC_W
D_IN = SRC_SMALL_B + 2 * H_C

assert HPG == 2 and P_C == N_C == CHUNK and HPG * P_C == LANES and G_C * N_C == LANES and DK_A == DV_A == LANES

VMEM_LIMIT = 56 * 1024 * 1024
STATIC_PREP_CHUNKS = 4


def _dot(a, b):
    return jnp.dot(a.astype(BF16), b.astype(BF16), preferred_element_type=F32)


def _dot_nt(a, b):
    return lax.dot_general(a.astype(BF16), b.astype(BF16), (((1,), (1,)), ((), ())), preferred_element_type=F32)


def _dot_tn(a, b):
    return lax.dot_general(a.astype(BF16), b.astype(BF16), (((0,), (0,)), ((), ())), preferred_element_type=F32)


def _dot_f32(a, b):
    return jnp.dot(a, b, precision=lax.Precision.HIGHEST, preferred_element_type=F32)


def _dot_nt_f32(a, b):
    return lax.dot_general(a, b, (((1,), (1,)), ((), ())), precision=lax.Precision.HIGHEST,
                           preferred_element_type=F32)


def _silu(x):
    h = 0.5 * x
    return h + h * jnp.tanh(h)


def _sigmoid(x):
    return 1.0 / (1.0 + jnp.exp(-x))


def _softplus(x):
    return jnp.maximum(x, 0.0) + jnp.log1p(jnp.exp(-jnp.abs(x)))


def _rms(x, g):
    return x * lax.rsqrt(jnp.mean(x * x, axis=-1, keepdims=True) + EPS) * g


def _tile(n, pref):
    t = min(n, pref)
    while n % t:
        t -= 8
    assert t > 0 and t % 8 == 0, (n, pref)
    return t


def _params(sem):
    return pltpu.CompilerParams(dimension_semantics=sem, vmem_limit_bytes=VMEM_LIMIT)


def _mod_kernel(cond_ref, w_ref, b_ref, o_ref):
    s = _silu(cond_ref[...])
    o_ref[0] = _dot_f32(s, w_ref[0]) + b_ref[0]


def _modulation(cond_rows, mod_w, mod_b):
    depth, d, n = mod_w.shape
    r = cond_rows.shape[0]
    tn = _tile(n, 1536)
    return pl.pallas_call(
        _mod_kernel,
        grid=(depth, n // tn),
        in_specs=[pl.BlockSpec((r, d), lambda l, j: (0, 0)),
                  pl.BlockSpec((1, d, tn), lambda l, j: (l, 0, j)),
                  pl.BlockSpec((1, 1, tn), lambda l, j: (l, 0, j))],
        out_specs=pl.BlockSpec((1, r, tn), lambda l, j: (l, 0, j)),
        out_shape=jax.ShapeDtypeStruct((depth, r, n), F32),
        compiler_params=_params(("arbitrary", "arbitrary")),
        name="modulation",
    )(cond_rows, mod_w, mod_b.reshape(depth, 1, n))


def _add_kernel(x_ref, p_ref, o_ref):
    o_ref[0] = x_ref[0] + p_ref[...]


def _add_pos(x, pos):
    nb, l, d = x.shape
    tl = _tile(l, 512)
    return pl.pallas_call(
        _add_kernel,
        grid=(nb, l // tl),
        in_specs=[pl.BlockSpec((1, tl, d), lambda b, i: (b, i, 0)),
                  pl.BlockSpec((tl, d), lambda b, i: (i, 0))],
        out_specs=pl.BlockSpec((1, tl, d), lambda b, i: (b, i, 0)),
        out_shape=jax.ShapeDtypeStruct(x.shape, x.dtype),
        compiler_params=_params(("arbitrary", "arbitrary")),
        name="add_pos",
    )(x, pos)


def _grid_pos_embed(n_tok, dim):
    rows = n_tok // GRID_W
    rr, cc = np.meshgrid(np.arange(rows, dtype=np.float64), np.arange(GRID_W, dtype=np.float64), indexing="ij")
    quarter = dim // 4
    omega = 1.0 / (POS_BASE ** (np.arange(quarter, dtype=np.float64) / quarter))
    ang_r = rr.reshape(-1, 1) * omega
    ang_c = cc.reshape(-1, 1) * omega
    return np.concatenate([np.sin(ang_r), np.cos(ang_r), np.sin(ang_c), np.cos(ang_c)], axis=-1)


def _inproj_kernel(x_ref, mod_ref, g_ref, w_ref, o_ref):
    m = mod_ref[0]
    shift, scale = m[0:1, :], m[1:2, :]
    h = _rms(x_ref[...], g_ref[...]) * (1.0 + scale) + shift
    o_ref[...] = _dot(h, w_ref[...])


def _inproj(x, mod6, seq_rows, norm_g, w_in_r, layer):
    t, d = x.shape
    tm = _tile(seq_rows, 256)
    return pl.pallas_call(
        _inproj_kernel,
        grid=(t // tm,),
        in_specs=[pl.BlockSpec((tm, d), lambda i: (i, 0)),
                  pl.BlockSpec((1, 6, d), lambda i: ((i * tm) // seq_rows, 0, 0)),
                  pl.BlockSpec((1, d), lambda i: (0, 0)),
                  pl.BlockSpec((None, d, D_IN_PAD), lambda i: (layer, 0, 0))],
        out_specs=pl.BlockSpec((tm, D_IN_PAD), lambda i: (i, 0)),
        out_shape=jax.ShapeDtypeStruct((t, D_IN_PAD), F32),
        compiler_params=_params(("arbitrary",)),
        name="inproj",
    )(x, mod6, norm_g.reshape(1, d), w_in_r)


CONV_WIN = CHUNK + 16


def _conv_select(r0, w0):
    ti = lax.broadcasted_iota(jnp.int32, (2 * CHUNK, 2 * CONV_WIN), 0)
    tj = lax.broadcasted_iota(jnp.int32, (2 * CHUNK, 2 * CONV_WIN), 1)
    want = r0 + jnp.where(ti < CHUNK, ti - 1, ti - CHUNK + 1)
    have = w0 + jnp.where(tj < CONV_WIN, tj, tj - CONV_WIN)
    return jnp.where(want == have, 1.0, 0.0).astype(BF16)


def _conv3(load, r0, w0, sel, w):
    cur = load(pl.ds(r0, CHUNK))
    win = load(pl.ds(w0, CONV_WIN))
    hi = win.astype(BF16)
    lo = (win - hi.astype(F32)).astype(BF16)
    nb = jnp.dot(sel, jnp.concatenate([hi, lo], axis=0), preferred_element_type=F32)
    return w[0:1, :] * nb[:CHUNK] + w[1:2, :] * cur + w[2:3, :] * nb[CHUNK:]


def _mixer_kernel(*refs, seq_len, zero_init, state_layer, state_layers, state_aliased):
    refs = list(refs)
    proj, dnw, scw, ssw, lanev, ssdv = refs[:6]
    k = 6
    if not zero_init:
        sdn0, sssd0 = refs[k:k + 2]
        k += 2
    emit_state = state_layer is not None
    if state_aliased:
        k += 2
    ycat = refs[k]
    k += 1
    if emit_state:
        sdn_out, sssd_out = refs[k:k + 2]
        k += 2
    (qkv_s, xbc_s, bdup_s, cdup_s, gc_s, gcrow_s, gcpair_s, tot_s, sp_s, beta_s, u_s, wq_s, kd_s, qk_s, o_s, y_s,
     st_s, hs_s) = refs[k:]

    nc = seq_len // CHUNK
    ri = lax.broadcasted_iota(jnp.int32, (CHUNK, CHUNK), 0)
    ci = lax.broadcasted_iota(jnp.int32, (CHUNK, CHUNK), 1)
    tril = (ri >= ci).astype(F32)
    eye = (ri == ci).astype(F32)
    r128 = lax.broadcasted_iota(jnp.int32, (LANES, LANES), 0)
    c128 = lax.broadcasted_iota(jnp.int32, (LANES, LANES), 1)
    eye128 = (r128 == c128).astype(F32)
    incl = (ri >= ci, ri <= ci)
    strict = (ri > ci, ri < ci)
    n_lvl = int(math.log2(CHUNK))
    lvl = [((ri >> s) == (ci >> s)) & ((ri >> (s - 1)) != (ci >> (s - 1))) for s in range(1, n_lvl + 1)]
    alog = lanev[0:1, :]
    bias = lanev[1:2, :]

    def chunk_rows(z):
        return pl.ds(z * CHUNK if isinstance(z, int) else pl.multiple_of(z * CHUNK, CHUNK), CHUNK)

    def prep(z, carry):
        r0 = z * CHUNK if isinstance(z, int) else pl.multiple_of(z * CHUNK, CHUNK)
        rows = pl.ds(r0, CHUNK)
        if isinstance(z, int):
            w0 = min(max(r0 - 8, 0), seq_len - CONV_WIN)
        else:
            w0 = pl.multiple_of(jnp.clip(r0 - 8, 0, seq_len - CONV_WIN), 8)
        sel = _conv_select(r0, w0)
        cw = 2 * LANES
        for jj in range(3 * A_W // cw):
            c0 = jj * cw
            a2 = _silu(_conv3(lambda rs: proj[0, rs, c0:c0 + cw], r0, w0, sel, dnw[:, c0:c0 + cw]))
            for half in range(cw // LANES):
                a = a2[:, half * LANES:(half + 1) * LANES]
                j = jj * (cw // LANES) + half
                if j < 2 * H_A:
                    a = a * lax.rsqrt(jnp.sum(a * a, axis=-1, keepdims=True) + EPS)
                if j < H_A:
                    a = a * (DK_A ** -0.5)
                qkv_s[rows, j * LANES:(j + 1) * LANES] = a
        for jj in range(B_W // cw):
            c0 = jj * cw
            cv = _conv3(lambda rs: proj[0, rs, COL_SCC + c0:COL_SCC + c0 + cw]
                        * proj[0, rs, COL_SCH + c0:COL_SCH + c0 + cw], r0, w0, sel, scw[:, c0:c0 + cw])
            yb = proj[0, rows, COL_SCB + c0:COL_SCB + c0 + cw] * cv
            ycat[0, rows, A_W + c0:A_W + c0 + cw] = yb.astype(ycat.dtype)
        for jj in range(XBC_W // cw):
            c1 = jj * cw
            a2 = _silu(_conv3(lambda rs: proj[0, rs, COL_XBC + c1:COL_XBC + c1 + cw], r0, w0, sel,
                              ssw[:, c1:c1 + cw]))
            if c1 < C_W:
                xbc_s[rows, c1:c1 + cw] = a2
                continue
            for half, dup_s in enumerate((bdup_s, cdup_s)):
                a = a2[:, half * LANES:(half + 1) * LANES]
                swapped = pltpu.roll(a, N_C, 1)
                lo = lax.broadcasted_iota(jnp.int32, a.shape, 1) < N_C
                dup_s[rows, 0:LANES] = jnp.where(lo, a, swapped)
                dup_s[rows, LANES:2 * LANES] = jnp.where(lo, swapped, a)
        sm = proj[0, rows, COL_SMALL:COL_SMALL + LANES]
        sp = _softplus(sm + bias)
        g = -jnp.exp(alog) * sp
        pre = _dot_f32(tril, g)
        tot = pre[CHUNK - 1:CHUNK, :]
        suf = tot - pre + g
        sp_s[rows, :] = sp
        beta_s[rows, :] = _sigmoid(sm)
        gc_s[0, rows, :] = pre
        gc_s[1, rows, :] = suf
        for d, gcd in enumerate((pre, suf)):
            gr = _dot_nt_f32(eye128, gcd)
            gcrow_s[d, z] = gr
            for g in range(G_C):
                ln = LANE_DT + d * H_C + g * HPG
                gcpair_s[d, z, g:g + 1, :] = jnp.concatenate([gr[ln:ln + 1, :], gr[ln + 1:ln + 2, :]], axis=1)
        tot_s[z] = jnp.broadcast_to(tot, (8, LANES))
        o_s[rows, :] = jnp.zeros((CHUNK, A_W), F32)
        y_s[rows, :] = jnp.zeros((CHUNK, C_W), F32)
        return carry

    if nc <= STATIC_PREP_CHUNKS:
        for z in range(nc):
            prep(z, 0)
    else:
        lax.fori_loop(0, nc, prep, 0, unroll=2)

    for d in range(2):
        for h in range(H_A):
            st_s[d * H_A + h] = jnp.zeros((DK_A, DV_A), F32) if zero_init else sdn0[0, d, h]
        for g in range(G_C):
            if zero_init:
                hs_s[d, g] = jnp.zeros((HPG * P_C, HPG * N_C), F32)
            else:
                zero = jnp.zeros((P_C, N_C), F32)
                hs_s[d, g] = jnp.concatenate(
                    [jnp.concatenate([sssd0[0, d, g * HPG], zero], axis=1),
                     jnp.concatenate([zero, sssd0[0, d, g * HPG + 1]], axis=1)], axis=0)

    cpi = 4 if nc % 4 == 0 else 2

    def delta_prep(i, carry):
        units = []
        for zz in range(cpi):
            z = cpi * i + zz
            rows = chunk_rows(z)
            tot = tot_s[z][0:1, :]
            beta = beta_s[rows, :]
            gcs = [gc_s[d, rows, :] for d in range(2)]
            grs = [gcrow_s[d, z] for d in range(2)]
            for h in range(H_A):
                q_h = qkv_s[rows, COL_Q + h * DK_A:COL_Q + (h + 1) * DK_A]
                k_h = qkv_s[rows, COL_K + h * DK_A:COL_K + (h + 1) * DK_A]
                v_h = qkv_s[rows, COL_V + h * DV_A:COL_V + (h + 1) * DV_A]
                units.append(dict(z=z, h=h, q=q_h, k=k_h, v=v_h, tot=tot, beta=beta, gcs=gcs, grs=grs))
        qkk = [_dot_nt(jnp.concatenate([p["q"], p["k"]], axis=0), p["k"]) for p in units]
        dus = []
        for p, qk_kk in zip(units, qkk):
            for d in range(2):
                h = p["h"]
                ln = LANE_ALPHA + d * H_A + h
                a_col = p["gcs"][d][:, ln:ln + 1]
                a_row = p["grs"][d][ln:ln + 1, :]
                t_col = p["tot"][:, ln:ln + 1]
                decay = jnp.exp(jnp.where(incl[d], a_col - a_row, -1e30))
                b_col = p["beta"][:, LANE_BETA + d * H_A + h:LANE_BETA + d * H_A + h + 1]
                eg = jnp.exp(a_col)
                m = jnp.where(strict[d], qk_kk[CHUNK:] * b_col * decay, 0.0)
                rhs = jnp.concatenate([p["v"] * b_col, p["k"] * (b_col * eg)], axis=1)
                idx = (d, p["z"], h)
                qk_s[idx] = (qk_kk[:CHUNK] * decay).astype(qk_s.dtype)
                kd_s[idx] = (p["k"] * jnp.exp(t_col - a_col)).astype(kd_s.dtype)
                wq_s[d, p["z"], h, CHUNK:, :] = (p["q"] * eg).astype(wq_s.dtype)
                dus.append(dict(idx=idx, m=m, rhs=rhs))
        t_inv = [eye - jnp.where(lvl[0], p["m"], 0.0) for p in dus]
        for s in range(1, n_lvl):
            x = [_dot(jnp.where(lvl[s], p["m"], 0.0), t) for p, t in zip(dus, t_inv)]
            t_inv = [t - _dot(t, xx) for t, xx in zip(t_inv, x)]
        uw = [_dot(t, p["rhs"]) for p, t in zip(dus, t_inv)]
        for p, r in zip(dus, uw):
            d, z, h = p["idx"]
            u_s[p["idx"]] = r[:, :DV_A]
            wq_s[d, z, h, :CHUNK, :] = r[:, DV_A:].astype(wq_s.dtype)
        return carry

    if nc <= STATIC_PREP_CHUNKS:
        for i in range(nc // cpi):
            delta_prep(i, 0)
    else:
        lax.fori_loop(0, nc // cpi, delta_prep, 0)

    pw = HPG * P_C
    ri2 = lax.broadcasted_iota(jnp.int32, (CHUNK, pw), 0)
    ci2 = lax.broadcasted_iota(jnp.int32, (CHUNK, pw), 1)
    lane_hi = ci2 >= P_C
    tj = ci2 & (CHUNK - 1)
    incl2 = (ri2 >= tj, ri2 <= tj)
    rb = lax.broadcasted_iota(jnp.int32, (pw, pw), 0) >= P_C
    cbk = lax.broadcasted_iota(jnp.int32, (pw, pw), 1) >= N_C
    diag_blk = rb == cbk

    def scan(z, carry):
        dus = [(d, (z if d == 0 else nc - 1 - z), h) for d in range(2) for h in range(H_A)]
        s_prev = [st_s[d * H_A + h] for d, _, h in dus]
        us = []
        for d in range(2):
            zc = z if d == 0 else nc - 1 - z
            rows = chunk_rows(zc)
            sp = sp_s[rows, :]
            tot = tot_s[zc][0:1, :]
            gc = gc_s[d, rows, :]
            for g in range(G_C):
                ln = LANE_DT + d * H_C + g * HPG
                gsl = slice(g * pw, (g + 1) * pw)
                a_pair = jnp.where(lane_hi, gc[:, ln + 1:ln + 2], gc[:, ln:ln + 1])
                t_pair = jnp.where(lane_hi, tot[:, ln + 1:ln + 2], tot[:, ln:ln + 1])
                lmat = jnp.exp(jnp.where(incl2[d], a_pair - gcpair_s[d, zc, g:g + 1, :], -1e30))
                xdt = xbc_s[rows, gsl] * jnp.where(lane_hi, sp[:, ln + 1:ln + 2], sp[:, ln:ln + 1])
                b_dup = bdup_s[rows, gsl]
                c_dup = cdup_s[rows, gsl]
                us.append(dict(d=d, g=g, rows=rows, gsl=gsl, lmat=lmat, xdt=xdt, c_dup=c_dup,
                               c_lo=jnp.where(lane_hi, 0.0, c_dup), b_st=jnp.concatenate([b_dup, b_dup], axis=0),
                               bdec=b_dup * jnp.exp(t_pair - a_pair), ea=jnp.exp(a_pair),
                               dec=jnp.where(rb, jnp.exp(tot[:, ln + 1:ln + 2]), jnp.exp(tot[:, ln:ln + 1])),
                               x_bd=jnp.concatenate([jnp.where(lane_hi, 0.0, xdt), jnp.where(lane_hi, xdt, 0.0)],
                                                    axis=0),
                               h_prev=hs_s[d, g]))
        ws_qs = [_dot(wq_s[idx], s) for idx, s in zip(dus, s_prev)]
        st = [_dot_tn(p["xdt"], p["bdec"]) for p in us]
        cb = [_dot_nt(p["c_lo"], p["b_st"]) for p in us]
        y_off = [_dot_nt(p["c_dup"], p["h_prev"]) for p in us]
        v_new = [u_s[idx] - r[:CHUNK] for idx, r in zip(dus, ws_qs)]
        o_in = [_dot(qk_s[idx], v) for idx, v in zip(dus, v_new)]
        s_add = [_dot_tn(kd_s[idx], v) for idx, v in zip(dus, v_new)]
        y_diag = [_dot(cbd * p["lmat"], p["x_bd"]) for p, cbd in zip(us, cb)]
        for p, yd, yo, s in zip(us, y_diag, y_off, st):
            y_s[p["rows"], p["gsl"]] = y_s[p["rows"], p["gsl"]] + yd + yo * p["ea"]
            hs_s[p["d"], p["g"]] = p["h_prev"] * p["dec"] + jnp.where(diag_blk, s, 0.0)
        for (d, zc, h), r, oi, sa, s in zip(dus, ws_qs, o_in, s_add, s_prev):
            ln = LANE_ALPHA + d * H_A + h
            cs = slice(h * DV_A, (h + 1) * DV_A)
            o_s[chunk_rows(zc), cs] = o_s[chunk_rows(zc), cs] + r[CHUNK:] + oi
            st_s[d * H_A + h] = s * jnp.exp(tot_s[zc][0:1, ln:ln + 1]) + sa
        return carry

    lax.fori_loop(0, nc, scan, 0, unroll=4)


    def finish(z, carry):
        rows = chunk_rows(z)
        for h in range(H_A):
            cs = slice(h * DV_A, (h + 1) * DV_A)
            o = _rms(o_s[rows, cs], lanev[2:3, :])
            o = o * _silu(proj[0, rows, COL_GATE + h * DV_A:COL_GATE + (h + 1) * DV_A])
            ycat[0, rows, cs] = o.astype(ycat.dtype)
        y = y_s[rows, :] + ssdv[0:1, :] * xbc_s[rows, 0:C_W]
        y = _rms(y * _silu(proj[0, rows, COL_Z:COL_Z + C_W]), ssdv[1:2, :])
        ycat[0, rows, A_W + B_W:] = y.astype(ycat.dtype)
        return carry

    if nc <= STATIC_PREP_CHUNKS:
        for z in range(nc):
            finish(z, 0)
    else:
        lax.fori_loop(0, nc, finish, 0, unroll=2)

    if emit_state:
        slots = (None,) if state_aliased else range(state_layers)
        for slot in slots:
            dn_slot = sdn_out.at[0] if slot is None else sdn_out.at[0, slot]
            ssd_slot = sssd_out.at[0] if slot is None else sssd_out.at[0, slot]
            mine = slot is None or slot == state_layer
            for d in range(2):
                for h in range(H_A):
                    dn_slot[d, h] = st_s[d * H_A + h] if mine else jnp.zeros((DK_A, DV_A), F32)
                for h in range(H_C):
                    k0 = (h % HPG) * P_C
                    ssd_slot[d, h] = (hs_s[d, h // HPG][k0:k0 + P_C, k0:k0 + N_C] if mine
                                      else jnp.zeros((P_C, N_C), F32))


def _mixer(proj, dn_conv_w, sc_conv_w, ssd_conv_w, lanev, ssdv, s_dn0, s_ssd0, state_out):
    nb, seq_len, _ = proj.shape
    nc = seq_len // CHUNK
    assert nc % 2 == 0
    zero_init = s_dn0 is None
    full = lambda a: pl.BlockSpec(a.shape, lambda b: (0,) * a.ndim)
    args = [proj, dn_conv_w, sc_conv_w, ssd_conv_w, lanev, ssdv]
    in_specs = [pl.BlockSpec((1, seq_len, D_IN_PAD), lambda b: (b, 0, 0), pipeline_mode=pl.Buffered(1))
                if seq_len > 512 else pl.BlockSpec((1, seq_len, D_IN_PAD), lambda b: (b, 0, 0)),
                full(dn_conv_w), full(sc_conv_w), full(ssd_conv_w), full(lanev), full(ssdv)]
    if not zero_init:
        args += [s_dn0, s_ssd0]
        in_specs += [pl.BlockSpec((1, 2, H_A, DK_A, DV_A), lambda b: (b, 0, 0, 0, 0)),
                     pl.BlockSpec((1, 2, H_C, P_C, N_C), lambda b: (b, 0, 0, 0, 0))]
    out_shape = [jax.ShapeDtypeStruct((nb, seq_len, D_MODEL), BF16)]
    out_specs = [pl.BlockSpec((1, seq_len, D_MODEL), lambda b: (b, 0, 0))]
    aliases = {}
    layer, depth, prev = state_out if state_out is not None else (None, None, None)
    if state_out is not None:
        out_shape += [jax.ShapeDtypeStruct((nb, depth, 2, H_A, DK_A, DV_A), F32),
                      jax.ShapeDtypeStruct((nb, depth, 2, H_C, P_C, N_C), F32)]
        if prev is None:
            out_specs += [pl.BlockSpec((1, depth, 2, H_A, DK_A, DV_A), lambda b: (b, 0, 0, 0, 0, 0)),
                          pl.BlockSpec((1, depth, 2, H_C, P_C, N_C), lambda b: (b, 0, 0, 0, 0, 0))]
        else:
            aliases = {len(args): 1, len(args) + 1: 2}
            args += list(prev)
            in_specs += [pl.BlockSpec(memory_space=pl.ANY), pl.BlockSpec(memory_space=pl.ANY)]
            out_specs += [pl.BlockSpec((1, None, 2, H_A, DK_A, DV_A), lambda b: (b, layer, 0, 0, 0, 0)),
                          pl.BlockSpec((1, None, 2, H_C, P_C, N_C), lambda b: (b, layer, 0, 0, 0, 0))]
    scratch = [pltpu.VMEM((seq_len, 3 * A_W), F32),
               pltpu.VMEM((seq_len, C_W), F32),
               pltpu.VMEM((seq_len, G_C * LANES), F32),
               pltpu.VMEM((seq_len, G_C * LANES), F32),
               pltpu.VMEM((2, seq_len, LANES), F32),
               pltpu.VMEM((2, nc, LANES, CHUNK), F32),
               pltpu.VMEM((2, nc, 8, LANES), F32),
               pltpu.VMEM((nc, 8, LANES), F32),
               pltpu.VMEM((seq_len, LANES), F32),
               pltpu.VMEM((seq_len, LANES), F32),
               pltpu.VMEM((2, nc, H_A, CHUNK, DV_A), F32),
               pltpu.VMEM((2, nc, H_A, 2 * CHUNK, DK_A), BF16),
               pltpu.VMEM((2, nc, H_A, CHUNK, DK_A), BF16),
               pltpu.VMEM((2, nc, H_A, CHUNK, CHUNK), BF16),
               pltpu.VMEM((seq_len, A_W), F32),
               pltpu.VMEM((seq_len, C_W), F32),
               pltpu.VMEM((2 * H_A, DK_A, DV_A), F32),
               pltpu.VMEM((2, G_C, HPG * P_C, HPG * N_C), F32)]
    return pl.pallas_call(
        functools.partial(_mixer_kernel, seq_len=seq_len, zero_init=zero_init, state_layer=layer,
                          state_layers=depth, state_aliased=prev is not None),
        grid=(nb,),
        in_specs=in_specs,
        out_specs=out_specs,
        out_shape=out_shape,
        input_output_aliases=aliases,
        scratch_shapes=scratch,
        compiler_params=_params(("arbitrary",)),
        name="mixer",
    )(*args)


def _top2_sum(a, b, c, d):
    hi1, lo1 = jnp.maximum(a, b), jnp.minimum(a, b)
    hi2, lo2 = jnp.maximum(c, d), jnp.minimum(c, d)
    return jnp.maximum(hi1, hi2) + jnp.maximum(jnp.minimum(hi1, hi2), jnp.maximum(lo1, lo2))


def _outproj_kernel(y_ref, x_ref, mod_ref, w_ref, g_ref, rw_ref, rb_ref, x1_ref, h2_ref, comb_ref, wb_ref):
    @pl.when(pl.program_id(0) == 0)
    def _():
        wb_ref[...] = w_ref[...].astype(wb_ref.dtype)

    m = mod_ref[0]
    gate1, shift2, scale2 = m[2:3, :], m[3:4, :], m[4:5, :]
    x1 = x_ref[...] + gate1 * _dot(y_ref[...], wb_ref[...])
    x1_ref[...] = x1
    h2 = _rms(x1, g_ref[...]) * (1.0 + scale2) + shift2
    h2_ref[...] = h2.astype(h2_ref.dtype)

    h_hi = h2.astype(BF16)
    h_lo = (h2 - h_hi.astype(F32)).astype(BF16)
    rw_hi, rw_lo = rw_ref[0], rw_ref[1]
    hl = jnp.dot(jnp.concatenate([h_hi, h_lo], axis=0), rw_hi, preferred_element_type=F32)
    tm = h2.shape[0]
    logits = hl[:tm] + hl[tm:] + jnp.dot(h_hi, rw_lo, preferred_element_type=F32)
    scores = _sigmoid(logits.T[:N_EXPERTS, :])
    biased = scores + rb_ref[...]
    sc = [scores[e:e + 1, :] for e in range(N_EXPERTS)]
    bi = [biased[e:e + 1, :] for e in range(N_EXPERTS)]
    gs = [_top2_sum(*bi[EPG * g:EPG * (g + 1)]) for g in range(N_GROUPS)]
    gmax = functools.reduce(jnp.maximum, gs)
    first = []
    taken = None
    for g in range(N_GROUPS):
        hit = gs[g] == gmax
        if taken is None:
            first.append(hit)
            taken = hit
        else:
            first.append(hit & jnp.logical_not(taken))
            taken = taken | hit

    def pick(vals, j):
        out = vals[EPG * (N_GROUPS - 1) + j]
        for g in range(N_GROUPS - 2, -1, -1):
            out = jnp.where(first[g], vals[EPG * g + j], out)
        return out

    ib = [pick(bi, j) for j in range(EPG)]
    isc = [pick(sc, j) for j in range(EPG)]
    sel = []
    for j in range(EPG):
        cnt = jnp.zeros_like(ib[j])
        for i in range(EPG):
            if i == j:
                continue
            ahead = (ib[i] > ib[j]) | ((ib[i] == ib[j]) if i < j else False)
            cnt = cnt + jnp.where(ahead, 1.0, 0.0)
        sel.append(cnt < 2.0)
    wj = [jnp.where(sel[j], isc[j], 0.0) for j in range(EPG)]
    denom = functools.reduce(lambda a, b: a + b, wj)
    for g in range(N_GROUPS):
        for j in range(EPG):
            comb_ref[EPG * g + j:EPG * g + j + 1, :] = jnp.where(first[g], wj[j] / denom, 0.0)


def _outproj(ycat, x, mod6, seq_rows, w_out, layer, norm_g, router_w_pad, router_b):
    t, d = x.shape
    tm = _tile(seq_rows, 256)
    return pl.pallas_call(
        _outproj_kernel,
        grid=(t // tm,),
        in_specs=[pl.BlockSpec((tm, d), lambda i: (i, 0)),
                  pl.BlockSpec((tm, d), lambda i: (i, 0)),
                  pl.BlockSpec((1, 6, d), lambda i: ((i * tm) // seq_rows, 0, 0)),
                  pl.BlockSpec((None, d, d), lambda i: (layer, 0, 0)),
                  pl.BlockSpec((1, d), lambda i: (0, 0)),
                  pl.BlockSpec((2, d, LANES), lambda i: (0, 0, 0)),
                  pl.BlockSpec((N_EXPERTS, 1), lambda i: (0, 0))],
        out_specs=[pl.BlockSpec((tm, d), lambda i: (i, 0)),
                   pl.BlockSpec((tm, d), lambda i: (i, 0)),
                   pl.BlockSpec((N_EXPERTS, tm), lambda i: (0, i))],
        out_shape=[jax.ShapeDtypeStruct((t, d), F32),
                   jax.ShapeDtypeStruct((t, d), BF16),
                   jax.ShapeDtypeStruct((N_EXPERTS, t), F32)],
        scratch_shapes=[pltpu.VMEM((d, d), BF16)],
        compiler_params=_params(("arbitrary",)),
        name="outproj_route",
    )(ycat, x, mod6, w_out, norm_g.reshape(1, d), router_w_pad, router_b.reshape(N_EXPERTS, 1))


def _moe_kernel(h_ref, comb_ref, wg_ref, wu_ref, wd_ref, x1_ref, mod_ref, fg_ref, o_ref, *, final):
    g = pl.program_id(1)
    h = h_ref[...]
    comb = comb_ref[...]
    acts = [(_silu(_dot(h, wg_ref[j])) * _dot(h, wu_ref[j]) * comb[:, j:j + 1]).astype(BF16) for j in range(EPG)]
    y = _dot(jnp.concatenate(acts, axis=1), wd_ref[...])

    @pl.when(g == 0)
    def _():
        o_ref[...] = y

    @pl.when(g > 0)
    def _():
        o_ref[...] += y

    @pl.when(g == pl.num_programs(1) - 1)
    def _():
        x2 = x1_ref[...] + mod_ref[0][5:6, :] * o_ref[...]
        o_ref[...] = _rms(x2, fg_ref[...]) if final else x2


def _moe(h2, comb, w_gate, w_up, w_down, layer, x1, mod6, seq_rows, final_g, final):
    t, d = x1.shape
    tm = _tile(seq_rows, 1024)
    return pl.pallas_call(
        functools.partial(_moe_kernel, final=final),
        grid=(t // tm, N_GROUPS),
        in_specs=[pl.BlockSpec((tm, d), lambda i, g: (i, 0)),
                  pl.BlockSpec((None, tm, EPG), lambda i, g: (g, i, 0)),
                  pl.BlockSpec((None, EPG, d, D_EXPERT), lambda i, g: (layer, g, 0, 0)),
                  pl.BlockSpec((None, EPG, d, D_EXPERT), lambda i, g: (layer, g, 0, 0)),
                  pl.BlockSpec((None, None, EPG * D_EXPERT, d), lambda i, g: (layer, g, 0, 0)),
                  pl.BlockSpec((tm, d), lambda i, g: (i, 0)),
                  pl.BlockSpec((1, 6, d), lambda i, g: ((i * tm) // seq_rows, 0, 0)),
                  pl.BlockSpec((1, d), lambda i, g: (0, 0))],
        out_specs=pl.BlockSpec((tm, d), lambda i, g: (i, 0)),
        out_shape=jax.ShapeDtypeStruct((t, d), F32),
        compiler_params=_params(("arbitrary", "arbitrary")),
        name="experts",
    )(h2, comb, w_gate, w_up, w_down, x1, mod6, final_g.reshape(1, d))


def _lane_row(*pieces):
    row = jnp.zeros((LANES,), F32)
    for lane, vals in pieces:
        row = lax.dynamic_update_slice(row, vals.reshape(-1).astype(F32), (lane,))
    return row


def kernel(x_prompt, x_sample, state_delta, state_ssd, c, c_ctx, mod_w, mod_b, norm1_g, norm2_g, w_in, w_out,
           dn_conv_w, dn_a_log, dn_dt_bias, dn_norm_g, sc_conv_w, ssd_conv_w, ssd_a_log, ssd_dt_bias, ssd_d,
           ssd_norm_g, router_w, router_b, exp_w_gate, exp_w_up, exp_w_down, final_norm_g):
    depth = mod_w.shape[0]
    n_ctx, seq, d = x_prompt.shape
    n_dec, dec_seq, _ = x_sample.shape
    assert seq % CHUNK == 0 and dec_seq % CHUNK == 0 and d == D_MODEL

    n_rows = -(-(1 + n_dec) // 8) * 8
    cond_rows = jnp.zeros((n_rows, d), F32).at[0].set(c_ctx).at[1:1 + n_dec].set(c)
    mod = _modulation(cond_rows, mod_w, mod_b).reshape(depth, n_rows, 6, d)

    w_in_r = jnp.concatenate(
        [w_in[:, :, :SRC_SMALL_A], w_in[:, :, SRC_SCH:SRC_SMALL_B], w_in[:, :, SRC_SMALL_A:SRC_SCH],
         w_in[:, :, SRC_SMALL_B:], jnp.zeros((depth, d, D_IN_PAD - D_IN), w_in.dtype)], axis=-1).astype(BF16)
    w_down_g = exp_w_down.reshape(depth, N_GROUPS, EPG * D_EXPERT, d)
    rw = jnp.pad(router_w.astype(F32), ((0, 0), (0, LANES - N_EXPERTS)))
    rw_hi = rw.astype(BF16)
    router_w_pad = jnp.stack([rw_hi, (rw - rw_hi.astype(F32)).astype(BF16)])

    xp = x_prompt.reshape(n_ctx * seq, d)
    pos = jnp.asarray(_grid_pos_embed(dec_seq, d), dtype=x_sample.dtype)
    xs = _add_pos(x_sample, pos).reshape(n_dec * dec_seq, d)

    states = None
    for l in range(depth):
        lanev = jnp.zeros((8, LANES), F32)
        lanev = lanev.at[0].set(_lane_row((LANE_ALPHA, dn_a_log[l]), (LANE_DT, ssd_a_log[l])))
        lanev = lanev.at[1].set(_lane_row((LANE_ALPHA, dn_dt_bias[l]), (LANE_DT, ssd_dt_bias[l])))
        lanev = lanev.at[2].set(dn_norm_g[l].astype(F32))
        ssdv = jnp.zeros((8, C_W), F32).at[0].set(jnp.repeat(ssd_d[l].astype(F32), P_C)).at[1].set(ssd_norm_g[l])
        final = l == depth - 1

        def block(x, mod6, nb, seq_len, seq_rows, s_dn0, s_ssd0, state_out):
            proj = _inproj(x, mod6, seq_rows, norm1_g[l], w_in_r, l).reshape(nb, seq_len, D_IN_PAD)
            outs = _mixer(proj, dn_conv_w[l], sc_conv_w[l], ssd_conv_w[l], lanev, ssdv, s_dn0, s_ssd0, state_out)
            ycat = outs[0].reshape(nb * seq_len, d)
            x1, h2, comb_t = _outproj(ycat, x, mod6, seq_rows, w_out, l, norm2_g[l], router_w_pad, router_b)
            comb = comb_t.reshape(N_GROUPS, EPG, -1).transpose(0, 2, 1)
            x2 = _moe(h2, comb, exp_w_gate, exp_w_up, w_down_g, l, x1, mod6, seq_rows, final_norm_g, final)
            return x2, outs[1:]

        xp, states = block(xp, mod[l, 0:1], n_ctx, seq, n_ctx * seq, None, None, (l, depth, states))
        xs, _ = block(xs, mod[l, 1:1 + n_dec], n_dec, dec_seq, dec_seq,
                      state_delta[:, l].astype(F32), state_ssd[:, l].astype(F32), None)

    return (xp.reshape(n_ctx, seq, d), xs.reshape(n_dec, dec_seq, d), states[0], states[1])
```

```python
import functools
import math

import jax
import jax.numpy as jnp
import numpy as np
from jax import lax
from jax.experimental import pallas as pl
from jax.experimental.pallas import tpu as pltpu

F32 = jnp.float32
BF16 = jnp.bfloat16

D_MODEL = 1024
GRID_W = 64
POS_BASE = 10000.0
H_A, DK_A, DV_A = 4, 128, 128
A_W = H_A * DV_A
H_C, P_C, N_C, G_C = 4, 64, 64, 2
HPG = H_C // G_C
C_W = H_C * P_C
B_W = D_MODEL - A_W - C_W
XBC_W = C_W + 2 * G_C * N_C
CHUNK = 64
N_EXPERTS = 16
N_GROUPS = 4
EPG = N_EXPERTS // N_GROUPS
D_EXPERT = 256
EPS = 1e-6
LANES = 128

COL_Q, COL_K, COL_V, COL_GATE = 0, A_W, 2 * A_W, 3 * A_W
COL_SCH = 4 * A_W
COL_SCB = COL_SCH + B_W
COL_SCC = COL_SCB + B_W
COL_Z = COL_SCC + B_W
COL_XBC = COL_Z + C_W
COL_SMALL = COL_XBC + XBC_W
D_IN_PAD = COL_SMALL + LANES
LANE_BETA, LANE_ALPHA, LANE_DT = 0, 2 * H_A, 4 * H_A
SRC_SMALL_A = 4 * A_W
SRC_SCH = SRC_SMALL_A + 4 * H_A
SRC_SMALL_B = SRC_SCH + 3 * B_W + C_W + XBC_W
D_IN = SRC_SMALL_B + 2 * H_C

assert HPG == 2 and P_C == N_C == CHUNK and HPG * P_C == LANES and G_C * N_C == LANES and DK_A == DV_A == LANES

VMEM_LIMIT = 56 * 1024 * 1024
STATIC_PREP_CHUNKS = 4


def _dot(a, b):
    return jnp.dot(a.astype(BF16), b.astype(BF16), preferred_element_type=F32)


def _dot_nt(a, b):
    return lax.dot_general(a.astype(BF16), b.astype(BF16), (((1,), (1,)), ((), ())), preferred_element_type=F32)


def _dot_tn(a, b):
    return lax.dot_general(a.astype(BF16), b.astype(BF16), (((0,), (0,)), ((), ())), preferred_element_type=F32)


def _dot_f32(a, b):
    return jnp.dot(a, b, precision=lax.Precision.HIGHEST, preferred_element_type=F32)


def _dot_nt_f32(a, b):
    return lax.dot_general(a, b, (((1,), (1,)), ((), ())), precision=lax.Precision.HIGHEST,
                           preferred_element_type=F32)


def _silu(x):
    h = 0.5 * x
    return h + h * jnp.tanh(h)


def _sigmoid(x):
    return 1.0 / (1.0 + jnp.exp(-x))


def _softplus(x):
    return jnp.maximum(x, 0.0) + jnp.log1p(jnp.exp(-jnp.abs(x)))


def _rms(x, g):
    return x * lax.rsqrt(jnp.mean(x * x, axis=-1, keepdims=True) + EPS) * g


def _tile(n, pref):
    t = min(n, pref)
    while n % t:
        t -= 8
    assert t > 0 and t % 8 == 0, (n, pref)
    return t


def _params(sem):
    return pltpu.CompilerParams(dimension_semantics=sem, vmem_limit_bytes=VMEM_LIMIT)


def _mod_kernel(cond_ref, w_ref, b_ref, o_ref):
    s = _silu(cond_ref[...])
    o_ref[0] = _dot_f32(s, w_ref[0]) + b_ref[0]


def _modulation(cond_rows, mod_w, mod_b):
    depth, d, n = mod_w.shape
    r = cond_rows.shape[0]
    tn = _tile(n, 1536)
    return pl.pallas_call(
        _mod_kernel,
        grid=(depth, n // tn),
        in_specs=[pl.BlockSpec((r, d), lambda l, j: (0, 0)),
                  pl.BlockSpec((1, d, tn), lambda l, j: (l, 0, j)),
                  pl.BlockSpec((1, 1, tn), lambda l, j: (l, 0, j))],
        out_specs=pl.BlockSpec((1, r, tn), lambda l, j: (l, 0, j)),
        out_shape=jax.ShapeDtypeStruct((depth, r, n), F32),
        compiler_params=_params(("arbitrary", "arbitrary")),
        name="modulation",
    )(cond_rows, mod_w, mod_b.reshape(depth, 1, n))


def _add_kernel(x_ref, p_ref, o_ref):
    o_ref[0] = x_ref[0] + p_ref[...]


def _add_pos(x, pos):
    nb, l, d = x.shape
    tl = _tile(l, 512)
    return pl.pallas_call(
        _add_kernel,
        grid=(nb, l // tl),
        in_specs=[pl.BlockSpec((1, tl, d), lambda b, i: (b, i, 0)),
                  pl.BlockSpec((tl, d), lambda b, i: (i, 0))],
        out_specs=pl.BlockSpec((1, tl, d), lambda b, i: (b, i, 0)),
        out_shape=jax.ShapeDtypeStruct(x.shape, x.dtype),
        compiler_params=_params(("arbitrary", "arbitrary")),
        name="add_pos",
    )(x, pos)


def _grid_pos_embed(n_tok, dim):
    rows = n_tok // GRID_W
    rr, cc = np.meshgrid(np.arange(rows, dtype=np.float64), np.arange(GRID_W, dtype=np.float64), indexing="ij")
    quarter = dim // 4
    omega = 1.0 / (POS_BASE ** (np.arange(quarter, dtype=np.float64) / quarter))
    ang_r = rr.reshape(-1, 1) * omega
    ang_c = cc.reshape(-1, 1) * omega
    return np.concatenate([np.sin(ang_r), np.cos(ang_r), np.sin(ang_c), np.cos(ang_c)], axis=-1)


def _inproj_kernel(x_ref, mod_ref, g_ref, w_ref, o_ref):
    m = mod_ref[0]
    shift, scale = m[0:1, :], m[1:2, :]
    h = _rms(x_ref[...], g_ref[...]) * (1.0 + scale) + shift
    o_ref[...] = _dot(h, w_ref[...])


def _inproj(x, mod6, seq_rows, norm_g, w_in_r, layer):
    t, d = x.shape
    tm = _tile(seq_rows, 512)
    return pl.pallas_call(
        _inproj_kernel,
        grid=(t // tm,),
        in_specs=[pl.BlockSpec((tm, d), lambda i: (i, 0)),
                  pl.BlockSpec((1, 6, d), lambda i: ((i * tm) // seq_rows, 0, 0)),
                  pl.BlockSpec((1, d), lambda i: (0, 0)),
                  pl.BlockSpec((None, d, D_IN_PAD), lambda i: (layer, 0, 0))],
        out_specs=pl.BlockSpec((tm, D_IN_PAD), lambda i: (i, 0)),
        out_shape=jax.ShapeDtypeStruct((t, D_IN_PAD), F32),
        compiler_params=_params(("arbitrary",)),
        name="inproj",
    )(x, mod6, norm_g.reshape(1, d), w_in_r)


CONV_WIN = CHUNK + 16


def _conv_select(r0, w0):
    ti = lax.broadcasted_iota(jnp.int32, (2 * CHUNK, 2 * CONV_WIN), 0)
    tj = lax.broadcasted_iota(jnp.int32, (2 * CHUNK, 2 * CONV_WIN), 1)
    want = r0 + jnp.where(ti < CHUNK, ti - 1, ti - CHUNK + 1)
    have = w0 + jnp.where(tj < CONV_WIN, tj, tj - CONV_WIN)
    return jnp.where(want == have, 1.0, 0.0).astype(BF16)


def _conv3(load, r0, w0, sel, w):
    cur = load(pl.ds(r0, CHUNK))
    win = load(pl.ds(w0, CONV_WIN))
    hi = win.astype(BF16)
    lo = (win - hi.astype(F32)).astype(BF16)
    nb = jnp.dot(sel, jnp.concatenate([hi, lo], axis=0), preferred_element_type=F32)
    return w[0:1, :] * nb[:CHUNK] + w[1:2, :] * cur + w[2:3, :] * nb[CHUNK:]


def _mixer_kernel(*refs, seq_len, zero_init, state_layer, state_layers, state_aliased):
    refs = list(refs)
    proj, dnw, scw, ssw, lanev, ssdv = refs[:6]
    k = 6
    if not zero_init:
        sdn0, sssd0 = refs[k:k + 2]
        k += 2
    emit_state = state_layer is not None
    if state_aliased:
        k += 2
    ycat = refs[k]
    k += 1
    if emit_state:
        sdn_out, sssd_out = refs[k:k + 2]
        k += 2
    (qkv_s, xbc_s, bdup_s, cdup_s, gc_s, gcrow_s, gcpair_s, tot_s, sp_s, beta_s, u_s, wq_s, kd_s, qk_s, o_s, y_s,
     st_s, hs_s) = refs[k:]

    nc = seq_len // CHUNK
    ri = lax.broadcasted_iota(jnp.int32, (CHUNK, CHUNK), 0)
    ci = lax.broadcasted_iota(jnp.int32, (CHUNK, CHUNK), 1)
    tril = (ri >= ci).astype(F32)
    eye = (ri == ci).astype(F32)
    r128 = lax.broadcasted_iota(jnp.int32, (LANES, LANES), 0)
    c128 = lax.broadcasted_iota(jnp.int32, (LANES, LANES), 1)
    eye128 = (r128 == c128).astype(F32)
    incl = (ri >= ci, ri <= ci)
    strict = (ri > ci, ri < ci)
    n_lvl = int(math.log2(CHUNK))
    lvl = [((ri >> s) == (ci >> s)) & ((ri >> (s - 1)) != (ci >> (s - 1))) for s in range(1, n_lvl + 1)]
    alog = lanev[0:1, :]
    bias = lanev[1:2, :]

    def chunk_rows(z):
        return pl.ds(z * CHUNK if isinstance(z, int) else pl.multiple_of(z * CHUNK, CHUNK), CHUNK)

    def prep(z, carry):
        r0 = z * CHUNK if isinstance(z, int) else pl.multiple_of(z * CHUNK, CHUNK)
        rows = pl.ds(r0, CHUNK)
        if isinstance(z, int):
            w0 = min(max(r0 - 8, 0), seq_len - CONV_WIN)
        else:
            w0 = pl.multiple_of(jnp.clip(r0 - 8, 0, seq_len - CONV_WIN), 8)
        sel = _conv_select(r0, w0)
        cw = 2 * LANES
        for jj in range(3 * A_W // cw):
            c0 = jj * cw
            a2 = _silu(_conv3(lambda rs: proj[0, rs, c0:c0 + cw], r0, w0, sel, dnw[:, c0:c0 + cw]))
            for half in range(cw // LANES):
                a = a2[:, half * LANES:(half + 1) * LANES]
                j = jj * (cw // LANES) + half
                if j < 2 * H_A:
                    a = a * lax.rsqrt(jnp.sum(a * a, axis=-1, keepdims=True) + EPS)
                if j < H_A:
                    a = a * (DK_A ** -0.5)
                qkv_s[rows, j * LANES:(j + 1) * LANES] = a
        for jj in range(B_W // cw):
            c0 = jj * cw
            cv = _conv3(lambda rs: proj[0, rs, COL_SCC + c0:COL_SCC + c0 + cw]
                        * proj[0, rs, COL_SCH + c0:COL_SCH + c0 + cw], r0, w0, sel, scw[:, c0:c0 + cw])
            yb = proj[0, rows, COL_SCB + c0:COL_SCB + c0 + cw] * cv
            ycat[0, rows, A_W + c0:A_W + c0 + cw] = yb.astype(ycat.dtype)
        for jj in range(XBC_W // cw):
            c1 = jj * cw
            a2 = _silu(_conv3(lambda rs: proj[0, rs, COL_XBC + c1:COL_XBC + c1 + cw], r0, w0, sel,
                              ssw[:, c1:c1 + cw]))
            if c1 < C_W:
                xbc_s[rows, c1:c1 + cw] = a2
                continue
            for half, dup_s in enumerate((bdup_s, cdup_s)):
                a = a2[:, half * LANES:(half + 1) * LANES]
                swapped = pltpu.roll(a, N_C, 1)
                lo = lax.broadcasted_iota(jnp.int32, a.shape, 1) < N_C
                dup_s[rows, 0:LANES] = jnp.where(lo, a, swapped)
                dup_s[rows, LANES:2 * LANES] = jnp.where(lo, swapped, a)
        sm = proj[0, rows, COL_SMALL:COL_SMALL + LANES]
        sp = _softplus(sm + bias)
        g = -jnp.exp(alog) * sp
        pre = _dot_f32(tril, g)
        tot = pre[CHUNK - 1:CHUNK, :]
        suf = tot - pre + g
        sp_s[rows, :] = sp
        beta_s[rows, :] = _sigmoid(sm)
        gc_s[0, rows, :] = pre
        gc_s[1, rows, :] = suf
        for d, gcd in enumerate((pre, suf)):
            gr = _dot_nt_f32(eye128, gcd)
            gcrow_s[d, z] = gr
            for g in range(G_C):
                ln = LANE_DT + d * H_C + g * HPG
                gcpair_s[d, z, g:g + 1, :] = jnp.concatenate([gr[ln:ln + 1, :], gr[ln + 1:ln + 2, :]], axis=1)
        tot_s[z] = jnp.broadcast_to(tot, (8, LANES))
        o_s[rows, :] = jnp.zeros((CHUNK, A_W), F32)
        y_s[rows, :] = jnp.zeros((CHUNK, C_W), F32)
        return carry

    if nc <= STATIC_PREP_CHUNKS:
        for z in range(nc):
            prep(z, 0)
    else:
        lax.fori_loop(0, nc, prep, 0, unroll=2)

    for d in range(2):
        for h in range(H_A):
            st_s[d * H_A + h] = jnp.zeros((DK_A, DV_A), F32) if zero_init else sdn0[0, d, h]
        for g in range(G_C):
            if zero_init:
                hs_s[d, g] = jnp.zeros((HPG * P_C, HPG * N_C), F32)
            else:
                zero = jnp.zeros((P_C, N_C), F32)
                hs_s[d, g] = jnp.concatenate(
                    [jnp.concatenate([sssd0[0, d, g * HPG], zero], axis=1),
                     jnp.concatenate([zero, sssd0[0, d, g * HPG + 1]], axis=1)], axis=0)

    cpi = 4 if nc % 4 == 0 else 2

    def delta_prep(i, carry):
        units = []
        for zz in range(cpi):
            z = cpi * i + zz
            rows = chunk_rows(z)
            tot = tot_s[z][0:1, :]
            beta = beta_s[rows, :]
            gcs = [gc_s[d, rows, :] for d in range(2)]
            grs = [gcrow_s[d, z] for d in range(2)]
            for h in range(H_A):
                q_h = qkv_s[rows, COL_Q + h * DK_A:COL_Q + (h + 1) * DK_A]
                k_h = qkv_s[rows, COL_K + h * DK_A:COL_K + (h + 1) * DK_A]
                v_h = qkv_s[rows, COL_V + h * DV_A:COL_V + (h + 1) * DV_A]
                units.append(dict(z=z, h=h, q=q_h, k=k_h, v=v_h, tot=tot, beta=beta, gcs=gcs, grs=grs))
        qkk = [_dot_nt(jnp.concatenate([p["q"], p["k"]], axis=0), p["k"]) for p in units]
        dus = []
        for p, qk_kk in zip(units, qkk):
            for d in range(2):
                h = p["h"]
                ln = LANE_ALPHA + d * H_A + h
                a_col = p["gcs"][d][:, ln:ln + 1]
                a_row = p["grs"][d][ln:ln + 1, :]
                t_col = p["tot"][:, ln:ln + 1]
                decay = jnp.exp(jnp.where(incl[d], a_col - a_row, -1e30))
                b_col = p["beta"][:, LANE_BETA + d * H_A + h:LANE_BETA + d * H_A + h + 1]
                eg = jnp.exp(a_col)
                m = jnp.where(strict[d], qk_kk[CHUNK:] * b_col * decay, 0.0)
                rhs = jnp.concatenate([p["v"] * b_col, p["k"] * (b_col * eg)], axis=1)
                idx = (d, p["z"], h)
                qk_s[idx] = (qk_kk[:CHUNK] * decay).astype(qk_s.dtype)
                kd_s[idx] = (p["k"] * jnp.exp(t_col - a_col)).astype(kd_s.dtype)
                wq_s[d, p["z"], h, CHUNK:, :] = (p["q"] * eg).astype(wq_s.dtype)
                dus.append(dict(idx=idx, m=m, rhs=rhs))
        t_inv = [eye - jnp.where(lvl[0], p["m"], 0.0) for p in dus]
        for s in range(1, n_lvl):
            x = [_dot(jnp.where(lvl[s], p["m"], 0.0), t) for p, t in zip(dus, t_inv)]
            t_inv = [t - _dot(t, xx) for t, xx in zip(t_inv, x)]
        uw = [_dot(t, p["rhs"]) for p, t in zip(dus, t_inv)]
        for p, r in zip(dus, uw):
            d, z, h = p["idx"]
            u_s[p["idx"]] = r[:, :DV_A]
            wq_s[d, z, h, :CHUNK, :] = r[:, DV_A:].astype(wq_s.dtype)
        return carry

    if nc <= STATIC_PREP_CHUNKS:
        for i in range(nc // cpi):
            delta_prep(i, 0)
    else:
        lax.fori_loop(0, nc // cpi, delta_prep, 0)

    pw = HPG * P_C
    ri2 = lax.broadcasted_iota(jnp.int32, (CHUNK, pw), 0)
    ci2 = lax.broadcasted_iota(jnp.int32, (CHUNK, pw), 1)
    lane_hi = ci2 >= P_C
    tj = ci2 & (CHUNK - 1)
    incl2 = (ri2 >= tj, ri2 <= tj)
    rb = lax.broadcasted_iota(jnp.int32, (pw, pw), 0) >= P_C
    cbk = lax.broadcasted_iota(jnp.int32, (pw, pw), 1) >= N_C
    diag_blk = rb == cbk

    def scan(z, carry):
        dus = [(d, (z if d == 0 else nc - 1 - z), h) for d in range(2) for h in range(H_A)]
        s_prev = [st_s[d * H_A + h] for d, _, h in dus]
        us = []
        for d in range(2):
            zc = z if d == 0 else nc - 1 - z
            rows = chunk_rows(zc)
            sp = sp_s[rows, :]
            tot = tot_s[zc][0:1, :]
            gc = gc_s[d, rows, :]
            for g in range(G_C):
                ln = LANE_DT + d * H_C + g * HPG
                gsl = slice(g * pw, (g + 1) * pw)
                a_pair = jnp.where(lane_hi, gc[:, ln + 1:ln + 2], gc[:, ln:ln + 1])
                t_pair = jnp.where(lane_hi, tot[:, ln + 1:ln + 2], tot[:, ln:ln + 1])
                lmat = jnp.exp(jnp.where(incl2[d], a_pair - gcpair_s[d, zc, g:g + 1, :], -1e30))
                xdt = xbc_s[rows, gsl] * jnp.where(lane_hi, sp[:, ln + 1:ln + 2], sp[:, ln:ln + 1])
                b_dup = bdup_s[rows, gsl]
                c_dup = cdup_s[rows, gsl]
                us.append(dict(d=d, g=g, rows=rows, gsl=gsl, lmat=lmat, xdt=xdt, c_dup=c_dup,
                               c_lo=jnp.where(lane_hi, 0.0, c_dup), b_st=jnp.concatenate([b_dup, b_dup], axis=0),
                               bdec=b_dup * jnp.exp(t_pair - a_pair), ea=jnp.exp(a_pair),
                               dec=jnp.where(rb, jnp.exp(tot[:, ln + 1:ln + 2]), jnp.exp(tot[:, ln:ln + 1])),
                               x_bd=jnp.concatenate([jnp.where(lane_hi, 0.0, xdt), jnp.where(lane_hi, xdt, 0.0)],
                                                    axis=0),
                               h_prev=hs_s[d, g]))
        ws_qs = [_dot(wq_s[idx], s) for idx, s in zip(dus, s_prev)]
        st = [_dot_tn(p["xdt"], p["bdec"]) for p in us]
        cb = [_dot_nt(p["c_lo"], p["b_st"]) for p in us]
        y_off = [_dot_nt(p["c_dup"], p["h_prev"]) for p in us]
        v_new = [u_s[idx] - r[:CHUNK] for idx, r in zip(dus, ws_qs)]
        o_in = [_dot(qk_s[idx], v) for idx, v in zip(dus, v_new)]
        s_add = [_dot_tn(kd_s[idx], v) for idx, v in zip(dus, v_new)]
        y_diag = [_dot(cbd * p["lmat"], p["x_bd"]) for p, cbd in zip(us, cb)]
        for p, yd, yo, s in zip(us, y_diag, y_off, st):
            y_s[p["rows"], p["gsl"]] = y_s[p["rows"], p["gsl"]] + yd + yo * p["ea"]
            hs_s[p["d"], p["g"]] = p["h_prev"] * p["dec"] + jnp.where(diag_blk, s, 0.0)
        for (d, zc, h), r, oi, sa, s in zip(dus, ws_qs, o_in, s_add, s_prev):
            ln = LANE_ALPHA + d * H_A + h
            cs = slice(h * DV_A, (h + 1) * DV_A)
            o_s[chunk_rows(zc), cs] = o_s[chunk_rows(zc), cs] + r[CHUNK:] + oi
            st_s[d * H_A + h] = s * jnp.exp(tot_s[zc][0:1, ln:ln + 1]) + sa
        return carry

    lax.fori_loop(0, nc, scan, 0, unroll=4)


    def finish(z, carry):
        rows = chunk_rows(z)
        for h in range(H_A):
            cs = slice(h * DV_A, (h + 1) * DV_A)
            o = _rms(o_s[rows, cs], lanev[2:3, :])
            o = o * _silu(proj[0, rows, COL_GATE + h * DV_A:COL_GATE + (h + 1) * DV_A])
            ycat[0, rows, cs] = o.astype(ycat.dtype)
        y = y_s[rows, :] + ssdv[0:1, :] * xbc_s[rows, 0:C_W]
        y = _rms(y * _silu(proj[0, rows, COL_Z:COL_Z + C_W]), ssdv[1:2, :])
        ycat[0, rows, A_W + B_W:] = y.astype(ycat.dtype)
        return carry

    if nc <= STATIC_PREP_CHUNKS:
        for z in range(nc):
            finish(z, 0)
    else:
        lax.fori_loop(0, nc, finish, 0, unroll=2)

    if emit_state:
        slots = (None,) if state_aliased else range(state_layers)
        for slot in slots:
            dn_slot = sdn_out.at[0] if slot is None else sdn_out.at[0, slot]
            ssd_slot = sssd_out.at[0] if slot is None else sssd_out.at[0, slot]
            mine = slot is None or slot == state_layer
            for d in range(2):
                for h in range(H_A):
                    dn_slot[d, h] = st_s[d * H_A + h] if mine else jnp.zeros((DK_A, DV_A), F32)
                for h in range(H_C):
                    k0 = (h % HPG) * P_C
                    ssd_slot[d, h] = (hs_s[d, h // HPG][k0:k0 + P_C, k0:k0 + N_C] if mine
                                      else jnp.zeros((P_C, N_C), F32))


def _mixer(proj, dn_conv_w, sc_conv_w, ssd_conv_w, lanev, ssdv, s_dn0, s_ssd0, state_out):
    nb, seq_len, _ = proj.shape
    nc = seq_len // CHUNK
    assert nc % 2 == 0
    zero_init = s_dn0 is None
    full = lambda a: pl.BlockSpec(a.shape, lambda b: (0,) * a.ndim)
    args = [proj, dn_conv_w, sc_conv_w, ssd_conv_w, lanev, ssdv]
    in_specs = [pl.BlockSpec((1, seq_len, D_IN_PAD), lambda b: (b, 0, 0), pipeline_mode=pl.Buffered(1))
                if seq_len > 512 else pl.BlockSpec((1, seq_len, D_IN_PAD), lambda b: (b, 0, 0)),
                full(dn_conv_w), full(sc_conv_w), full(ssd_conv_w), full(lanev), full(ssdv)]
    if not zero_init:
        args += [s_dn0, s_ssd0]
        in_specs += [pl.BlockSpec((1, 2, H_A, DK_A, DV_A), lambda b: (b, 0, 0, 0, 0)),
                     pl.BlockSpec((1, 2, H_C, P_C, N_C), lambda b: (b, 0, 0, 0, 0))]
    out_shape = [jax.ShapeDtypeStruct((nb, seq_len, D_MODEL), BF16)]
    out_specs = [pl.BlockSpec((1, seq_len, D_MODEL), lambda b: (b, 0, 0))]
    aliases = {}
    layer, depth, prev = state_out if state_out is not None else (None, None, None)
    if state_out is not None:
        out_shape += [jax.ShapeDtypeStruct((nb, depth, 2, H_A, DK_A, DV_A), F32),
                      jax.ShapeDtypeStruct((nb, depth, 2, H_C, P_C, N_C), F32)]
        if prev is None:
            out_specs += [pl.BlockSpec((1, depth, 2, H_A, DK_A, DV_A), lambda b: (b, 0, 0, 0, 0, 0)),
                          pl.BlockSpec((1, depth, 2, H_C, P_C, N_C), lambda b: (b, 0, 0, 0, 0, 0))]
        else:
            aliases = {len(args): 1, len(args) + 1: 2}
            args += list(prev)
            in_specs += [pl.BlockSpec(memory_space=pl.ANY), pl.BlockSpec(memory_space=pl.ANY)]
            out_specs += [pl.BlockSpec((1, None, 2, H_A, DK_A, DV_A), lambda b: (b, layer, 0, 0, 0, 0)),
                          pl.BlockSpec((1, None, 2, H_C, P_C, N_C), lambda b: (b, layer, 0, 0, 0, 0))]
    scratch = [pltpu.VMEM((seq_len, 3 * A_W), F32),
               pltpu.VMEM((seq_len, C_W), F32),
               pltpu.VMEM((seq_len, G_C * LANES), F32),
               pltpu.VMEM((seq_len, G_C * LANES), F32),
               pltpu.VMEM((2, seq_len, LANES), F32),
               pltpu.VMEM((2, nc, LANES, CHUNK), F32),
               pltpu.VMEM((2, nc, 8, LANES), F32),
               pltpu.VMEM((nc, 8, LANES), F32),
               pltpu.VMEM((seq_len, LANES), F32),
               pltpu.VMEM((seq_len, LANES), F32),
               pltpu.VMEM((2, nc, H_A, CHUNK, DV_A), F32),
               pltpu.VMEM((2, nc, H_A, 2 * CHUNK, DK_A), BF16),
               pltpu.VMEM((2, nc, H_A, CHUNK, DK_A), BF16),
               pltpu.VMEM((2, nc, H_A, CHUNK, CHUNK), BF16),
               pltpu.VMEM((seq_len, A_W), F32),
               pltpu.VMEM((seq_len, C_W), F32),
               pltpu.VMEM((2 * H_A, DK_A, DV_A), F32),
               pltpu.VMEM((2, G_C, HPG * P_C, HPG * N_C), F32)]
    return pl.pallas_call(
        functools.partial(_mixer_kernel, seq_len=seq_len, zero_init=zero_init, state_layer=layer,
                          state_layers=depth, state_aliased=prev is not None),
        grid=(nb,),
        in_specs=in_specs,
        out_specs=out_specs,
        out_shape=out_shape,
        input_output_aliases=aliases,
        scratch_shapes=scratch,
        compiler_params=_params(("arbitrary",)),
        name="mixer",
    )(*args)


def _top2_sum(a, b, c, d):
    hi1, lo1 = jnp.maximum(a, b), jnp.minimum(a, b)
    hi2, lo2 = jnp.maximum(c, d), jnp.minimum(c, d)
    return jnp.maximum(hi1, hi2) + jnp.maximum(jnp.minimum(hi1, hi2), jnp.maximum(lo1, lo2))


def _outproj_kernel(y_ref, x_ref, mod_ref, w_ref, g_ref, rw_ref, rb_ref, x1_ref, h2_ref, comb_ref, wb_ref):
    @pl.when(pl.program_id(0) == 0)
    def _():
        wb_ref[...] = w_ref[...].astype(wb_ref.dtype)

    m = mod_ref[0]
    gate1, shift2, scale2 = m[2:3, :], m[3:4, :], m[4:5, :]
    x1 = x_ref[...] + gate1 * _dot(y_ref[...], wb_ref[...])
    x1_ref[...] = x1
    h2 = _rms(x1, g_ref[...]) * (1.0 + scale2) + shift2
    h2_ref[...] = h2.astype(h2_ref.dtype)

    h_hi = h2.astype(BF16)
    h_lo = (h2 - h_hi.astype(F32)).astype(BF16)
    rw_hi, rw_lo = rw_ref[0], rw_ref[1]
    hl = jnp.dot(jnp.concatenate([h_hi, h_lo], axis=0), rw_hi, preferred_element_type=F32)
    tm = h2.shape[0]
    logits = hl[:tm] + hl[tm:] + jnp.dot(h_hi, rw_lo, preferred_element_type=F32)
    scores = _sigmoid(logits.T[:N_EXPERTS, :])
    biased = scores + rb_ref[...]
    sc = [scores[e:e + 1, :] for e in range(N_EXPERTS)]
    bi = [biased[e:e + 1, :] for e in range(N_EXPERTS)]
    gs = [_top2_sum(*bi[EPG * g:EPG * (g + 1)]) for g in range(N_GROUPS)]
    gmax = functools.reduce(jnp.maximum, gs)
    first = []
    taken = None
    for g in range(N_GROUPS):
        hit = gs[g] == gmax
        if taken is None:
            first.append(hit)
            taken = hit
        else:
            first.append(hit & jnp.logical_not(taken))
            taken = taken | hit

    def pick(vals, j):
        out = vals[EPG * (N_GROUPS - 1) + j]
        for g in range(N_GROUPS - 2, -1, -1):
            out = jnp.where(first[g], vals[EPG * g + j], out)
        return out

    ib = [pick(bi, j) for j in range(EPG)]
    isc = [pick(sc, j) for j in range(EPG)]
    sel = []
    for j in range(EPG):
        cnt = jnp.zeros_like(ib[j])
        for i in range(EPG):
            if i == j:
                continue
            ahead = (ib[i] > ib[j]) | ((ib[i] == ib[j]) if i < j else False)
            cnt = cnt + jnp.where(ahead, 1.0, 0.0)
        sel.append(cnt < 2.0)
    wj = [jnp.where(sel[j], isc[j], 0.0) for j in range(EPG)]
    denom = functools.reduce(lambda a, b: a + b, wj)
    for g in range(N_GROUPS):
        for j in range(EPG):
            comb_ref[EPG * g + j:EPG * g + j + 1, :] = jnp.where(first[g], wj[j] / denom, 0.0)


def _outproj(ycat, x, mod6, seq_rows, w_out, layer, norm_g, router_w_pad, router_b):
    t, d = x.shape
    tm = _tile(seq_rows, 512)
    return pl.pallas_call(
        _outproj_kernel,
        grid=(t // tm,),
        in_specs=[pl.BlockSpec((tm, d), lambda i: (i, 0)),
                  pl.BlockSpec((tm, d), lambda i: (i, 0)),
                  pl.BlockSpec((1, 6, d), lambda i: ((i * tm) // seq_rows, 0, 0)),
                  pl.BlockSpec((None, d, d), lambda i: (layer, 0, 0)),
                  pl.BlockSpec((1, d), lambda i: (0, 0)),
                  pl.BlockSpec((2, d, LANES), lambda i: (0, 0, 0)),
                  pl.BlockSpec((N_EXPERTS, 1), lambda i: (0, 0))],
        out_specs=[pl.BlockSpec((tm, d), lambda i: (i, 0)),
                   pl.BlockSpec((tm, d), lambda i: (i, 0)),
                   pl.BlockSpec((N_EXPERTS, tm), lambda i: (0, i))],
        out_shape=[jax.ShapeDtypeStruct((t, d), F32),
                   jax.ShapeDtypeStruct((t, d), BF16),
                   jax.ShapeDtypeStruct((N_EXPERTS, t), F32)],
        scratch_shapes=[pltpu.VMEM((d, d), BF16)],
        compiler_params=_params(("arbitrary",)),
        name="outproj_route",
    )(ycat, x, mod6, w_out, norm_g.reshape(1, d), router_w_pad, router_b.reshape(N_EXPERTS, 1))


def _moe_kernel(h_ref, comb_ref, wg_ref, wu_ref, wd_ref, x1_ref, mod_ref, fg_ref, o_ref, *, final):
    g = pl.program_id(1)

    @pl.when(g == 0)
    def _():
        o_ref[...] = jnp.zeros_like(o_ref)

    h = h_ref[...]
    comb = comb_ref[...]
    acts = [(_silu(_dot(h, wg_ref[j])) * _dot(h, wu_ref[j]) * comb[:, j:j + 1]).astype(BF16) for j in range(EPG)]
    o_ref[...] += _dot(jnp.concatenate(acts, axis=1), wd_ref[...])

    @pl.when(g == pl.num_programs(1) - 1)
    def _():
        x2 = x1_ref[...] + mod_ref[0][5:6, :] * o_ref[...]
        o_ref[...] = _rms(x2, fg_ref[...]) if final else x2


def _moe(h2, comb, w_gate, w_up, w_down, layer, x1, mod6, seq_rows, final_g, final):
    t, d = x1.shape
    tm = _tile(seq_rows, 1024)
    return pl.pallas_call(
        functools.partial(_moe_kernel, final=final),
        grid=(t // tm, N_GROUPS),
        in_specs=[pl.BlockSpec((tm, d), lambda i, g: (i, 0)),
                  pl.BlockSpec((None, tm, EPG), lambda i, g: (g, i, 0)),
                  pl.BlockSpec((None, EPG, d, D_EXPERT), lambda i, g: (layer, g, 0, 0)),
                  pl.BlockSpec((None, EPG, d, D_EXPERT), lambda i, g: (layer, g, 0, 0)),
                  pl.BlockSpec((None, None, EPG * D_EXPERT, d), lambda i, g: (layer, g, 0, 0)),
                  pl.BlockSpec((tm, d), lambda i, g: (i, 0)),
                  pl.BlockSpec((1, 6, d), lambda i, g: ((i * tm) // seq_rows, 0, 0)),
                  pl.BlockSpec((1, d), lambda i, g: (0, 0))],
        out_specs=pl.BlockSpec((tm, d), lambda i, g: (i, 0)),
        out_shape=jax.ShapeDtypeStruct((t, d), F32),
        compiler_params=_params(("arbitrary", "arbitrary")),
        name="experts",
    )(h2, comb, w_gate, w_up, w_down, x1, mod6, final_g.reshape(1, d))


def _lane_row(*pieces):
    row = jnp.zeros((LANES,), F32)
    for lane, vals in pieces:
        row = lax.dynamic_update_slice(row, vals.reshape(-1).astype(F32), (lane,))
    return row


def kernel(x_prompt, x_sample, state_delta, state_ssd, c, c_ctx, mod_w, mod_b, norm1_g, norm2_g, w_in, w_out,
           dn_conv_w, dn_a_log, dn_dt_bias, dn_norm_g, sc_conv_w, ssd_conv_w, ssd_a_log, ssd_dt_bias, ssd_d,
           ssd_norm_g, router_w, router_b, exp_w_gate, exp_w_up, exp_w_down, final_norm_g):
    depth = mod_w.shape[0]
    n_ctx, seq, d = x_prompt.shape
    n_dec, dec_seq, _ = x_sample.shape
    assert seq % CHUNK == 0 and dec_seq % CHUNK == 0 and d == D_MODEL

    n_rows = -(-(1 + n_dec) // 8) * 8
    cond_rows = jnp.zeros((n_rows, d), F32).at[0].set(c_ctx).at[1:1 + n_dec].set(c)
    mod = _modulation(cond_rows, mod_w, mod_b).reshape(depth, n_rows, 6, d)

    w_in_b = w_in.astype(BF16)
    w_in_r = jnp.concatenate(
        [w_in_b[:, :, :SRC_SMALL_A], w_in_b[:, :, SRC_SCH:SRC_SMALL_B], w_in_b[:, :, SRC_SMALL_A:SRC_SCH],
         w_in_b[:, :, SRC_SMALL_B:], jnp.zeros((depth, d, D_IN_PAD - D_IN), BF16)], axis=-1)
    w_down_g = exp_w_down.reshape(depth, N_GROUPS, EPG * D_EXPERT, d)
    rw = jnp.pad(router_w.astype(F32), ((0, 0), (0, LANES - N_EXPERTS)))
    rw_hi = rw.astype(BF16)
    router_w_pad = jnp.stack([rw_hi, (rw - rw_hi.astype(F32)).astype(BF16)])

    xp = x_prompt.reshape(n_ctx * seq, d)
    pos = jnp.asarray(_grid_pos_embed(dec_seq, d), dtype=x_sample.dtype)
    xs = _add_pos(x_sample, pos).reshape(n_dec * dec_seq, d)

    states = None
    for l in range(depth):
        lanev = jnp.zeros((8, LANES), F32)
        lanev = lanev.at[0].set(_lane_row((LANE_ALPHA, dn_a_log[l]), (LANE_DT, ssd_a_log[l])))
        lanev = lanev.at[1].set(_lane_row((LANE_ALPHA, dn_dt_bias[l]), (LANE_DT, ssd_dt_bias[l])))
        lanev = lanev.at[2].set(dn_norm_g[l].astype(F32))
        ssdv = jnp.zeros((8, C_W), F32).at[0].set(jnp.repeat(ssd_d[l].astype(F32), P_C)).at[1].set(ssd_norm_g[l])
        final = l == depth - 1

        def block(x, mod6, nb, seq_len, seq_rows, s_dn0, s_ssd0, state_out):
            proj = _inproj(x, mod6, seq_rows, norm1_g[l], w_in_r, l).reshape(nb, seq_len, D_IN_PAD)
            outs = _mixer(proj, dn_conv_w[l], sc_conv_w[l], ssd_conv_w[l], lanev, ssdv, s_dn0, s_ssd0, state_out)
            ycat = outs[0].reshape(nb * seq_len, d)
            x1, h2, comb_t = _outproj(ycat, x, mod6, seq_rows, w_out, l, norm2_g[l], router_w_pad, router_b)
            comb = comb_t.reshape(N_GROUPS, EPG, -1).transpose(0, 2, 1)
            x2 = _moe(h2, comb, exp_w_gate, exp_w_up, w_down_g, l, x1, mod6, seq_rows, final_norm_g, final)
            return x2, outs[1:]

        xp, states = block(xp, mod[l, 0:1], n_ctx, seq, n_ctx * seq, None, None, (l, depth, states))
        xs, _ = block(xs, mod[l, 1:1 + n_dec], n_dec, dec_seq, dec_seq,
                      state_delta[:, l].astype(F32), state_ssd[:, l].astype(F32), None)

    return (xp.reshape(n_ctx, seq, d), xs.reshape(n_dec, dec_seq, d), states[0], states[1])
```

```python
import functools
import math
import types

import jax
import jax.numpy as jnp
import numpy as np
from jax import lax
from jax.experimental import pallas as pl
from jax.experimental.pallas import tpu as pltpu

F32 = jnp.float32
BF16 = jnp.bfloat16

D_MODEL = 1024
GRID_W = 64
POS_BASE = 10000.0
H_A, DK_A, DV_A = 4, 128, 128
A_W = H_A * DV_A
H_C, P_C, N_C, G_C = 4, 64, 64, 2
HPG = H_C // G_C
C_W = H_C * P_C
B_W = D_MODEL - A_W - C_W
XBC_W = C_W + 2 * G_C * N_C
CHUNK = 64
N_EXPERTS = 16
N_GROUPS = 4
EPG = N_EXPERTS // N_GROUPS
D_EXPERT = 256
EPS = 1e-6
LANES = 128

COL_Q, COL_K, COL_V, COL_GATE = 0, A_W, 2 * A_W, 3 * A_W
COL_SCH = 4 * A_W
COL_SCB = COL_SCH + B_W
COL_SCC = COL_SCB + B_W
COL_Z = COL_SCC + B_W
COL_XBC = COL_Z + C_W
COL_SMALL = COL_XBC + XBC_W
D_IN_PAD = COL_SMALL + LANES
LANE_BETA, LANE_ALPHA, LANE_DT = 0, 2 * H_A, 4 * H_A
SRC_SMALL_A = 4 * A_W
SRC_SCH = SRC_SMALL_A + 4 * H_A
SRC_SMALL_B = SRC_SCH + 3 * B_W + C_W + XBC_W
D_IN = SRC_SMALL_B + 2 * H_C

assert HPG == 2 and P_C == N_C == CHUNK and HPG * P_C == LANES and G_C * N_C == LANES and DK_A == DV_A == LANES

VMEM_LIMIT = 56 * 1024 * 1024
STATIC_PREP_CHUNKS = 4


def _dot(a, b):
    return jnp.dot(a.astype(BF16), b.astype(BF16), preferred_element_type=F32)


def _dot_nt(a, b):
    return lax.dot_general(a.astype(BF16), b.astype(BF16), (((1,), (1,)), ((), ())), preferred_element_type=F32)


def _dot_tn(a, b):
    return lax.dot_general(a.astype(BF16), b.astype(BF16), (((0,), (0,)), ((), ())), preferred_element_type=F32)


def _dot_f32(a, b):
    return jnp.dot(a, b, precision=lax.Precision.HIGHEST, preferred_element_type=F32)


def _dot_nt_f32(a, b):
    return lax.dot_general(a, b, (((1,), (1,)), ((), ())), precision=lax.Precision.HIGHEST,
                           preferred_element_type=F32)


def _silu(x):
    h = 0.5 * x
    return h + h * jnp.tanh(h)


def _sigmoid(x):
    return 1.0 / (1.0 + jnp.exp(-x))


def _softplus(x):
    return jnp.maximum(x, 0.0) + jnp.log1p(jnp.exp(-jnp.abs(x)))


def _rms(x, g):
    return x * lax.rsqrt(jnp.mean(x * x, axis=-1, keepdims=True) + EPS) * g


def _tile(n, pref):
    t = min(n, pref)
    while n % t:
        t -= 8
    assert t > 0 and t % 8 == 0, (n, pref)
    return t


def _params(sem):
    return pltpu.CompilerParams(dimension_semantics=sem, vmem_limit_bytes=VMEM_LIMIT)


def _mod_kernel(cond_ref, w_ref, b_ref, o_ref):
    s = _silu(cond_ref[...])
    o_ref[0] = _dot_f32(s, w_ref[0]) + b_ref[0]


def _modulation(cond_rows, mod_w, mod_b):
    depth, d, n = mod_w.shape
    r = cond_rows.shape[0]
    tn = _tile(n, 1536)
    return pl.pallas_call(
        _mod_kernel,
        grid=(depth, n // tn),
        in_specs=[pl.BlockSpec((r, d), lambda l, j: (0, 0)),
                  pl.BlockSpec((1, d, tn), lambda l, j: (l, 0, j)),
                  pl.BlockSpec((1, 1, tn), lambda l, j: (l, 0, j))],
        out_specs=pl.BlockSpec((1, r, tn), lambda l, j: (l, 0, j)),
        out_shape=jax.ShapeDtypeStruct((depth, r, n), F32),
        compiler_params=_params(("arbitrary", "arbitrary")),
        name="modulation",
    )(cond_rows, mod_w, mod_b.reshape(depth, 1, n))


def _add_kernel(x_ref, p_ref, o_ref):
    o_ref[0] = x_ref[0] + p_ref[...]


def _add_pos(x, pos):
    nb, l, d = x.shape
    tl = _tile(l, 512)
    return pl.pallas_call(
        _add_kernel,
        grid=(nb, l // tl),
        in_specs=[pl.BlockSpec((1, tl, d), lambda b, i: (b, i, 0)),
                  pl.BlockSpec((tl, d), lambda b, i: (i, 0))],
        out_specs=pl.BlockSpec((1, tl, d), lambda b, i: (b, i, 0)),
        out_shape=jax.ShapeDtypeStruct(x.shape, x.dtype),
        compiler_params=_params(("arbitrary", "arbitrary")),
        name="add_pos",
    )(x, pos)


def _grid_pos_embed(n_tok, dim):
    rows = n_tok // GRID_W
    rr, cc = np.meshgrid(np.arange(rows, dtype=np.float64), np.arange(GRID_W, dtype=np.float64), indexing="ij")
    quarter = dim // 4
    omega = 1.0 / (POS_BASE ** (np.arange(quarter, dtype=np.float64) / quarter))
    ang_r = rr.reshape(-1, 1) * omega
    ang_c = cc.reshape(-1, 1) * omega
    return np.concatenate([np.sin(ang_r), np.cos(ang_r), np.sin(ang_c), np.cos(ang_c)], axis=-1)


def _inproj_kernel(x_ref, mod_ref, g_ref, w_ref, o_ref):
    m = mod_ref[0]
    shift, scale = m[0:1, :], m[1:2, :]
    h = _rms(x_ref[...], g_ref[...]) * (1.0 + scale) + shift
    o_ref[...] = _dot(h, w_ref[...])


def _inproj(x, mod6, seq_rows, norm_g, w_in_r, layer):
    t, d = x.shape
    tm = _tile(seq_rows, 512)
    return pl.pallas_call(
        _inproj_kernel,
        grid=(t // tm,),
        in_specs=[pl.BlockSpec((tm, d), lambda i: (i, 0)),
                  pl.BlockSpec((1, 6, d), lambda i: ((i * tm) // seq_rows, 0, 0)),
                  pl.BlockSpec((1, d), lambda i: (0, 0)),
                  pl.BlockSpec((None, d, D_IN_PAD), lambda i: (layer, 0, 0))],
        out_specs=pl.BlockSpec((tm, D_IN_PAD), lambda i: (i, 0)),
        out_shape=jax.ShapeDtypeStruct((t, D_IN_PAD), F32),
        compiler_params=_params(("arbitrary",)),
        name="inproj",
    )(x, mod6, norm_g.reshape(1, d), w_in_r)


CONV_WIN = CHUNK + 16


def _conv_select(r0, w0):
    ti = lax.broadcasted_iota(jnp.int32, (2 * CHUNK, 2 * CONV_WIN), 0)
    tj = lax.broadcasted_iota(jnp.int32, (2 * CHUNK, 2 * CONV_WIN), 1)
    want = r0 + jnp.where(ti < CHUNK, ti - 1, ti - CHUNK + 1)
    have = w0 + jnp.where(tj < CONV_WIN, tj, tj - CONV_WIN)
    return jnp.where(want == have, 1.0, 0.0).astype(BF16)


def _conv3(load, r0, w0, sel, w):
    cur = load(pl.ds(r0, CHUNK))
    win = load(pl.ds(w0, CONV_WIN))
    hi = win.astype(BF16)
    lo = (win - hi.astype(F32)).astype(BF16)
    nb = jnp.dot(sel, jnp.concatenate([hi, lo], axis=0), preferred_element_type=F32)
    return w[0:1, :] * nb[:CHUNK] + w[1:2, :] * cur + w[2:3, :] * nb[CHUNK:]


def _mixer_kernel(*refs, seqs_per_step, batched_in, n_in, n_out, **static):
    ins, outs, scratch = refs[:n_in], refs[n_in:n_in + n_out], refs[n_in + n_out:]
    n_scr = len(scratch) // seqs_per_step
    seqs = []
    for s in range(seqs_per_step):
        one = lambda r, s=s: r.at[pl.ds(s, 1)]
        seqs.append(_mixer_phases([one(r) if i in batched_in else r for i, r in enumerate(ins)]
                                  + [one(r) for r in outs] + list(scratch[s * n_scr:(s + 1) * n_scr]), **static))
    nc, cpi = seqs[0].nc, seqs[0].cpi
    if nc <= STATIC_PREP_CHUNKS:
        for q in seqs:
            for z in range(nc):
                q.prep(z, 0)
            q.init()
        for q in seqs:
            for i in range(nc // cpi):
                q.delta_prep(i, 0)
        for z in range(nc):
            _scan_step(seqs, z)
        for q in seqs:
            for z in range(nc):
                q.finish(z, 0)
    else:
        for q in seqs:
            lax.fori_loop(0, nc, q.prep, 0, unroll=2)
            q.init()
            lax.fori_loop(0, nc // cpi, q.delta_prep, 0)

        def scan(z, carry):
            _scan_step(seqs, z)
            return carry

        lax.fori_loop(0, nc, scan, 0, unroll=4)
        for q in seqs:
            lax.fori_loop(0, nc, q.finish, 0, unroll=2)
    for q in seqs:
        q.emit()


def _scan_step(seqs, z):
    states = [q.scan_begin(z) for q in seqs]
    for stage in range(len(seqs[0].scan_stages)):
        for q, st in zip(seqs, states):
            q.scan_stages[stage](st)


def _mixer_phases(refs, seq_len, zero_init, state_layer, state_layers, state_aliased):
    refs = list(refs)
    proj, dnw, scw, ssw, lanev, ssdv = refs[:6]
    k = 6
    if not zero_init:
        sdn0, sssd0 = refs[k:k + 2]
        k += 2
    emit_state = state_layer is not None
    if state_aliased:
        k += 2
    ycat = refs[k]
    k += 1
    if emit_state:
        sdn_out, sssd_out = refs[k:k + 2]
        k += 2
    (qkv_s, xbc_s, bdup_s, cdup_s, gc_s, dpair_s, gcpair_s, tot_s, sp_s, beta_s, u_s, wq_s, kd_s, qk_s, o_s, y_s,
     st_s, hs_s) = refs[k:]

    nc = seq_len // CHUNK
    ri = lax.broadcasted_iota(jnp.int32, (CHUNK, CHUNK), 0)
    ci = lax.broadcasted_iota(jnp.int32, (CHUNK, CHUNK), 1)
    tril = (ri >= ci).astype(F32)
    r128 = lax.broadcasted_iota(jnp.int32, (LANES, LANES), 0)
    c128 = lax.broadcasted_iota(jnp.int32, (LANES, LANES), 1)
    eye128 = (r128 == c128).astype(F32)
    n_lvl = int(math.log2(CHUNK))
    alog = lanev[0:1, :]
    bias = lanev[1:2, :]
    pw = 2 * CHUNK
    ri2 = lax.broadcasted_iota(jnp.int32, (CHUNK, pw), 0)
    ci2 = lax.broadcasted_iota(jnp.int32, (CHUNK, pw), 1)
    lane_hi = ci2 >= CHUNK
    tj = ci2 & (CHUNK - 1)
    incl2 = (ri2 >= tj, ri2 <= tj)
    ahead = jnp.where(lane_hi, tj - ri2, ri2 - tj)
    incl_fb = ahead >= 0
    strict_fb = ahead > 0
    eye2 = (ri2 == tj).astype(F32)
    lvl2 = [((ri2 >> s) == (tj >> s)) & ((ri2 >> (s - 1)) != (tj >> (s - 1))) for s in range(1, n_lvl + 1)]

    def block_diag(x):
        return jnp.concatenate([jnp.where(lane_hi, 0.0, x), jnp.where(lane_hi, x, 0.0)], axis=0).astype(BF16)

    def chunk_rows(z):
        return pl.ds(z * CHUNK if isinstance(z, int) else pl.multiple_of(z * CHUNK, CHUNK), CHUNK)

    def prep(z, carry):
        r0 = z * CHUNK if isinstance(z, int) else pl.multiple_of(z * CHUNK, CHUNK)
        rows = pl.ds(r0, CHUNK)
        if isinstance(z, int):
            w0 = min(max(r0 - 8, 0), seq_len - CONV_WIN)
        else:
            w0 = pl.multiple_of(jnp.clip(r0 - 8, 0, seq_len - CONV_WIN), 8)
        sel = _conv_select(r0, w0)
        cw = 2 * LANES
        for jj in range(3 * A_W // cw):
            c0 = jj * cw
            a2 = _silu(_conv3(lambda rs: proj[0, rs, c0:c0 + cw], r0, w0, sel, dnw[:, c0:c0 + cw]))
            for half in range(cw // LANES):
                a = a2[:, half * LANES:(half + 1) * LANES]
                j = jj * (cw // LANES) + half
                if j < 2 * H_A:
                    a = a * lax.rsqrt(jnp.sum(a * a, axis=-1, keepdims=True) + EPS)
                if j < H_A:
                    a = a * (DK_A ** -0.5)
                qkv_s[rows, j * LANES:(j + 1) * LANES] = a
        for jj in range(B_W // cw):
            c0 = jj * cw
            cv = _conv3(lambda rs: proj[0, rs, COL_SCC + c0:COL_SCC + c0 + cw]
                        * proj[0, rs, COL_SCH + c0:COL_SCH + c0 + cw], r0, w0, sel, scw[:, c0:c0 + cw])
            yb = proj[0, rows, COL_SCB + c0:COL_SCB + c0 + cw] * cv
            ycat[0, rows, A_W + c0:A_W + c0 + cw] = yb.astype(ycat.dtype)
        for jj in range(XBC_W // cw):
            c1 = jj * cw
            a2 = _silu(_conv3(lambda rs: proj[0, rs, COL_XBC + c1:COL_XBC + c1 + cw], r0, w0, sel,
                              ssw[:, c1:c1 + cw]))
            if c1 < C_W:
                xbc_s[rows, c1:c1 + cw] = a2
                continue
            for half, dup_s in enumerate((bdup_s, cdup_s)):
                a = a2[:, half * LANES:(half + 1) * LANES]
                swapped = pltpu.roll(a, N_C, 1)
                lo = lax.broadcasted_iota(jnp.int32, a.shape, 1) < N_C
                dup_s[rows, 0:LANES] = jnp.where(lo, a, swapped)
                dup_s[rows, LANES:2 * LANES] = jnp.where(lo, swapped, a)
        sm = proj[0, rows, COL_SMALL:COL_SMALL + LANES]
        sp = _softplus(sm + bias)
        g = -jnp.exp(alog) * sp
        pre = _dot_f32(tril, g)
        tot = pre[CHUNK - 1:CHUNK, :]
        suf = tot - pre + g
        sp_s[rows, :] = sp
        beta_s[rows, :] = _sigmoid(sm)
        gc_s[0, rows, :] = pre
        gc_s[1, rows, :] = suf
        grs = [_dot_nt_f32(eye128, gcd) for gcd in (pre, suf)]
        for d, gr in enumerate(grs):
            for g in range(G_C):
                ln = LANE_DT + d * H_C + g * HPG
                gcpair_s[d, z, g:g + 1, :] = jnp.concatenate([gr[ln:ln + 1, :], gr[ln + 1:ln + 2, :]], axis=1)
        for h in range(H_A):
            lf, lb = LANE_ALPHA + h, LANE_ALPHA + H_A + h
            dpair_s[z, h:h + 1, :] = jnp.concatenate([grs[0][lf:lf + 1, :], grs[1][lb:lb + 1, :]], axis=1)
        tot_s[z] = jnp.broadcast_to(tot, (8, LANES))
        o_s[rows, :] = jnp.zeros((CHUNK, A_W), F32)
        y_s[rows, :] = jnp.zeros((CHUNK, C_W), F32)
        return carry

    def init():
        for d in range(2):
            for h in range(H_A):
                st_s[d * H_A + h] = jnp.zeros((DK_A, DV_A), F32) if zero_init else sdn0[0, d, h]
            for g in range(G_C):
                if zero_init:
                    hs_s[d, g] = jnp.zeros((HPG * P_C, HPG * N_C), F32)
                else:
                    zero = jnp.zeros((P_C, N_C), F32)
                    hs_s[d, g] = jnp.concatenate(
                        [jnp.concatenate([sssd0[0, d, g * HPG], zero], axis=1),
                         jnp.concatenate([zero, sssd0[0, d, g * HPG + 1]], axis=1)], axis=0)

    cpi = 4 if nc % 4 == 0 else 2

    def delta_prep(i, carry):
        units = []
        for zz in range(cpi):
            z = cpi * i + zz
            rows = chunk_rows(z)
            tot = tot_s[z][0:1, :]
            beta = beta_s[rows, :]
            gcs = [gc_s[d, rows, :] for d in range(2)]
            for h in range(H_A):
                q_h = qkv_s[rows, COL_Q + h * DK_A:COL_Q + (h + 1) * DK_A]
                k_h = qkv_s[rows, COL_K + h * DK_A:COL_K + (h + 1) * DK_A]
                v_h = qkv_s[rows, COL_V + h * DV_A:COL_V + (h + 1) * DV_A]
                units.append(dict(z=z, h=h, q=q_h, k=k_h, v=v_h, tot=tot, beta=beta, gcs=gcs))
        qkk = [_dot_nt(jnp.concatenate([p["q"], p["k"]], axis=0), jnp.concatenate([p["k"], p["k"]], axis=0))
               for p in units]
        ms, rhss = [], []
        for p, qk_kk in zip(units, qkk):
            z, h = p["z"], p["h"]
            lf, lb = LANE_ALPHA + h, LANE_ALPHA + H_A + h
            bf, bb = LANE_BETA + h, LANE_BETA + H_A + h
            a_pair = jnp.where(lane_hi, p["gcs"][1][:, lb:lb + 1], p["gcs"][0][:, lf:lf + 1])
            b_pair = jnp.where(lane_hi, p["beta"][:, bb:bb + 1], p["beta"][:, bf:bf + 1])
            decay = jnp.exp(jnp.where(incl_fb, a_pair - dpair_s[z, h:h + 1, :], -1e30))
            ms.append(jnp.where(strict_fb, qk_kk[CHUNK:] * b_pair * decay, 0.0))
            qk_s[z, h] = (qk_kk[:CHUNK] * decay).astype(qk_s.dtype)
            rhs_d = []
            for d, (ln, bl) in enumerate(((lf, bf), (lb, bb))):
                a_col = p["gcs"][d][:, ln:ln + 1]
                b_col = p["beta"][:, bl:bl + 1]
                eg = jnp.exp(a_col)
                rhs_d.append(jnp.concatenate([p["v"] * b_col, p["k"] * (b_col * eg)], axis=1))
                kd_s[d, z, h] = (p["k"] * jnp.exp(p["tot"][:, ln:ln + 1] - a_col)).astype(kd_s.dtype)
                wq_s[d, z, h, CHUNK:, :] = (p["q"] * eg).astype(wq_s.dtype)
            zero = jnp.zeros_like(rhs_d[0])
            rhss.append(jnp.concatenate([jnp.concatenate([rhs_d[0], zero], axis=1),
                                         jnp.concatenate([zero, rhs_d[1]], axis=1)], axis=0))
        t_inv = [eye2 - jnp.where(lvl2[0], m, 0.0) for m in ms]
        for s in range(1, n_lvl):
            x = [_dot(jnp.where(lvl2[s], m, 0.0), block_diag(t)) for m, t in zip(ms, t_inv)]
            t_inv = [t - _dot(t, block_diag(xx)) for t, xx in zip(t_inv, x)]
        uw = [_dot(t, rhs) for t, rhs in zip(t_inv, rhss)]
        for p, r in zip(units, uw):
            z, h = p["z"], p["h"]
            for d in range(2):
                c0 = d * (DV_A + DK_A)
                u_s[d, z, h] = r[:, c0:c0 + DV_A]
                wq_s[d, z, h, :CHUNK, :] = r[:, c0 + DV_A:c0 + DV_A + DK_A].astype(wq_s.dtype)
        return carry

    rb =lax.broadcasted_iota(jnp.int32, (pw, pw), 0) >= P_C
    cbk = lax.broadcasted_iota(jnp.int32, (pw, pw), 1) >= N_C
    diag_blk = rb == cbk

    def scan_begin(z):
        dus = [(d, (z if d == 0 else nc - 1 - z), h) for d in range(2) for h in range(H_A)]
        s_prev = [st_s[d * H_A + h] for d, _, h in dus]
        us = []
        for d in range(2):
            zc = z if d == 0 else nc - 1 - z
            rows = chunk_rows(zc)
            sp = sp_s[rows, :]
            tot = tot_s[zc][0:1, :]
            gc = gc_s[d, rows, :]
            for g in range(G_C):
                ln = LANE_DT + d * H_C + g * HPG
                gsl = slice(g * pw, (g + 1) * pw)
                a_pair = jnp.where(lane_hi, gc[:, ln + 1:ln + 2], gc[:, ln:ln + 1])
                t_pair = jnp.where(lane_hi, tot[:, ln + 1:ln + 2], tot[:, ln:ln + 1])
                lmat = jnp.exp(jnp.where(incl2[d], a_pair - gcpair_s[d, zc, g:g + 1, :], -1e30))
                xdt = xbc_s[rows, gsl] * jnp.where(lane_hi, sp[:, ln + 1:ln + 2], sp[:, ln:ln + 1])
                b_dup = bdup_s[rows, gsl]
                c_dup = cdup_s[rows, gsl]
                us.append(dict(d=d, g=g, rows=rows, gsl=gsl, lmat=lmat, xdt=xdt, c_dup=c_dup,
                               c_lo=jnp.where(lane_hi, 0.0, c_dup), b_st=jnp.concatenate([b_dup, b_dup], axis=0),
                               bdec=b_dup * jnp.exp(t_pair - a_pair), ea=jnp.exp(a_pair),
                               dec=jnp.where(rb, jnp.exp(tot[:, ln + 1:ln + 2]), jnp.exp(tot[:, ln:ln + 1])),
                               x_bd=jnp.concatenate([jnp.where(lane_hi, 0.0, xdt), jnp.where(lane_hi, xdt, 0.0)],
                                                    axis=0),
                               h_prev=hs_s[d, g]))
        return dict(dus=dus, s_prev=s_prev, us=us)

    def scan_delta_1(t):
        t["ws_qs"] = [_dot(wq_s[idx], s) for idx, s in zip(t["dus"], t["s_prev"])]

    def scan_ssd_1(t):
        t["st"] = [_dot_tn(p["xdt"], p["bdec"]) for p in t["us"]]
        t["cb"] = [_dot_nt(p["c_lo"], p["b_st"]) for p in t["us"]]
        t["y_off"] = [_dot_nt(p["c_dup"], p["h_prev"]) for p in t["us"]]

    def scan_delta_2(t):
        v_new = [u_s[idx] - r[:CHUNK] for idx, r in zip(t["dus"], t["ws_qs"])]
        zero = jnp.zeros((CHUNK, DV_A), F32)
        t["o_in"] = [_dot(qk_s[zc, h], jnp.concatenate([v, zero] if d == 0 else [zero, v], axis=0))
                     for (d, zc, h), v in zip(t["dus"], v_new)]
        t["s_add"] = [_dot_tn(kd_s[idx], v) for idx, v in zip(t["dus"], v_new)]

    def scan_ssd_2(t):
        t["y_diag"] = [_dot(cbd * p["lmat"], p["x_bd"]) for p, cbd in zip(t["us"], t["cb"])]

    def scan_store(t):
        for p, yd, yo, s in zip(t["us"], t["y_diag"], t["y_off"], t["st"]):
            y_s[p["rows"], p["gsl"]] = y_s[p["rows"], p["gsl"]] + yd + yo * p["ea"]
            hs_s[p["d"], p["g"]] = p["h_prev"] * p["dec"] + jnp.where(diag_blk, s, 0.0)
        for (d, zc, h), r, oi, sa, s in zip(t["dus"], t["ws_qs"], t["o_in"], t["s_add"], t["s_prev"]):
            ln = LANE_ALPHA + d * H_A + h
            cs = slice(h * DV_A, (h + 1) * DV_A)
            o_s[chunk_rows(zc), cs] = o_s[chunk_rows(zc), cs] + r[CHUNK:] + oi
            st_s[d * H_A + h] = s * jnp.exp(tot_s[zc][0:1, ln:ln + 1]) + sa


    def finish(z, carry):
        rows = chunk_rows(z)
        for h in range(H_A):
            cs = slice(h * DV_A, (h + 1) * DV_A)
            o = _rms(o_s[rows, cs], lanev[2:3, :])
            o = o * _silu(proj[0, rows, COL_GATE + h * DV_A:COL_GATE + (h + 1) * DV_A])
            ycat[0, rows, cs] = o.astype(ycat.dtype)
        y = y_s[rows, :] + ssdv[0:1, :] * xbc_s[rows, 0:C_W]
        y = _rms(y * _silu(proj[0, rows, COL_Z:COL_Z + C_W]), ssdv[1:2, :])
        ycat[0, rows, A_W + B_W:] = y.astype(ycat.dtype)
        return carry

    def emit():
        if not emit_state:
            return
        slots = (None,) if state_aliased else range(state_layers)
        for slot in slots:
            dn_slot = sdn_out.at[0] if slot is None else sdn_out.at[0, slot]
            ssd_slot = sssd_out.at[0] if slot is None else sssd_out.at[0, slot]
            mine = slot is None or slot == state_layer
            for d in range(2):
                for h in range(H_A):
                    dn_slot[d, h] = st_s[d * H_A + h] if mine else jnp.zeros((DK_A, DV_A), F32)
                for h in range(H_C):
                    k0 = (h % HPG) * P_C
                    ssd_slot[d, h] = (hs_s[d, h // HPG][k0:k0 + P_C, k0:k0 + N_C] if mine
                                      else jnp.zeros((P_C, N_C), F32))

    return types.SimpleNamespace(
        nc=nc, cpi=cpi, prep=prep, init=init, delta_prep=delta_prep, scan_begin=scan_begin,
        scan_stages=(scan_delta_1, scan_ssd_1, scan_delta_2, scan_ssd_2, scan_store), finish=finish, emit=emit)


def _mixer(proj, dn_conv_w, sc_conv_w, ssd_conv_w, lanev, ssdv, s_dn0, s_ssd0, state_out):
    nb, seq_len, _ = proj.shape
    nc = seq_len // CHUNK
    assert nc % 2 == 0
    zero_init = s_dn0 is None
    spg = 2 if nc <= STATIC_PREP_CHUNKS and nb % 2 == 0 else 1
    full = lambda a: pl.BlockSpec(a.shape, lambda b: (0,) * a.ndim)
    args = [proj, dn_conv_w, sc_conv_w, ssd_conv_w, lanev, ssdv]
    batched_in = [0]
    in_specs = [pl.BlockSpec((spg, seq_len, D_IN_PAD), lambda b: (b, 0, 0), pipeline_mode=pl.Buffered(1))
                if seq_len > 512 else pl.BlockSpec((spg, seq_len, D_IN_PAD), lambda b: (b, 0, 0)),
                full(dn_conv_w), full(sc_conv_w), full(ssd_conv_w), full(lanev), full(ssdv)]
    if not zero_init:
        batched_in += [len(args), len(args) + 1]
        args += [s_dn0, s_ssd0]
        in_specs += [pl.BlockSpec((spg, 2, H_A, DK_A, DV_A), lambda b: (b, 0, 0, 0, 0)),
                     pl.BlockSpec((spg, 2, H_C, P_C, N_C), lambda b: (b, 0, 0, 0, 0))]
    out_shape = [jax.ShapeDtypeStruct((nb, seq_len, D_MODEL), BF16)]
    out_specs = [pl.BlockSpec((spg, seq_len, D_MODEL), lambda b: (b, 0, 0))]
    aliases = {}
    layer, depth, prev = state_out if state_out is not None else (None, None, None)
    if state_out is not None:
        out_shape += [jax.ShapeDtypeStruct((nb, depth, 2, H_A, DK_A, DV_A), F32),
                      jax.ShapeDtypeStruct((nb, depth, 2, H_C, P_C, N_C), F32)]
        if prev is None:
            out_specs += [pl.BlockSpec((spg, depth, 2, H_A, DK_A, DV_A), lambda b: (b, 0, 0, 0, 0, 0)),
                          pl.BlockSpec((spg, depth, 2, H_C, P_C, N_C), lambda b: (b, 0, 0, 0, 0, 0))]
        else:
            aliases = {len(args): 1, len(args) + 1: 2}
            args += list(prev)
            in_specs += [pl.BlockSpec(memory_space=pl.ANY), pl.BlockSpec(memory_space=pl.ANY)]
            out_specs += [pl.BlockSpec((spg, None, 2, H_A, DK_A, DV_A), lambda b: (b, layer, 0, 0, 0, 0)),
                          pl.BlockSpec((spg, None, 2, H_C, P_C, N_C), lambda b: (b, layer, 0, 0, 0, 0))]
    scratch = [pltpu.VMEM((seq_len, 3 * A_W), F32),
               pltpu.VMEM((seq_len, C_W), F32),
               pltpu.VMEM((seq_len, G_C * LANES), F32),
               pltpu.VMEM((seq_len, G_C * LANES), F32),
               pltpu.VMEM((2, seq_len, LANES), F32),
               pltpu.VMEM((nc, 8, LANES), F32),
               pltpu.VMEM((2, nc, 8, LANES), F32),
               pltpu.VMEM((nc, 8, LANES), F32),
               pltpu.VMEM((seq_len, LANES), F32),
               pltpu.VMEM((seq_len, LANES), F32),
               pltpu.VMEM((2, nc, H_A, CHUNK, DV_A), F32),
               pltpu.VMEM((2, nc, H_A, 2 * CHUNK, DK_A), BF16),
               pltpu.VMEM((2, nc, H_A, CHUNK, DK_A), BF16),
               pltpu.VMEM((nc, H_A, CHUNK, 2 * CHUNK), BF16),
               pltpu.VMEM((seq_len, A_W), F32),
               pltpu.VMEM((seq_len, C_W), F32),
               pltpu.VMEM((2 * H_A, DK_A, DV_A), F32),
               pltpu.VMEM((2, G_C, HPG * P_C, HPG * N_C), F32)]
    return pl.pallas_call(
        functools.partial(_mixer_kernel, seqs_per_step=spg, batched_in=tuple(batched_in), n_in=len(args),
                          n_out=len(out_shape), seq_len=seq_len, zero_init=zero_init, state_layer=layer,
                          state_layers=depth, state_aliased=prev is not None),
        grid=(nb // spg,),
        in_specs=in_specs,
        out_specs=out_specs,
        out_shape=out_shape,
        input_output_aliases=aliases,
        scratch_shapes=scratch * spg,
        compiler_params=_params(("arbitrary",)),
        name="mixer",
    )(*args)


def _top2_sum(a, b, c, d):
    hi1, lo1 = jnp.maximum(a, b), jnp.minimum(a, b)
    hi2, lo2 = jnp.maximum(c, d), jnp.minimum(c, d)
    return jnp.maximum(hi1, hi2) + jnp.maximum(jnp.minimum(hi1, hi2), jnp.maximum(lo1, lo2))


def _outproj_kernel(y_ref, x_ref, mod_ref, w_ref, g_ref, rw_ref, rb_ref, x1_ref, h2_ref, comb_ref, wb_ref):
    @pl.when(pl.program_id(0) == 0)
    def _():
        wb_ref[...] = w_ref[...].astype(wb_ref.dtype)

    m = mod_ref[0]
    gate1, shift2, scale2 = m[2:3, :], m[3:4, :], m[4:5, :]
    x1 = x_ref[...] + gate1 * _dot(y_ref[...], wb_ref[...])
    x1_ref[...] = x1
    h2 = _rms(x1, g_ref[...]) * (1.0 + scale2) + shift2
    h2_ref[...] = h2.astype(h2_ref.dtype)

    h_hi = h2.astype(BF16)
    h_lo = (h2 - h_hi.astype(F32)).astype(BF16)
    rw_hi, rw_lo = rw_ref[0], rw_ref[1]
    hl = jnp.dot(jnp.concatenate([h_hi, h_lo], axis=0), rw_hi, preferred_element_type=F32)
    tm = h2.shape[0]
    logits = hl[:tm] + hl[tm:] + jnp.dot(h_hi, rw_lo, preferred_element_type=F32)
    scores = _sigmoid(logits.T[:N_EXPERTS, :])
    biased = scores + rb_ref[...]
    sc = [scores[e:e + 1, :] for e in range(N_EXPERTS)]
    bi = [biased[e:e + 1, :] for e in range(N_EXPERTS)]
    gs = [_top2_sum(*bi[EPG * g:EPG * (g + 1)]) for g in range(N_GROUPS)]
    gmax = functools.reduce(jnp.maximum, gs)
    first = []
    taken = None
    for g in range(N_GROUPS):
        hit = gs[g] == gmax
        if taken is None:
            first.append(hit)
            taken = hit
        else:
            first.append(hit & jnp.logical_not(taken))
            taken = taken | hit

    def pick(vals, j):
        out = vals[EPG * (N_GROUPS - 1) + j]
        for g in range(N_GROUPS - 2, -1, -1):
            out = jnp.where(first[g], vals[EPG * g + j], out)
        return out

    ib = [pick(bi, j) for j in range(EPG)]
    isc = [pick(sc, j) for j in range(EPG)]
    sel = []
    for j in range(EPG):
        cnt = jnp.zeros_like(ib[j])
        for i in range(EPG):
            if i == j:
                continue
            ahead = (ib[i] > ib[j]) | ((ib[i] == ib[j]) if i < j else False)
            cnt = cnt + jnp.where(ahead, 1.0, 0.0)
        sel.append(cnt < 2.0)
    wj = [jnp.where(sel[j], isc[j], 0.0) for j in range(EPG)]
    denom = functools.reduce(lambda a, b: a + b, wj)
    for g in range(N_GROUPS):
        for j in range(EPG):
            comb_ref[EPG * g + j:EPG * g + j + 1, :] = jnp.where(first[g], wj[j] / denom, 0.0)


def _outproj(ycat, x, mod6, seq_rows, w_out, layer, norm_g, router_w_pad, router_b):
    t, d = x.shape
    tm = _tile(seq_rows, 512)
    return pl.pallas_call(
        _outproj_kernel,
        grid=(t // tm,),
        in_specs=[pl.BlockSpec((tm, d), lambda i: (i, 0)),
                  pl.BlockSpec((tm, d), lambda i: (i, 0)),
                  pl.BlockSpec((1, 6, d), lambda i: ((i * tm) // seq_rows, 0, 0)),
                  pl.BlockSpec((None, d, d), lambda i: (layer, 0, 0)),
                  pl.BlockSpec((1, d), lambda i: (0, 0)),
                  pl.BlockSpec((2, d, LANES), lambda i: (0, 0, 0)),
                  pl.BlockSpec((N_EXPERTS, 1), lambda i: (0, 0))],
        out_specs=[pl.BlockSpec((tm, d), lambda i: (i, 0)),
                   pl.BlockSpec((tm, d), lambda i: (i, 0)),
                   pl.BlockSpec((N_EXPERTS, tm), lambda i: (0, i))],
        out_shape=[jax.ShapeDtypeStruct((t, d), F32),
                   jax.ShapeDtypeStruct((t, d), BF16),
                   jax.ShapeDtypeStruct((N_EXPERTS, t), F32)],
        scratch_shapes=[pltpu.VMEM((d, d), BF16)],
        compiler_params=_params(("arbitrary",)),
        name="outproj_route",
    )(ycat, x, mod6, w_out, norm_g.reshape(1, d), router_w_pad, router_b.reshape(N_EXPERTS, 1))


def _moe_kernel(h_ref, comb_ref, wg_ref, wu_ref, wd_ref, x1_ref, mod_ref, fg_ref, o_ref, *, final):
    g = pl.program_id(1)

    @pl.when(g == 0)
    def _():
        o_ref[...] = jnp.zeros_like(o_ref)

    h = h_ref[...]
    comb = comb_ref[...]
    acts = [(_silu(_dot(h, wg_ref[j])) * _dot(h, wu_ref[j]) * comb[:, j:j + 1]).astype(BF16) for j in range(EPG)]
    o_ref[...] += _dot(jnp.concatenate(acts, axis=1), wd_ref[...])

    @pl.when(g == pl.num_programs(1) - 1)
    def _():
        x2 = x1_ref[...] + mod_ref[0][5:6, :] * o_ref[...]
        o_ref[...] = _rms(x2, fg_ref[...]) if final else x2


def _moe(h2, comb, w_gate, w_up, w_down, layer, x1, mod6, seq_rows, final_g, final):
    t, d = x1.shape
    tm = _tile(seq_rows, 1024)
    return pl.pallas_call(
        functools.partial(_moe_kernel, final=final),
        grid=(t // tm, N_GROUPS),
        in_specs=[pl.BlockSpec((tm, d), lambda i, g: (i, 0)),
                  pl.BlockSpec((None, tm, EPG), lambda i, g: (g, i, 0)),
                  pl.BlockSpec((None, EPG, d, D_EXPERT), lambda i, g: (layer, g, 0, 0)),
                  pl.BlockSpec((None, EPG, d, D_EXPERT), lambda i, g: (layer, g, 0, 0)),
                  pl.BlockSpec((None, None, EPG * D_EXPERT, d), lambda i, g: (layer, g, 0, 0)),
                  pl.BlockSpec((tm, d), lambda i, g: (i, 0)),
                  pl.BlockSpec((1, 6, d), lambda i, g: ((i * tm) // seq_rows, 0, 0)),
                  pl.BlockSpec((1, d), lambda i, g: (0, 0))],
        out_specs=pl.BlockSpec((tm, d), lambda i, g: (i, 0)),
        out_shape=jax.ShapeDtypeStruct((t, d), F32),
        compiler_params=_params(("arbitrary", "arbitrary")),
        name="experts",
    )(h2, comb, w_gate, w_up, w_down, x1, mod6, final_g.reshape(1, d))


def _lane_row(*pieces):
    row = jnp.zeros((LANES,), F32)
    for lane, vals in pieces:
        row = lax.dynamic_update_slice(row, vals.reshape(-1).astype(F32), (lane,))
    return row


def kernel(x_prompt, x_sample, state_delta, state_ssd, c, c_ctx, mod_w, mod_b, norm1_g, norm2_g, w_in, w_out,
           dn_conv_w, dn_a_log, dn_dt_bias, dn_norm_g, sc_conv_w, ssd_conv_w, ssd_a_log, ssd_dt_bias, ssd_d,
           ssd_norm_g, router_w, router_b, exp_w_gate, exp_w_up, exp_w_down, final_norm_g):
    depth = mod_w.shape[0]
    n_ctx, seq, d = x_prompt.shape
    n_dec, dec_seq, _ = x_sample.shape
    assert seq % CHUNK == 0 and dec_seq % CHUNK == 0 and d == D_MODEL

    n_rows = -(-(1 + n_dec) // 8) * 8
    cond_rows = jnp.zeros((n_rows, d), F32).at[0].set(c_ctx).at[1:1 + n_dec].set(c)
    mod = _modulation(cond_rows, mod_w, mod_b).reshape(depth, n_rows, 6, d)

    w_in_b = w_in.astype(BF16)
    w_in_r = jnp.concatenate(
        [w_in_b[:, :, :SRC_SMALL_A], w_in_b[:, :, SRC_SCH:SRC_SMALL_B], w_in_b[:, :, SRC_SMALL_A:SRC_SCH],
         w_in_b[:, :, SRC_SMALL_B:], jnp.zeros((depth, d, D_IN_PAD - D_IN), BF16)], axis=-1)
    w_down_g = exp_w_down.reshape(depth, N_GROUPS, EPG * D_EXPERT, d)
    rw = jnp.pad(router_w.astype(F32), ((0, 0), (0, LANES - N_EXPERTS)))
    rw_hi = rw.astype(BF16)
    router_w_pad = jnp.stack([rw_hi, (rw - rw_hi.astype(F32)).astype(BF16)])

    xp = x_prompt.reshape(n_ctx * seq, d)
    pos = jnp.asarray(_grid_pos_embed(dec_seq, d), dtype=x_sample.dtype)
    xs = _add_pos(x_sample, pos).reshape(n_dec * dec_seq, d)

    states = None
    for l in range(depth):
        lanev = jnp.zeros((8, LANES), F32)
        lanev = lanev.at[0].set(_lane_row((LANE_ALPHA, dn_a_log[l]), (LANE_DT, ssd_a_log[l])))
        lanev = lanev.at[1].set(_lane_row((LANE_ALPHA, dn_dt_bias[l]), (LANE_DT, ssd_dt_bias[l])))
        lanev = lanev.at[2].set(dn_norm_g[l].astype(F32))
        ssdv = jnp.zeros((8, C_W), F32).at[0].set(jnp.repeat(ssd_d[l].astype(F32), P_C)).at[1].set(ssd_norm_g[l])
        final = l == depth - 1

        def block(x, mod6, nb, seq_len, seq_rows, s_dn0, s_ssd0, state_out):
            proj = _inproj(x, mod6, seq_rows, norm1_g[l], w_in_r, l).reshape(nb, seq_len, D_IN_PAD)
            outs = _mixer(proj, dn_conv_w[l], sc_conv_w[l], ssd_conv_w[l], lanev, ssdv, s_dn0, s_ssd0, state_out)
            ycat = outs[0].reshape(nb * seq_len, d)
            x1, h2, comb_t = _outproj(ycat, x, mod6, seq_rows, w_out, l, norm2_g[l], router_w_pad, router_b)
            comb = comb_t.reshape(N_GROUPS, EPG, -1).transpose(0, 2, 1)
            x2 = _moe(h2, comb, exp_w_gate, exp_w_up, w_down_g, l, x1, mod6, seq_rows, final_norm_g, final)
            return x2, outs[1:]

        xp, states = block(xp, mod[l, 0:1], n_ctx, seq, n_ctx * seq, None, None, (l, depth, states))
        xs, _ = block(xs, mod[l, 1:1 + n_dec], n_dec, dec_seq, dec_seq,
                      state_delta[:, l].astype(F32), state_ssd[:, l].astype(F32), None)

    return (xp.reshape(n_ctx, seq, d), xs.reshape(n_dec, dec_seq, d), states[0], states[1])
```

```python
import functools
import math
import types

import jax
import jax.numpy as jnp
import numpy as np
from jax import lax
from jax.experimental import pallas as pl
from jax.experimental.pallas import tpu as pltpu

F32 = jnp.float32
BF16 = jnp.bfloat16

D_MODEL = 1024
GRID_W = 64
POS_BASE = 10000.0
H_A, DK_A, DV_A = 4, 128, 128
A_W = H_A * DV_A
H_C, P_C, N_C, G_C = 4, 64, 64, 2
HPG = H_C // G_C
C_W = H_C * P_C
B_W = D_MODEL - A_W - C_W
XBC_W = C_W + 2 * G_C * N_C
CHUNK = 64
N_EXPERTS = 16
N_GROUPS = 4
EPG = N_EXPERTS // N_GROUPS
D_EXPERT = 256
EPS = 1e-6
LANES = 128

COL_Q, COL_K, COL_V, COL_GATE = 0, A_W, 2 * A_W, 3 * A_W
COL_SCH = 4 * A_W
COL_SCB = COL_SCH + B_W
COL_SCC = COL_SCB + B_W
COL_Z = COL_SCC + B_W
COL_XBC = COL_Z + C_W
COL_SMALL = COL_XBC + XBC_W
D_IN_PAD = COL_SMALL + LANES
LANE_BETA, LANE_ALPHA, LANE_DT = 0, 2 * H_A, 4 * H_A
SRC_SMALL_A = 4 * A_W
SRC_SCH = SRC_SMALL_A + 4 * H_A
SRC_SMALL_B = SRC_SCH + 3 * B_W + C_W + XBC_W
D_IN = SRC_SMALL_B + 2 * H_C

assert HPG == 2 and P_C == N_C == CHUNK and HPG * P_C == LANES and G_C * N_C == LANES and DK_A == DV_A == LANES

VMEM_LIMIT = 56 * 1024 * 1024
STATIC_PREP_CHUNKS = 4


def _dot(a, b):
    return jnp.dot(a.astype(BF16), b.astype(BF16), preferred_element_type=F32)


def _dot_nt(a, b):
    return lax.dot_general(a.astype(BF16), b.astype(BF16), (((1,), (1,)), ((), ())), preferred_element_type=F32)


def _dot_tn(a, b):
    return lax.dot_general(a.astype(BF16), b.astype(BF16), (((0,), (0,)), ((), ())), preferred_element_type=F32)


def _dot_f32(a, b):
    return jnp.dot(a, b, precision=lax.Precision.HIGHEST, preferred_element_type=F32)


def _dot_nt_f32(a, b):
    return lax.dot_general(a, b, (((1,), (1,)), ((), ())), precision=lax.Precision.HIGHEST,
                           preferred_element_type=F32)


def _silu(x):
    h = 0.5 * x
    return h + h * jnp.tanh(h)


def _sigmoid(x):
    return 1.0 / (1.0 + jnp.exp(-x))


def _softplus(x):
    return jnp.maximum(x, 0.0) + jnp.log1p(jnp.exp(-jnp.abs(x)))


def _rms(x, g):
    return x * lax.rsqrt(jnp.mean(x * x, axis=-1, keepdims=True) + EPS) * g


def _tile(n, pref):
    t = min(n, pref)
    while n % t:
        t -= 8
    assert t > 0 and t % 8 == 0, (n, pref)
    return t


def _params(sem):
    return pltpu.CompilerParams(dimension_semantics=sem, vmem_limit_bytes=VMEM_LIMIT)


def _mod_kernel(cond_ref, w_ref, b_ref, o_ref):
    s = _silu(cond_ref[...])
    o_ref[0] = _dot_f32(s, w_ref[0]) + b_ref[0]


def _modulation(cond_rows, mod_w, mod_b):
    depth, d, n = mod_w.shape
    r = cond_rows.shape[0]
    tn = _tile(n, 1536)
    return pl.pallas_call(
        _mod_kernel,
        grid=(depth, n // tn),
        in_specs=[pl.BlockSpec((r, d), lambda l, j: (0, 0)),
                  pl.BlockSpec((1, d, tn), lambda l, j: (l, 0, j)),
                  pl.BlockSpec((1, 1, tn), lambda l, j: (l, 0, j))],
        out_specs=pl.BlockSpec((1, r, tn), lambda l, j: (l, 0, j)),
        out_shape=jax.ShapeDtypeStruct((depth, r, n), F32),
        compiler_params=_params(("arbitrary", "arbitrary")),
        name="modulation",
    )(cond_rows, mod_w, mod_b.reshape(depth, 1, n))


def _add_kernel(x_ref, p_ref, o_ref):
    o_ref[0] = x_ref[0] + p_ref[...]


def _add_pos(x, pos):
    nb, l, d = x.shape
    tl = _tile(l, 512)
    return pl.pallas_call(
        _add_kernel,
        grid=(nb, l // tl),
        in_specs=[pl.BlockSpec((1, tl, d), lambda b, i: (b, i, 0)),
                  pl.BlockSpec((tl, d), lambda b, i: (i, 0))],
        out_specs=pl.BlockSpec((1, tl, d), lambda b, i: (b, i, 0)),
        out_shape=jax.ShapeDtypeStruct(x.shape, x.dtype),
        compiler_params=_params(("arbitrary", "arbitrary")),
        name="add_pos",
    )(x, pos)


def _grid_pos_embed(n_tok, dim):
    rows = n_tok // GRID_W
    rr, cc = np.meshgrid(np.arange(rows, dtype=np.float64), np.arange(GRID_W, dtype=np.float64), indexing="ij")
    quarter = dim // 4
    omega = 1.0 / (POS_BASE ** (np.arange(quarter, dtype=np.float64) / quarter))
    ang_r = rr.reshape(-1, 1) * omega
    ang_c = cc.reshape(-1, 1) * omega
    return np.concatenate([np.sin(ang_r), np.cos(ang_r), np.sin(ang_c), np.cos(ang_c)], axis=-1)


def _inproj_kernel(x_ref, mod_ref, g_ref, w_ref, o_ref):
    m = mod_ref[0]
    shift, scale = m[0:1, :], m[1:2, :]
    h = _rms(x_ref[...], g_ref[...]) * (1.0 + scale) + shift
    o_ref[...] = _dot(h, w_ref[...])


def _inproj(x, mod6, seq_rows, norm_g, w_in_r, layer):
    t, d = x.shape
    tm = _tile(seq_rows, 512)
    return pl.pallas_call(
        _inproj_kernel,
        grid=(t // tm,),
        in_specs=[pl.BlockSpec((tm, d), lambda i: (i, 0)),
                  pl.BlockSpec((1, 6, d), lambda i: ((i * tm) // seq_rows, 0, 0)),
                  pl.BlockSpec((1, d), lambda i: (0, 0)),
                  pl.BlockSpec((None, d, D_IN_PAD), lambda i: (layer, 0, 0))],
        out_specs=pl.BlockSpec((tm, D_IN_PAD), lambda i: (i, 0)),
        out_shape=jax.ShapeDtypeStruct((t, D_IN_PAD), F32),
        compiler_params=_params(("arbitrary",)),
        name="inproj",
    )(x, mod6, norm_g.reshape(1, d), w_in_r)


CONV_WIN = CHUNK + 16


def _conv_select(r0, w0):
    ti = lax.broadcasted_iota(jnp.int32, (2 * CHUNK, 2 * CONV_WIN), 0)
    tj = lax.broadcasted_iota(jnp.int32, (2 * CHUNK, 2 * CONV_WIN), 1)
    want = r0 + jnp.where(ti < CHUNK, ti - 1, ti - CHUNK + 1)
    have = w0 + jnp.where(tj < CONV_WIN, tj, tj - CONV_WIN)
    return jnp.where(want == have, 1.0, 0.0).astype(BF16)


def _conv3(load, r0, w0, sel, w):
    cur = load(pl.ds(r0, CHUNK))
    win = load(pl.ds(w0, CONV_WIN))
    hi = win.astype(BF16)
    lo = (win - hi.astype(F32)).astype(BF16)
    nb = jnp.dot(sel, jnp.concatenate([hi, lo], axis=0), preferred_element_type=F32)
    return w[0:1, :] * nb[:CHUNK] + w[1:2, :] * cur + w[2:3, :] * nb[CHUNK:]


def _mixer_kernel(*refs, seqs_per_step, batched_in, n_in, n_out, **static):
    ins, outs, scratch = refs[:n_in], refs[n_in:n_in + n_out], refs[n_in + n_out:]
    n_scr = len(scratch) // seqs_per_step
    seqs = []
    for s in range(seqs_per_step):
        one = lambda r, s=s: r.at[pl.ds(s, 1)]
        seqs.append(_mixer_phases([one(r) if i in batched_in else r for i, r in enumerate(ins)]
                                  + [one(r) for r in outs] + list(scratch[s * n_scr:(s + 1) * n_scr]), **static))
    nc, cpi = seqs[0].nc, seqs[0].cpi
    if nc <= STATIC_PREP_CHUNKS:
        for q in seqs:
            for z in range(nc):
                q.prep(z, 0)
            q.init()
        for q in seqs:
            for i in range(nc // cpi):
                q.delta_prep(i, 0)
        for z in range(nc):
            _scan_step(seqs, z)
        for q in seqs:
            for z in range(nc):
                q.finish(z, 0)
    else:
        for q in seqs:
            lax.fori_loop(0, nc, q.prep, 0, unroll=2)
            q.init()
            lax.fori_loop(0, nc // cpi, q.delta_prep, 0)

        def scan(z, carry):
            _scan_step(seqs, z)
            return carry

        lax.fori_loop(0, nc, scan, 0, unroll=4)
        for q in seqs:
            lax.fori_loop(0, nc, q.finish, 0, unroll=2)
    for q in seqs:
        q.emit()


def _scan_step(seqs, z):
    states = [q.scan_begin(z) for q in seqs]
    for stage in range(len(seqs[0].scan_stages)):
        for q, st in zip(seqs, states):
            q.scan_stages[stage](st)


def _mixer_phases(refs, seq_len, zero_init, state_layer, state_layers, state_aliased):
    refs = list(refs)
    proj, dnw, scw, ssw, lanev, ssdv = refs[:6]
    k = 6
    if not zero_init:
        sdn0, sssd0 = refs[k:k + 2]
        k += 2
    emit_state = state_layer is not None
    if state_aliased:
        k += 2
    ycat = refs[k]
    k += 1
    if emit_state:
        sdn_out, sssd_out = refs[k:k + 2]
        k += 2
    (qkv_s, xbc_s, bdup_s, cdup_s, gc_s, dpair_s, gcpair_s, tot_s, sp_s, beta_s, u_s, wq_s, kd_s, qk_s, o_s, y_s,
     st_s, hs_s) = refs[k:]

    nc = seq_len // CHUNK
    ri = lax.broadcasted_iota(jnp.int32, (CHUNK, CHUNK), 0)
    ci = lax.broadcasted_iota(jnp.int32, (CHUNK, CHUNK), 1)
    tril = (ri >= ci).astype(F32)
    r128 = lax.broadcasted_iota(jnp.int32, (LANES, LANES), 0)
    c128 = lax.broadcasted_iota(jnp.int32, (LANES, LANES), 1)
    eye128 = (r128 == c128).astype(F32)
    n_lvl = int(math.log2(CHUNK))
    alog = lanev[0:1, :]
    bias = lanev[1:2, :]
    pw = 2 * CHUNK
    ri2 = lax.broadcasted_iota(jnp.int32, (CHUNK, pw), 0)
    ci2 = lax.broadcasted_iota(jnp.int32, (CHUNK, pw), 1)
    lane_hi = ci2 >= CHUNK
    tj = ci2 & (CHUNK - 1)
    incl2 = (ri2 >= tj, ri2 <= tj)
    ahead = jnp.where(lane_hi, tj - ri2, ri2 - tj)
    incl_fb = ahead >= 0
    strict_fb = ahead > 0
    eye2 = (ri2 == tj).astype(F32)
    lvl2 = [((ri2 >> s) == (tj >> s)) & ((ri2 >> (s - 1)) != (tj >> (s - 1))) for s in range(1, n_lvl + 1)]

    def block_diag(x):
        return jnp.concatenate([jnp.where(lane_hi, 0.0, x), jnp.where(lane_hi, x, 0.0)], axis=0).astype(BF16)

    def chunk_rows(z):
        return pl.ds(z * CHUNK if isinstance(z, int) else pl.multiple_of(z * CHUNK, CHUNK), CHUNK)

    def prep(z, carry):
        r0 = z * CHUNK if isinstance(z, int) else pl.multiple_of(z * CHUNK, CHUNK)
        rows = pl.ds(r0, CHUNK)
        if isinstance(z, int):
            w0 = min(max(r0 - 8, 0), seq_len - CONV_WIN)
        else:
            w0 = pl.multiple_of(jnp.clip(r0 - 8, 0, seq_len - CONV_WIN), 8)
        sel = _conv_select(r0, w0)
        cw = 2 * LANES
        for jj in range(3 * A_W // cw):
            c0 = jj * cw
            a2 = _silu(_conv3(lambda rs: proj[0, rs, c0:c0 + cw], r0, w0, sel, dnw[:, c0:c0 + cw]))
            for half in range(cw // LANES):
                a = a2[:, half * LANES:(half + 1) * LANES]
                j = jj * (cw // LANES) + half
                if j < 2 * H_A:
                    a = a * lax.rsqrt(jnp.sum(a * a, axis=-1, keepdims=True) + EPS)
                if j < H_A:
                    a = a * (DK_A ** -0.5)
                qkv_s[rows, j * LANES:(j + 1) * LANES] = a
        for jj in range(B_W // cw):
            c0 = jj * cw
            cv = _conv3(lambda rs: proj[0, rs, COL_SCC + c0:COL_SCC + c0 + cw]
                        * proj[0, rs, COL_SCH + c0:COL_SCH + c0 + cw], r0, w0, sel, scw[:, c0:c0 + cw])
            yb = proj[0, rows, COL_SCB + c0:COL_SCB + c0 + cw] * cv
            ycat[0, rows, A_W + c0:A_W + c0 + cw] = yb.astype(ycat.dtype)
        for jj in range(XBC_W // cw):
            c1 = jj * cw
            a2 = _silu(_conv3(lambda rs: proj[0, rs, COL_XBC + c1:COL_XBC + c1 + cw], r0, w0, sel,
                              ssw[:, c1:c1 + cw]))
            if c1 < C_W:
                xbc_s[rows, c1:c1 + cw] = a2
                continue
            for half, dup_s in enumerate((bdup_s, cdup_s)):
                a = a2[:, half * LANES:(half + 1) * LANES]
                swapped = pltpu.roll(a, N_C, 1)
                lo = lax.broadcasted_iota(jnp.int32, a.shape, 1) < N_C
                dup_s[rows, 0:LANES] = jnp.where(lo, a, swapped)
                dup_s[rows, LANES:2 * LANES] = jnp.where(lo, swapped, a)
        sm = proj[0, rows, COL_SMALL:COL_SMALL + LANES]
        sp = _softplus(sm + bias)
        g = -jnp.exp(alog) * sp
        pre = _dot_f32(tril, g)
        tot = pre[CHUNK - 1:CHUNK, :]
        suf = tot - pre + g
        sp_s[rows, :] = sp
        beta_s[rows, :] = _sigmoid(sm)
        gc_s[0, rows, :] = pre
        gc_s[1, rows, :] = suf
        grs = [_dot_nt_f32(eye128, gcd) for gcd in (pre, suf)]
        for d, gr in enumerate(grs):
            for g in range(G_C):
                ln = LANE_DT + d * H_C + g * HPG
                gcpair_s[d, z, g:g + 1, :] = jnp.concatenate([gr[ln:ln + 1, :], gr[ln + 1:ln + 2, :]], axis=1)
        for h in range(H_A):
            lf, lb = LANE_ALPHA + h, LANE_ALPHA + H_A + h
            dpair_s[z, h:h + 1, :] = jnp.concatenate([grs[0][lf:lf + 1, :], grs[1][lb:lb + 1, :]], axis=1)
        tot_s[z] = jnp.broadcast_to(tot, (8, LANES))
        o_s[rows, :] = jnp.zeros((CHUNK, A_W), F32)
        y_s[rows, :] = jnp.zeros((CHUNK, C_W), F32)
        return carry

    def init():
        for d in range(2):
            for h in range(H_A):
                st_s[d * H_A + h] = jnp.zeros((DK_A, DV_A), F32) if zero_init else sdn0[0, d, h]
            for g in range(G_C):
                if zero_init:
                    hs_s[d, g] = jnp.zeros((HPG * P_C, HPG * N_C), F32)
                else:
                    zero = jnp.zeros((P_C, N_C), F32)
                    hs_s[d, g] = jnp.concatenate(
                        [jnp.concatenate([sssd0[0, d, g * HPG], zero], axis=1),
                         jnp.concatenate([zero, sssd0[0, d, g * HPG + 1]], axis=1)], axis=0)

    cpi = 4 if nc % 4 == 0 else 2

    def delta_prep(i, carry):
        units = []
        for zz in range(cpi):
            z = cpi * i + zz
            rows = chunk_rows(z)
            tot = tot_s[z][0:1, :]
            beta = beta_s[rows, :]
            gcs = [gc_s[d, rows, :] for d in range(2)]
            for h in range(H_A):
                q_h = qkv_s[rows, COL_Q + h * DK_A:COL_Q + (h + 1) * DK_A]
                k_h = qkv_s[rows, COL_K + h * DK_A:COL_K + (h + 1) * DK_A]
                v_h = qkv_s[rows, COL_V + h * DV_A:COL_V + (h + 1) * DV_A]
                units.append(dict(z=z, h=h, q=q_h, k=k_h, v=v_h, tot=tot, beta=beta, gcs=gcs))
        qkk = [_dot_nt(jnp.concatenate([p["q"], p["k"]], axis=0), jnp.concatenate([p["k"], p["k"]], axis=0))
               for p in units]
        ms, rhss = [], []
        for p, qk_kk in zip(units, qkk):
            z, h = p["z"], p["h"]
            lf, lb = LANE_ALPHA + h, LANE_ALPHA + H_A + h
            bf, bb = LANE_BETA + h, LANE_BETA + H_A + h
            a_pair = jnp.where(lane_hi, p["gcs"][1][:, lb:lb + 1], p["gcs"][0][:, lf:lf + 1])
            b_pair = jnp.where(lane_hi, p["beta"][:, bb:bb + 1], p["beta"][:, bf:bf + 1])
            decay = jnp.exp(jnp.where(incl_fb, a_pair - dpair_s[z, h:h + 1, :], -1e30))
            ms.append(jnp.where(strict_fb, qk_kk[CHUNK:] * b_pair * decay, 0.0))
            qk_s[z, h] = (qk_kk[:CHUNK] * decay).astype(qk_s.dtype)
            rhs_d = []
            for d, (ln, bl) in enumerate(((lf, bf), (lb, bb))):
                a_col = p["gcs"][d][:, ln:ln + 1]
                b_col = p["beta"][:, bl:bl + 1]
                eg = jnp.exp(a_col)
                rhs_d.append(jnp.concatenate([p["v"] * b_col, p["k"] * (b_col * eg)], axis=1))
                kd_s[d, z, h] = (p["k"] * jnp.exp(p["tot"][:, ln:ln + 1] - a_col)).astype(kd_s.dtype)
                wq_s[d, z, h, CHUNK:, :] = (p["q"] * eg).astype(wq_s.dtype)
            zero = jnp.zeros_like(rhs_d[0])
            rhss.append(jnp.concatenate([jnp.concatenate([rhs_d[0], zero], axis=1),
                                         jnp.concatenate([zero, rhs_d[1]], axis=1)], axis=0))
        t_inv = [eye2 - jnp.where(lvl2[0], m, 0.0) for m in ms]
        for s in range(1, n_lvl):
            x = [_dot(jnp.where(lvl2[s], m, 0.0), block_diag(t)) for m, t in zip(ms, t_inv)]
            t_inv = [t - _dot(t, block_diag(xx)) for t, xx in zip(t_inv, x)]
        uw = [_dot(t, rhs) for t, rhs in zip(t_inv, rhss)]
        for p, r in zip(units, uw):
            z, h = p["z"], p["h"]
            for d in range(2):
                c0 = d * (DV_A + DK_A)
                u_s[d, z, h] = r[:, c0:c0 + DV_A]
                wq_s[d, z, h, :CHUNK, :] = r[:, c0 + DV_A:c0 + DV_A + DK_A].astype(wq_s.dtype)
        return carry

    rb =lax.broadcasted_iota(jnp.int32, (pw, pw), 0) >= P_C
    cbk = lax.broadcasted_iota(jnp.int32, (pw, pw), 1) >= N_C
    diag_blk = rb == cbk

    def scan_begin(z):
        dus = [(d, (z if d == 0 else nc - 1 - z), h) for d in range(2) for h in range(H_A)]
        s_prev = [st_s[d * H_A + h] for d, _, h in dus]
        us = []
        for d in range(2):
            zc = z if d == 0 else nc - 1 - z
            rows = chunk_rows(zc)
            sp = sp_s[rows, :]
            tot = tot_s[zc][0:1, :]
            gc = gc_s[d, rows, :]
            for g in range(G_C):
                ln = LANE_DT + d * H_C + g * HPG
                gsl = slice(g * pw, (g + 1) * pw)
                a_pair = jnp.where(lane_hi, gc[:, ln + 1:ln + 2], gc[:, ln:ln + 1])
                t_pair = jnp.where(lane_hi, tot[:, ln + 1:ln + 2], tot[:, ln:ln + 1])
                lmat = jnp.exp(jnp.where(incl2[d], a_pair - gcpair_s[d, zc, g:g + 1, :], -1e30))
                xdt = xbc_s[rows, gsl] * jnp.where(lane_hi, sp[:, ln + 1:ln + 2], sp[:, ln:ln + 1])
                b_dup = bdup_s[rows, gsl]
                c_dup = cdup_s[rows, gsl]
                us.append(dict(d=d, g=g, rows=rows, gsl=gsl, lmat=lmat, xdt=xdt, c_dup=c_dup,
                               c_lo=jnp.where(lane_hi, 0.0, c_dup), b_st=jnp.concatenate([b_dup, b_dup], axis=0),
                               bdec=b_dup * jnp.exp(t_pair - a_pair), ea=jnp.exp(a_pair),
                               dec=jnp.where(rb, jnp.exp(tot[:, ln + 1:ln + 2]), jnp.exp(tot[:, ln:ln + 1])),
                               x_bd=jnp.concatenate([jnp.where(lane_hi, 0.0, xdt), jnp.where(lane_hi, xdt, 0.0)],
                                                    axis=0),
                               h_prev=hs_s[d, g]))
        return dict(dus=dus, s_prev=s_prev, us=us)

    def scan_delta_1(t):
        t["ws_qs"] = [_dot(wq_s[idx], s) for idx, s in zip(t["dus"], t["s_prev"])]

    def scan_ssd_1(t):
        t["st"] = [_dot_tn(p["xdt"], p["bdec"]) for p in t["us"]]
        t["cb"] = [_dot_nt(p["c_lo"], p["b_st"]) for p in t["us"]]
        t["y_off"] = [_dot_nt(p["c_dup"], p["h_prev"]) for p in t["us"]]

    def scan_delta_2(t):
        v_new = [u_s[idx] - r[:CHUNK] for idx, r in zip(t["dus"], t["ws_qs"])]
        zero = jnp.zeros((CHUNK, DV_A), F32)
        t["o_in"] = [_dot(qk_s[zc, h], jnp.concatenate([v, zero] if d == 0 else [zero, v], axis=0))
                     for (d, zc, h), v in zip(t["dus"], v_new)]
        t["s_add"] = [_dot_tn(kd_s[idx], v) for idx, v in zip(t["dus"], v_new)]

    def scan_ssd_2(t):
        t["y_diag"] = [_dot(cbd * p["lmat"], p["x_bd"]) for p, cbd in zip(t["us"], t["cb"])]

    def scan_store(t):
        for p, yd, yo, s in zip(t["us"], t["y_diag"], t["y_off"], t["st"]):
            y_s[p["rows"], p["gsl"]] = y_s[p["rows"], p["gsl"]] + yd + yo * p["ea"]
            hs_s[p["d"], p["g"]] = p["h_prev"] * p["dec"] + jnp.where(diag_blk, s, 0.0)
        for (d, zc, h), r, oi, sa, s in zip(t["dus"], t["ws_qs"], t["o_in"], t["s_add"], t["s_prev"]):
            ln = LANE_ALPHA + d * H_A + h
            cs = slice(h * DV_A, (h + 1) * DV_A)
            o_s[chunk_rows(zc), cs] = o_s[chunk_rows(zc), cs] + r[CHUNK:] + oi
            st_s[d * H_A + h] = s * jnp.exp(tot_s[zc][0:1, ln:ln + 1]) + sa


    def finish(z, carry):
        rows = chunk_rows(z)
        for h in range(H_A):
            cs = slice(h * DV_A, (h + 1) * DV_A)
            o = _rms(o_s[rows, cs], lanev[2:3, :])
            o = o * _silu(proj[0, rows, COL_GATE + h * DV_A:COL_GATE + (h + 1) * DV_A])
            ycat[0, rows, cs] = o.astype(ycat.dtype)
        y = y_s[rows, :] + ssdv[0:1, :] * xbc_s[rows, 0:C_W]
        y = _rms(y * _silu(proj[0, rows, COL_Z:COL_Z + C_W]), ssdv[1:2, :])
        ycat[0, rows, A_W + B_W:] = y.astype(ycat.dtype)
        return carry

    def emit():
        if not emit_state:
            return
        slots = (None,) if state_aliased else range(state_layers)
        for slot in slots:
            dn_slot = sdn_out.at[0] if slot is None else sdn_out.at[0, slot]
            ssd_slot = sssd_out.at[0] if slot is None else sssd_out.at[0, slot]
            mine = slot is None or slot == state_layer
            for d in range(2):
                for h in range(H_A):
                    dn_slot[d, h] = st_s[d * H_A + h] if mine else jnp.zeros((DK_A, DV_A), F32)
                for h in range(H_C):
                    k0 = (h % HPG) * P_C
                    ssd_slot[d, h] = (hs_s[d, h // HPG][k0:k0 + P_C, k0:k0 + N_C] if mine
                                      else jnp.zeros((P_C, N_C), F32))

    return types.SimpleNamespace(
        nc=nc, cpi=cpi, prep=prep, init=init, delta_prep=delta_prep, scan_begin=scan_begin,
        scan_stages=(scan_delta_1, scan_ssd_1, scan_delta_2, scan_ssd_2, scan_store), finish=finish, emit=emit)


def _mixer(proj, dn_conv_w, sc_conv_w, ssd_conv_w, lanev, ssdv, s_dn0, s_ssd0, state_out):
    nb, seq_len, _ = proj.shape
    nc = seq_len // CHUNK
    assert nc % 2 == 0
    zero_init = s_dn0 is None
    spg = 2 if nc <= STATIC_PREP_CHUNKS and nb % 2 == 0 else 1
    full = lambda a: pl.BlockSpec(a.shape, lambda b: (0,) * a.ndim)
    args = [proj, dn_conv_w, sc_conv_w, ssd_conv_w, lanev, ssdv]
    batched_in = [0]
    in_specs = [pl.BlockSpec((spg, seq_len, D_IN_PAD), lambda b: (b, 0, 0), pipeline_mode=pl.Buffered(1))
                if seq_len > 512 else pl.BlockSpec((spg, seq_len, D_IN_PAD), lambda b: (b, 0, 0)),
                full(dn_conv_w), full(sc_conv_w), full(ssd_conv_w), full(lanev), full(ssdv)]
    if not zero_init:
        batched_in += [len(args), len(args) + 1]
        args += [s_dn0, s_ssd0]
        in_specs += [pl.BlockSpec((spg, 2, H_A, DK_A, DV_A), lambda b: (b, 0, 0, 0, 0)),
                     pl.BlockSpec((spg, 2, H_C, P_C, N_C), lambda b: (b, 0, 0, 0, 0))]
    out_shape = [jax.ShapeDtypeStruct((nb, seq_len, D_MODEL), BF16)]
    out_specs = [pl.BlockSpec((spg, seq_len, D_MODEL), lambda b: (b, 0, 0))]
    aliases = {}
    layer, depth, prev = state_out if state_out is not None else (None, None, None)
    if state_out is not None:
        out_shape += [jax.ShapeDtypeStruct((nb, depth, 2, H_A, DK_A, DV_A), F32),
                      jax.ShapeDtypeStruct((nb, depth, 2, H_C, P_C, N_C), F32)]
        if prev is None:
            out_specs += [pl.BlockSpec((spg, depth, 2, H_A, DK_A, DV_A), lambda b: (b, 0, 0, 0, 0, 0)),
                          pl.BlockSpec((spg, depth, 2, H_C, P_C, N_C), lambda b: (b, 0, 0, 0, 0, 0))]
        else:
            aliases = {len(args): 1, len(args) + 1: 2}
            args += list(prev)
            in_specs += [pl.BlockSpec(memory_space=pl.ANY), pl.BlockSpec(memory_space=pl.ANY)]
            out_specs += [pl.BlockSpec((spg, None, 2, H_A, DK_A, DV_A), lambda b: (b, layer, 0, 0, 0, 0)),
                          pl.BlockSpec((spg, None, 2, H_C, P_C, N_C), lambda b: (b, layer, 0, 0, 0, 0))]
    scratch = [pltpu.VMEM((seq_len, 3 * A_W), F32),
               pltpu.VMEM((seq_len, C_W), F32),
               pltpu.VMEM((seq_len, G_C * LANES), F32),
               pltpu.VMEM((seq_len, G_C * LANES), F32),
               pltpu.VMEM((2, seq_len, LANES), F32),
               pltpu.VMEM((nc, 8, LANES), F32),
               pltpu.VMEM((2, nc, 8, LANES), F32),
               pltpu.VMEM((nc, 8, LANES), F32),
               pltpu.VMEM((seq_len, LANES), F32),
               pltpu.VMEM((seq_len, LANES), F32),
               pltpu.VMEM((2, nc, H_A, CHUNK, DV_A), F32),
               pltpu.VMEM((2, nc, H_A, 2 * CHUNK, DK_A), BF16),
               pltpu.VMEM((2, nc, H_A, CHUNK, DK_A), BF16),
               pltpu.VMEM((nc, H_A, CHUNK, 2 * CHUNK), BF16),
               pltpu.VMEM((seq_len, A_W), F32),
               pltpu.VMEM((seq_len, C_W), F32),
               pltpu.VMEM((2 * H_A, DK_A, DV_A), F32),
               pltpu.VMEM((2, G_C, HPG * P_C, HPG * N_C), F32)]
    return pl.pallas_call(
        functools.partial(_mixer_kernel, seqs_per_step=spg, batched_in=tuple(batched_in), n_in=len(args),
                          n_out=len(out_shape), seq_len=seq_len, zero_init=zero_init, state_layer=layer,
                          state_layers=depth, state_aliased=prev is not None),
        grid=(nb // spg,),
        in_specs=in_specs,
        out_specs=out_specs,
        out_shape=out_shape,
        input_output_aliases=aliases,
        scratch_shapes=scratch * spg,
        compiler_params=_params(("arbitrary",)),
        name="mixer",
    )(*args)


def _top2_sum(a, b, c, d):
    hi1, lo1 = jnp.maximum(a, b), jnp.minimum(a, b)
    hi2, lo2 = jnp.maximum(c, d), jnp.minimum(c, d)
    return jnp.maximum(hi1, hi2) + jnp.maximum(jnp.minimum(hi1, hi2), jnp.maximum(lo1, lo2))


def _outproj_kernel(y_ref, x_ref, mod_ref, w_ref, g_ref, rw_ref, rb_ref, x1_ref, h2_ref, comb_ref, wb_ref):
    @pl.when(pl.program_id(0) == 0)
    def _():
        wb_ref[...] = w_ref[...].astype(wb_ref.dtype)

    m = mod_ref[0]
    gate1, shift2, scale2 = m[2:3, :], m[3:4, :], m[4:5, :]
    x1 = x_ref[...] + gate1 * _dot(y_ref[...], wb_ref[...])
    x1_ref[...] = x1
    h2 = _rms(x1, g_ref[...]) * (1.0 + scale2) + shift2
    h2_ref[...] = h2.astype(h2_ref.dtype)

    h_hi = h2.astype(BF16)
    h_lo = (h2 - h_hi.astype(F32)).astype(BF16)
    rw_hi, rw_lo = rw_ref[0], rw_ref[1]
    hl = jnp.dot(jnp.concatenate([h_hi, h_lo], axis=0), rw_hi, preferred_element_type=F32)
    tm = h2.shape[0]
    logits = hl[:tm] + hl[tm:] + jnp.dot(h_hi, rw_lo, preferred_element_type=F32)
    scores = _sigmoid(logits.T[:N_EXPERTS, :])
    biased = scores + rb_ref[...]
    sc = [scores[e:e + 1, :] for e in range(N_EXPERTS)]
    bi = [biased[e:e + 1, :] for e in range(N_EXPERTS)]
    gs = [_top2_sum(*bi[EPG * g:EPG * (g + 1)]) for g in range(N_GROUPS)]
    gmax = functools.reduce(jnp.maximum, gs)
    first = []
    taken = None
    for g in range(N_GROUPS):
        hit = gs[g] == gmax
        if taken is None:
            first.append(hit)
            taken = hit
        else:
            first.append(hit & jnp.logical_not(taken))
            taken = taken | hit

    def pick(vals, j):
        out = vals[EPG * (N_GROUPS - 1) + j]
        for g in range(N_GROUPS - 2, -1, -1):
            out = jnp.where(first[g], vals[EPG * g + j], out)
        return out

    ib = [pick(bi, j) for j in range(EPG)]
    isc = [pick(sc, j) for j in range(EPG)]
    sel = []
    for j in range(EPG):
        cnt = jnp.zeros_like(ib[j])
        for i in range(EPG):
            if i == j:
                continue
            ahead = (ib[i] > ib[j]) | ((ib[i] == ib[j]) if i < j else False)
            cnt = cnt + jnp.where(ahead, 1.0, 0.0)
        sel.append(cnt < 2.0)
    wj = [jnp.where(sel[j], isc[j], 0.0) for j in range(EPG)]
    denom = functools.reduce(lambda a, b: a + b, wj)
    for g in range(N_GROUPS):
        for j in range(EPG):
            comb_ref[EPG * g + j:EPG * g + j + 1, :] = jnp.where(first[g], wj[j] / denom, 0.0)


def _outproj(ycat, x, mod6, seq_rows, w_out, layer, norm_g, router_w_pad, router_b):
    t, d = x.shape
    tm = _tile(seq_rows, 512)
    return pl.pallas_call(
        _outproj_kernel,
        grid=(t // tm,),
        in_specs=[pl.BlockSpec((tm, d), lambda i: (i, 0)),
                  pl.BlockSpec((tm, d), lambda i: (i, 0)),
                  pl.BlockSpec((1, 6, d), lambda i: ((i * tm) // seq_rows, 0, 0)),
                  pl.BlockSpec((None, d, d), lambda i: (layer, 0, 0)),
                  pl.BlockSpec((1, d), lambda i: (0, 0)),
                  pl.BlockSpec((2, d, LANES), lambda i: (0, 0, 0)),
                  pl.BlockSpec((N_EXPERTS, 1), lambda i: (0, 0))],
        out_specs=[pl.BlockSpec((tm, d), lambda i: (i, 0)),
                   pl.BlockSpec((tm, d), lambda i: (i, 0)),
                   pl.BlockSpec((N_EXPERTS, tm), lambda i: (0, i))],
        out_shape=[jax.ShapeDtypeStruct((t, d), F32),
                   jax.ShapeDtypeStruct((t, d), BF16),
                   jax.ShapeDtypeStruct((N_EXPERTS, t), F32)],
        scratch_shapes=[pltpu.VMEM((d, d), BF16)],
        compiler_params=_params(("arbitrary",)),
        name="outproj_route",
    )(ycat, x, mod6, w_out, norm_g.reshape(1, d), router_w_pad, router_b.reshape(N_EXPERTS, 1))


def _moe_kernel(h_ref, comb_ref, wg_ref, wu_ref, wd_ref, x1_ref, mod_ref, fg_ref, o_ref, *, final):
    g = pl.program_id(1)

    @pl.when(g == 0)
    def _():
        o_ref[...] = jnp.zeros_like(o_ref)

    h = h_ref[...]
    comb = comb_ref[...]
    acts = [(_silu(_dot(h, wg_ref[j])) * _dot(h, wu_ref[j]) * comb[:, j:j + 1]).astype(BF16) for j in range(EPG)]
    o_ref[...] += _dot(jnp.concatenate(acts, axis=1), wd_ref[...])

    @pl.when(g == pl.num_programs(1) - 1)
    def _():
        x2 = x1_ref[...] + mod_ref[0][5:6, :] * o_ref[...]
        o_ref[...] = _rms(x2, fg_ref[...]) if final else x2


def _moe(h2, comb, w_gate, w_up, w_down, layer, x1, mod6, seq_rows, final_g, final):
    t, d = x1.shape
    tm = _tile(seq_rows, 1024)
    return pl.pallas_call(
        functools.partial(_moe_kernel, final=final),
        grid=(t // tm, N_GROUPS),
        in_specs=[pl.BlockSpec((tm, d), lambda i, g: (i, 0)),
                  pl.BlockSpec((None, tm, EPG), lambda i, g: (g, i, 0)),
                  pl.BlockSpec((None, EPG, d, D_EXPERT), lambda i, g: (layer, g, 0, 0)),
                  pl.BlockSpec((None, EPG, d, D_EXPERT), lambda i, g: (layer, g, 0, 0)),
                  pl.BlockSpec((None, None, EPG * D_EXPERT, d), lambda i, g: (layer, g, 0, 0)),
                  pl.BlockSpec((tm, d), lambda i, g: (i, 0)),
                  pl.BlockSpec((1, 6, d), lambda i, g: ((i * tm) // seq_rows, 0, 0)),
                  pl.BlockSpec((1, d), lambda i, g: (0, 0))],
        out_specs=pl.BlockSpec((tm, d), lambda i, g: (i, 0)),
        out_shape=jax.ShapeDtypeStruct((t, d), F32),
        compiler_params=_params(("arbitrary", "arbitrary")),
        name="experts",
    )(h2, comb, w_gate, w_up, w_down, x1, mod6, final_g.reshape(1, d))


def _lane_row(*pieces):
    row = jnp.zeros((LANES,), F32)
    for lane, vals in pieces:
        row = lax.dynamic_update_slice(row, vals.reshape(-1).astype(F32), (lane,))
    return row


def kernel(x_prompt, x_sample, state_delta, state_ssd, c, c_ctx, mod_w, mod_b, norm1_g, norm2_g, w_in, w_out,
           dn_conv_w, dn_a_log, dn_dt_bias, dn_norm_g, sc_conv_w, ssd_conv_w, ssd_a_log, ssd_dt_bias, ssd_d,
           ssd_norm_g, router_w, router_b, exp_w_gate, exp_w_up, exp_w_down, final_norm_g):
    depth = mod_w.shape[0]
    n_ctx, seq, d = x_prompt.shape
    n_dec, dec_seq, _ = x_sample.shape
    assert seq % CHUNK == 0 and dec_seq % CHUNK == 0 and d == D_MODEL

    n_rows = -(-(1 + n_dec) // 8) * 8
    cond_rows = jnp.zeros((n_rows, d), F32).at[0].set(c_ctx).at[1:1 + n_dec].set(c)
    mod = _modulation(cond_rows, mod_w, mod_b).reshape(depth, n_rows, 6, d)

    w_in_r = jnp.zeros((depth, d, D_IN_PAD), BF16)
    for dst, lo, hi in ((0, 0, SRC_SMALL_A), (COL_SCH, SRC_SCH, SRC_SMALL_B), (COL_SMALL, SRC_SMALL_A, SRC_SCH),
                        (COL_SMALL + SRC_SCH - SRC_SMALL_A, SRC_SMALL_B, D_IN)):
        w_in_r = lax.dynamic_update_slice(w_in_r, w_in[:, :, lo:hi].astype(BF16), (0, 0, dst))
    w_down_g = exp_w_down.reshape(depth, N_GROUPS, EPG * D_EXPERT, d)
    rw = jnp.pad(router_w.astype(F32), ((0, 0), (0, LANES - N_EXPERTS)))
    rw_hi = rw.astype(BF16)
    router_w_pad = jnp.stack([rw_hi, (rw - rw_hi.astype(F32)).astype(BF16)])

    xp = x_prompt.reshape(n_ctx * seq, d)
    pos = jnp.asarray(_grid_pos_embed(dec_seq, d), dtype=x_sample.dtype)
    xs = _add_pos(x_sample, pos).reshape(n_dec * dec_seq, d)

    states = None
    for l in range(depth):
        lanev = jnp.zeros((8, LANES), F32)
        lanev = lanev.at[0].set(_lane_row((LANE_ALPHA, dn_a_log[l]), (LANE_DT, ssd_a_log[l])))
        lanev = lanev.at[1].set(_lane_row((LANE_ALPHA, dn_dt_bias[l]), (LANE_DT, ssd_dt_bias[l])))
        lanev = lanev.at[2].set(dn_norm_g[l].astype(F32))
        ssdv = jnp.zeros((8, C_W), F32).at[0].set(jnp.repeat(ssd_d[l].astype(F32), P_C)).at[1].set(ssd_norm_g[l])
        final = l == depth - 1

        def block(x, mod6, nb, seq_len, seq_rows, s_dn0, s_ssd0, state_out):
            proj = _inproj(x, mod6, seq_rows, norm1_g[l], w_in_r, l).reshape(nb, seq_len, D_IN_PAD)
            outs = _mixer(proj, dn_conv_w[l], sc_conv_w[l], ssd_conv_w[l], lanev, ssdv, s_dn0, s_ssd0, state_out)
            ycat = outs[0].reshape(nb * seq_len, d)
            x1, h2, comb_t = _outproj(ycat, x, mod6, seq_rows, w_out, l, norm2_g[l], router_w_pad, router_b)
            comb = comb_t.reshape(N_GROUPS, EPG, -1).transpose(0, 2, 1)
            x2 = _moe(h2, comb, exp_w_gate, exp_w_up, w_down_g, l, x1, mod6, seq_rows, final_norm_g, final)
            return x2, outs[1:]

        xp, states = block(xp, mod[l, 0:1], n_ctx, seq, n_ctx * seq, None, None, (l, depth, states))
        xs, _ = block(xs, mod[l, 1:1 + n_dec], n_dec, dec_seq, dec_seq,
                      state_delta[:, l].astype(F32), state_ssd[:, l].astype(F32), None)

    return (xp.reshape(n_ctx, seq, d), xs.reshape(n_dec, dec_seq, d), states[0], states[1])
```

```python
import functools
import math
import types

import jax
import jax.numpy as jnp
import numpy as np
from jax import lax
from jax.experimental import pallas as pl
from jax.experimental.pallas import tpu as pltpu

F32 = jnp.float32
BF16 = jnp.bfloat16

D_MODEL = 1024
GRID_W = 64
POS_BASE = 10000.0
H_A, DK_A, DV_A = 4, 128, 128
A_W = H_A * DV_A
H_C, P_C, N_C, G_C = 4, 64, 64, 2
HPG = H_C // G_C
C_W = H_C * P_C
B_W = D_MODEL - A_W - C_W
XBC_W = C_W + 2 * G_C * N_C
CHUNK = 64
N_EXPERTS = 16
N_GROUPS = 4
EPG = N_EXPERTS // N_GROUPS
D_EXPERT = 256
EPS = 1e-6
LANES = 128

COL_Q, COL_K, COL_V, COL_GATE = 0, A_W, 2 * A_W, 3 * A_W
COL_SCH = 4 * A_W
COL_SCB = COL_SCH + B_W
COL_SCC = COL_SCB + B_W
COL_Z = COL_SCC + B_W
COL_XBC = COL_Z + C_W
COL_SMALL = COL_XBC + XBC_W
D_IN_PAD = COL_SMALL + LANES
LANE_BETA, LANE_ALPHA, LANE_DT = 0, 2 * H_A, 4 * H_A
SRC_SMALL_A = 4 * A_W
SRC_SCH = SRC_SMALL_A + 4 * H_A
SRC_SMALL_B = SRC_SCH + 3 * B_W + C_W + XBC_W
D_IN = SRC_SMALL_B + 2 * H_C

assert HPG == 2 and P_C == N_C == CHUNK and HPG * P_C == LANES and G_C * N_C == LANES and DK_A == DV_A == LANES

VMEM_LIMIT = 56 * 1024 * 1024
STATIC_PREP_CHUNKS = 4


def _dot(a, b):
    return jnp.dot(a.astype(BF16), b.astype(BF16), preferred_element_type=F32)


def _dot_nt(a, b):
    return lax.dot_general(a.astype(BF16), b.astype(BF16), (((1,), (1,)), ((), ())), preferred_element_type=F32)


def _dot_tn(a, b):
    return lax.dot_general(a.astype(BF16), b.astype(BF16), (((0,), (0,)), ((), ())), preferred_element_type=F32)


def _dot_f32(a, b):
    return jnp.dot(a, b, precision=lax.Precision.HIGHEST, preferred_element_type=F32)


def _dot_nt_f32(a, b):
    return lax.dot_general(a, b, (((1,), (1,)), ((), ())), precision=lax.Precision.HIGHEST,
                           preferred_element_type=F32)


def _silu(x):
    h = 0.5 * x
    return h + h * jnp.tanh(h)


def _sigmoid(x):
    return 1.0 / (1.0 + jnp.exp(-x))


def _softplus(x):
    return jnp.maximum(x, 0.0) + jnp.log1p(jnp.exp(-jnp.abs(x)))


def _rms(x, g):
    return x * lax.rsqrt(jnp.mean(x * x, axis=-1, keepdims=True) + EPS) * g


def _tile(n, pref):
    t = min(n, pref)
    while n % t:
        t -= 8
    assert t > 0 and t % 8 == 0, (n, pref)
    return t


def _params(sem):
    return pltpu.CompilerParams(dimension_semantics=sem, vmem_limit_bytes=VMEM_LIMIT)


def _mod_kernel(cond_ref, w_ref, b_ref, o_ref):
    s = _silu(cond_ref[...])
    o_ref[0] = _dot_f32(s, w_ref[0]) + b_ref[0]


def _modulation(cond_rows, mod_w, mod_b):
    depth, d, n = mod_w.shape
    r = cond_rows.shape[0]
    tn = _tile(n, 1536)
    return pl.pallas_call(
        _mod_kernel,
        grid=(depth, n // tn),
        in_specs=[pl.BlockSpec((r, d), lambda l, j: (0, 0)),
                  pl.BlockSpec((1, d, tn), lambda l, j: (l, 0, j)),
                  pl.BlockSpec((1, 1, tn), lambda l, j: (l, 0, j))],
        out_specs=pl.BlockSpec((1, r, tn), lambda l, j: (l, 0, j)),
        out_shape=jax.ShapeDtypeStruct((depth, r, n), F32),
        compiler_params=_params(("arbitrary", "arbitrary")),
        name="modulation",
    )(cond_rows, mod_w, mod_b.reshape(depth, 1, n))


def _add_kernel(x_ref, p_ref, o_ref):
    o_ref[0] = x_ref[0] + p_ref[...]


def _add_pos(x, pos):
    nb, l, d = x.shape
    tl = _tile(l, 512)
    return pl.pallas_call(
        _add_kernel,
        grid=(nb, l // tl),
        in_specs=[pl.BlockSpec((1, tl, d), lambda b, i: (b, i, 0)),
                  pl.BlockSpec((tl, d), lambda b, i: (i, 0))],
        out_specs=pl.BlockSpec((1, tl, d), lambda b, i: (b, i, 0)),
        out_shape=jax.ShapeDtypeStruct(x.shape, x.dtype),
        compiler_params=_params(("arbitrary", "arbitrary")),
        name="add_pos",
    )(x, pos)


def _grid_pos_embed(n_tok, dim):
    rows = n_tok // GRID_W
    rr, cc = np.meshgrid(np.arange(rows, dtype=np.float64), np.arange(GRID_W, dtype=np.float64), indexing="ij")
    quarter = dim // 4
    omega = 1.0 / (POS_BASE ** (np.arange(quarter, dtype=np.float64) / quarter))
    ang_r = rr.reshape(-1, 1) * omega
    ang_c = cc.reshape(-1, 1) * omega
    return np.concatenate([np.sin(ang_r), np.cos(ang_r), np.sin(ang_c), np.cos(ang_c)], axis=-1)


def _inproj_kernel(x_ref, mod_ref, g_ref, w_ref, o_ref):
    m = mod_ref[0]
    shift, scale = m[0:1, :], m[1:2, :]
    h = _rms(x_ref[...], g_ref[...]) * (1.0 + scale) + shift
    o_ref[...] = _dot(h, w_ref[...])


def _inproj(x, mod6, seq_rows, norm_g, w_in_r, layer):
    t, d = x.shape
    tm = _tile(seq_rows, 512)
    return pl.pallas_call(
        _inproj_kernel,
        grid=(t // tm,),
        in_specs=[pl.BlockSpec((tm, d), lambda i: (i, 0)),
                  pl.BlockSpec((1, 6, d), lambda i: ((i * tm) // seq_rows, 0, 0)),
                  pl.BlockSpec((1, d), lambda i: (0, 0)),
                  pl.BlockSpec((None, d, D_IN_PAD), lambda i: (layer, 0, 0))],
        out_specs=pl.BlockSpec((tm, D_IN_PAD), lambda i: (i, 0)),
        out_shape=jax.ShapeDtypeStruct((t, D_IN_PAD), F32),
        compiler_params=_params(("arbitrary",)),
        name="inproj",
    )(x, mod6, norm_g.reshape(1, d), w_in_r)


CONV_WIN = CHUNK + 16


def _conv_select(r0, w0):
    ti = lax.broadcasted_iota(jnp.int32, (2 * CHUNK, 2 * CONV_WIN), 0)
    tj = lax.broadcasted_iota(jnp.int32, (2 * CHUNK, 2 * CONV_WIN), 1)
    want = r0 + jnp.where(ti < CHUNK, ti - 1, ti - CHUNK + 1)
    have = w0 + jnp.where(tj < CONV_WIN, tj, tj - CONV_WIN)
    return jnp.where(want == have, 1.0, 0.0).astype(BF16)


def _conv3(load, r0, w0, sel, w):
    cur = load(pl.ds(r0, CHUNK))
    win = load(pl.ds(w0, CONV_WIN))
    hi = win.astype(BF16)
    lo = (win - hi.astype(F32)).astype(BF16)
    nb = jnp.dot(sel, jnp.concatenate([hi, lo], axis=0), preferred_element_type=F32)
    return w[0:1, :] * nb[:CHUNK] + w[1:2, :] * cur + w[2:3, :] * nb[CHUNK:]


def _mixer_kernel(*refs, seqs_per_step, batched_in, n_in, n_out, **static):
    ins, outs, scratch = refs[:n_in], refs[n_in:n_in + n_out], refs[n_in + n_out:]
    n_scr = len(scratch) // seqs_per_step
    seqs = []
    for s in range(seqs_per_step):
        one = lambda r, s=s: r.at[pl.ds(s, 1)]
        seqs.append(_mixer_phases([one(r) if i in batched_in else r for i, r in enumerate(ins)]
                                  + [one(r) for r in outs] + list(scratch[s * n_scr:(s + 1) * n_scr]), **static))
    nc, cpi = seqs[0].nc, seqs[0].cpi
    if nc <= STATIC_PREP_CHUNKS:
        for q in seqs:
            for z in range(nc):
                q.prep(z, 0)
            q.init()
        for q in seqs:
            for i in range(nc // cpi):
                q.delta_prep(i, 0)
        for z in range(nc):
            _scan_step(seqs, z)
        for q in seqs:
            for z in range(nc):
                q.finish(z, 0)
    else:
        for q in seqs:
            lax.fori_loop(0, nc, q.prep, 0, unroll=2)
            q.init()
            lax.fori_loop(0, nc // cpi, q.delta_prep, 0)

        def scan(z, carry):
            _scan_step(seqs, z)
            return carry

        lax.fori_loop(0, nc, scan, 0, unroll=4)
        for q in seqs:
            lax.fori_loop(0, nc, q.finish, 0, unroll=2)
    for q in seqs:
        q.emit()


def _scan_step(seqs, z):
    states = [q.scan_begin(z) for q in seqs]
    for stage in range(len(seqs[0].scan_stages)):
        for q, st in zip(seqs, states):
            q.scan_stages[stage](st)


def _mixer_phases(refs, seq_len, zero_init, state_layer, state_layers, state_aliased):
    refs = list(refs)
    proj, dnw, scw, ssw, lanev, ssdv = refs[:6]
    k = 6
    if not zero_init:
        sdn0, sssd0 = refs[k:k + 2]
        k += 2
    emit_state = state_layer is not None
    if state_aliased:
        k += 2
    ycat = refs[k]
    k += 1
    if emit_state:
        sdn_out, sssd_out = refs[k:k + 2]
        k += 2
    (qkv_s, xbc_s, bdup_s, cdup_s, gc_s, dpair_s, gcpair_s, tot_s, sp_s, beta_s, u_s, wq_s, kd_s, qk_s, o_s, y_s,
     st_s, hs_s) = refs[k:]

    nc = seq_len // CHUNK
    ri = lax.broadcasted_iota(jnp.int32, (CHUNK, CHUNK), 0)
    ci = lax.broadcasted_iota(jnp.int32, (CHUNK, CHUNK), 1)
    tril = (ri >= ci).astype(F32)
    r128 = lax.broadcasted_iota(jnp.int32, (LANES, LANES), 0)
    c128 = lax.broadcasted_iota(jnp.int32, (LANES, LANES), 1)
    eye128 = (r128 == c128).astype(F32)
    n_lvl = int(math.log2(CHUNK))
    alog = lanev[0:1, :]
    bias = lanev[1:2, :]
    pw = 2 * CHUNK
    ri2 = lax.broadcasted_iota(jnp.int32, (CHUNK, pw), 0)
    ci2 = lax.broadcasted_iota(jnp.int32, (CHUNK, pw), 1)
    lane_hi = ci2 >= CHUNK
    tj = ci2 & (CHUNK - 1)
    incl2 = (ri2 >= tj, ri2 <= tj)
    ahead = jnp.where(lane_hi, tj - ri2, ri2 - tj)
    incl_fb = ahead >= 0
    strict_fb = ahead > 0
    eye2 = (ri2 == tj).astype(F32)
    lvl2 = [((ri2 >> s) == (tj >> s)) & ((ri2 >> (s - 1)) != (tj >> (s - 1))) for s in range(1, n_lvl + 1)]

    def block_diag(x):
        return jnp.concatenate([jnp.where(lane_hi, 0.0, x), jnp.where(lane_hi, x, 0.0)], axis=0).astype(BF16)

    def chunk_rows(z):
        return pl.ds(z * CHUNK if isinstance(z, int) else pl.multiple_of(z * CHUNK, CHUNK), CHUNK)

    def prep(z, carry):
        r0 = z * CHUNK if isinstance(z, int) else pl.multiple_of(z * CHUNK, CHUNK)
        rows = pl.ds(r0, CHUNK)
        if isinstance(z, int):
            w0 = min(max(r0 - 8, 0), seq_len - CONV_WIN)
        else:
            w0 = pl.multiple_of(jnp.clip(r0 - 8, 0, seq_len - CONV_WIN), 8)
        sel = _conv_select(r0, w0)
        cw = 2 * LANES
        for jj in range(3 * A_W // cw):
            c0 = jj * cw
            a2 = _silu(_conv3(lambda rs: proj[0, rs, c0:c0 + cw], r0, w0, sel, dnw[:, c0:c0 + cw]))
            for half in range(cw // LANES):
                a = a2[:, half * LANES:(half + 1) * LANES]
                j = jj * (cw // LANES) + half
                if j < 2 * H_A:
                    a = a * lax.rsqrt(jnp.sum(a * a, axis=-1, keepdims=True) + EPS)
                if j < H_A:
                    a = a * (DK_A ** -0.5)
                qkv_s[rows, j * LANES:(j + 1) * LANES] = a
        for jj in range(B_W // cw):
            c0 = jj * cw
            cv = _conv3(lambda rs: proj[0, rs, COL_SCC + c0:COL_SCC + c0 + cw]
                        * proj[0, rs, COL_SCH + c0:COL_SCH + c0 + cw], r0, w0, sel, scw[:, c0:c0 + cw])
            yb = proj[0, rows, COL_SCB + c0:COL_SCB + c0 + cw] * cv
            ycat[0, rows, A_W + c0:A_W + c0 + cw] = yb.astype(ycat.dtype)
        for jj in range(XBC_W // cw):
            c1 = jj * cw
            a2 = _silu(_conv3(lambda rs: proj[0, rs, COL_XBC + c1:COL_XBC + c1 + cw], r0, w0, sel,
                              ssw[:, c1:c1 + cw]))
            if c1 < C_W:
                xbc_s[rows, c1:c1 + cw] = a2
                continue
            for half, dup_s in enumerate((bdup_s, cdup_s)):
                a = a2[:, half * LANES:(half + 1) * LANES]
                swapped = pltpu.roll(a, N_C, 1)
                lo = lax.broadcasted_iota(jnp.int32, a.shape, 1) < N_C
                dup_s[rows, 0:LANES] = jnp.where(lo, a, swapped)
                dup_s[rows, LANES:2 * LANES] = jnp.where(lo, swapped, a)
        sm = proj[0, rows, COL_SMALL:COL_SMALL + LANES]
        sp = _softplus(sm + bias)
        g = -jnp.exp(alog) * sp
        pre = _dot_f32(tril, g)
        tot = pre[CHUNK - 1:CHUNK, :]
        suf = tot - pre + g
        sp_s[rows, :] = sp
        beta_s[rows, :] = _sigmoid(sm)
        gc_s[0, rows, :] = pre
        gc_s[1, rows, :] = suf
        grs = [_dot_nt_f32(eye128, gcd) for gcd in (pre, suf)]
        for d, gr in enumerate(grs):
            for g in range(G_C):
                ln = LANE_DT + d * H_C + g * HPG
                gcpair_s[d, z, g:g + 1, :] = jnp.concatenate([gr[ln:ln + 1, :], gr[ln + 1:ln + 2, :]], axis=1)
        for h in range(H_A):
            lf, lb = LANE_ALPHA + h, LANE_ALPHA + H_A + h
            dpair_s[z, h:h + 1, :] = jnp.concatenate([grs[0][lf:lf + 1, :], grs[1][lb:lb + 1, :]], axis=1)
        tot_s[z] = jnp.broadcast_to(tot, (8, LANES))
        o_s[rows, :] = jnp.zeros((CHUNK, A_W), F32)
        y_s[rows, :] = jnp.zeros((CHUNK, C_W), F32)
        return carry

    def init():
        for d in range(2):
            for h in range(H_A):
                st_s[d * H_A + h] = jnp.zeros((DK_A, DV_A), F32) if zero_init else sdn0[0, d, h]
            for g in range(G_C):
                if zero_init:
                    hs_s[d, g] = jnp.zeros((HPG * P_C, HPG * N_C), F32)
                else:
                    zero = jnp.zeros((P_C, N_C), F32)
                    hs_s[d, g] = jnp.concatenate(
                        [jnp.concatenate([sssd0[0, d, g * HPG], zero], axis=1),
                         jnp.concatenate([zero, sssd0[0, d, g * HPG + 1]], axis=1)], axis=0)

    cpi = 4 if nc % 4 == 0 else 2

    def delta_prep(i, carry):
        units = []
        for zz in range(cpi):
            z = cpi * i + zz
            rows = chunk_rows(z)
            tot = tot_s[z][0:1, :]
            beta = beta_s[rows, :]
            gcs = [gc_s[d, rows, :] for d in range(2)]
            for h in range(H_A):
                q_h = qkv_s[rows, COL_Q + h * DK_A:COL_Q + (h + 1) * DK_A]
                k_h = qkv_s[rows, COL_K + h * DK_A:COL_K + (h + 1) * DK_A]
                v_h = qkv_s[rows, COL_V + h * DV_A:COL_V + (h + 1) * DV_A]
                units.append(dict(z=z, h=h, q=q_h, k=k_h, v=v_h, tot=tot, beta=beta, gcs=gcs))
        qkk = [_dot_nt(jnp.concatenate([p["q"], p["k"]], axis=0), jnp.concatenate([p["k"], p["k"]], axis=0))
               for p in units]
        ms, rhss = [], []
        for p, qk_kk in zip(units, qkk):
            z, h = p["z"], p["h"]
            lf, lb = LANE_ALPHA + h, LANE_ALPHA + H_A + h
            bf, bb = LANE_BETA + h, LANE_BETA + H_A + h
            a_pair = jnp.where(lane_hi, p["gcs"][1][:, lb:lb + 1], p["gcs"][0][:, lf:lf + 1])
            b_pair = jnp.where(lane_hi, p["beta"][:, bb:bb + 1], p["beta"][:, bf:bf + 1])
            decay = jnp.exp(jnp.where(incl_fb, a_pair - dpair_s[z, h:h + 1, :], -1e30))
            ms.append(jnp.where(strict_fb, qk_kk[CHUNK:] * b_pair * decay, 0.0))
            qk_s[z, h] = (qk_kk[:CHUNK] * decay).astype(qk_s.dtype)
            rhs_d = []
            for d, (ln, bl) in enumerate(((lf, bf), (lb, bb))):
                a_col = p["gcs"][d][:, ln:ln + 1]
                b_col = p["beta"][:, bl:bl + 1]
                eg = jnp.exp(a_col)
                rhs_d.append(jnp.concatenate([p["v"] * b_col, p["k"] * (b_col * eg)], axis=1))
                kd_s[d, z, h] = (p["k"] * jnp.exp(p["tot"][:, ln:ln + 1] - a_col)).astype(kd_s.dtype)
                wq_s[d, z, h, CHUNK:, :] = (p["q"] * eg).astype(wq_s.dtype)
            zero = jnp.zeros_like(rhs_d[0])
            rhss.append(jnp.concatenate([jnp.concatenate([rhs_d[0], zero], axis=1),
                                         jnp.concatenate([zero, rhs_d[1]], axis=1)], axis=0))
        t_inv = [eye2 - jnp.where(lvl2[0], m, 0.0) for m in ms]
        for s in range(1, n_lvl):
            x = [_dot(jnp.where(lvl2[s], m, 0.0), block_diag(t)) for m, t in zip(ms, t_inv)]
            t_inv = [t - _dot(t, block_diag(xx)) for t, xx in zip(t_inv, x)]
        uw = [_dot(t, rhs) for t, rhs in zip(t_inv, rhss)]
        for p, r in zip(units, uw):
            z, h = p["z"], p["h"]
            for d in range(2):
                c0 = d * (DV_A + DK_A)
                u_s[d, z, h] = r[:, c0:c0 + DV_A]
                wq_s[d, z, h, :CHUNK, :] = r[:, c0 + DV_A:c0 + DV_A + DK_A].astype(wq_s.dtype)
        return carry

    rb =lax.broadcasted_iota(jnp.int32, (pw, pw), 0) >= P_C
    cbk = lax.broadcasted_iota(jnp.int32, (pw, pw), 1) >= N_C
    diag_blk = rb == cbk

    def scan_begin(z):
        dus = [(d, (z if d == 0 else nc - 1 - z), h) for d in range(2) for h in range(H_A)]
        s_prev = [st_s[d * H_A + h] for d, _, h in dus]
        us = []
        for d in range(2):
            zc = z if d == 0 else nc - 1 - z
            rows = chunk_rows(zc)
            sp = sp_s[rows, :]
            tot = tot_s[zc][0:1, :]
            gc = gc_s[d, rows, :]
            for g in range(G_C):
                ln = LANE_DT + d * H_C + g * HPG
                gsl = slice(g * pw, (g + 1) * pw)
                a_pair = jnp.where(lane_hi, gc[:, ln + 1:ln + 2], gc[:, ln:ln + 1])
                t_pair = jnp.where(lane_hi, tot[:, ln + 1:ln + 2], tot[:, ln:ln + 1])
                lmat = jnp.exp(jnp.where(incl2[d], a_pair - gcpair_s[d, zc, g:g + 1, :], -1e30))
                xdt = xbc_s[rows, gsl] * jnp.where(lane_hi, sp[:, ln + 1:ln + 2], sp[:, ln:ln + 1])
                b_dup = bdup_s[rows, gsl]
                c_dup = cdup_s[rows, gsl]
                us.append(dict(d=d, g=g, rows=rows, gsl=gsl, lmat=lmat, xdt=xdt, c_dup=c_dup,
                               c_lo=jnp.where(lane_hi, 0.0, c_dup), b_st=jnp.concatenate([b_dup, b_dup], axis=0),
                               bdec=b_dup * jnp.exp(t_pair - a_pair), ea=jnp.exp(a_pair),
                               dec=jnp.where(rb, jnp.exp(tot[:, ln + 1:ln + 2]), jnp.exp(tot[:, ln:ln + 1])),
                               x_bd=jnp.concatenate([jnp.where(lane_hi, 0.0, xdt), jnp.where(lane_hi, xdt, 0.0)],
                                                    axis=0),
                               h_prev=hs_s[d, g]))
        return dict(dus=dus, s_prev=s_prev, us=us)

    def scan_delta_1(t):
        t["ws_qs"] = [_dot(wq_s[idx], s) for idx, s in zip(t["dus"], t["s_prev"])]

    def scan_ssd_1(t):
        t["st"] = [_dot_tn(p["xdt"], p["bdec"]) for p in t["us"]]
        t["cb"] = [_dot_nt(p["c_lo"], p["b_st"]) for p in t["us"]]
        t["y_off"] = [_dot_nt(p["c_dup"], p["h_prev"]) for p in t["us"]]

    def scan_delta_2(t):
        v_new = [u_s[idx] - r[:CHUNK] for idx, r in zip(t["dus"], t["ws_qs"])]
        zero = jnp.zeros((CHUNK, DV_A), F32)
        t["o_in"] = [_dot(qk_s[zc, h], jnp.concatenate([v, zero] if d == 0 else [zero, v], axis=0))
                     for (d, zc, h), v in zip(t["dus"], v_new)]
        t["s_add"] = [_dot_tn(kd_s[idx], v) for idx, v in zip(t["dus"], v_new)]

    def scan_ssd_2(t):
        t["y_diag"] = [_dot(cbd * p["lmat"], p["x_bd"]) for p, cbd in zip(t["us"], t["cb"])]

    def scan_store(t):
        for p, yd, yo, s in zip(t["us"], t["y_diag"], t["y_off"], t["st"]):
            y_s[p["rows"], p["gsl"]] = y_s[p["rows"], p["gsl"]] + yd + yo * p["ea"]
            hs_s[p["d"], p["g"]] = p["h_prev"] * p["dec"] + jnp.where(diag_blk, s, 0.0)
        for (d, zc, h), r, oi, sa, s in zip(t["dus"], t["ws_qs"], t["o_in"], t["s_add"], t["s_prev"]):
            ln = LANE_ALPHA + d * H_A + h
            cs = slice(h * DV_A, (h + 1) * DV_A)
            o_s[chunk_rows(zc), cs] = o_s[chunk_rows(zc), cs] + r[CHUNK:] + oi
            st_s[d * H_A + h] = s * jnp.exp(tot_s[zc][0:1, ln:ln + 1]) + sa


    def finish(z, carry):
        rows = chunk_rows(z)
        for h in range(H_A):
            cs = slice(h * DV_A, (h + 1) * DV_A)
            o = _rms(o_s[rows, cs], lanev[2:3, :])
            o = o * _silu(proj[0, rows, COL_GATE + h * DV_A:COL_GATE + (h + 1) * DV_A])
            ycat[0, rows, cs] = o.astype(ycat.dtype)
        y = y_s[rows, :] + ssdv[0:1, :] * xbc_s[rows, 0:C_W]
        y = _rms(y * _silu(proj[0, rows, COL_Z:COL_Z + C_W]), ssdv[1:2, :])
        ycat[0, rows, A_W + B_W:] = y.astype(ycat.dtype)
        return carry

    def emit():
        if not emit_state:
            return
        slots = (None,) if state_aliased else range(state_layers)
        for slot in slots:
            dn_slot = sdn_out.at[0] if slot is None else sdn_out.at[0, slot]
            ssd_slot = sssd_out.at[0] if slot is None else sssd_out.at[0, slot]
            mine = slot is None or slot == state_layer
            for d in range(2):
                for h in range(H_A):
                    dn_slot[d, h] = st_s[d * H_A + h] if mine else jnp.zeros((DK_A, DV_A), F32)
                for h in range(H_C):
                    k0 = (h % HPG) * P_C
                    ssd_slot[d, h] = (hs_s[d, h // HPG][k0:k0 + P_C, k0:k0 + N_C] if mine
                                      else jnp.zeros((P_C, N_C), F32))

    return types.SimpleNamespace(
        nc=nc, cpi=cpi, prep=prep, init=init, delta_prep=delta_prep, scan_begin=scan_begin,
        scan_stages=(scan_delta_1, scan_ssd_1, scan_delta_2, scan_ssd_2, scan_store), finish=finish, emit=emit)


def _mixer(proj, dn_conv_w, sc_conv_w, ssd_conv_w, lanev, ssdv, layer, s_dn0, s_ssd0, state_out):
    nb, seq_len, _ = proj.shape
    nc = seq_len // CHUNK
    assert nc % 2 == 0
    zero_init = s_dn0 is None
    spg = 2 if nc <= STATIC_PREP_CHUNKS and nb % 2 == 0 else 1
    full = lambda a: pl.BlockSpec((None,) + a.shape[1:], lambda b: (layer,) + (0,) * (a.ndim - 1))
    args = [proj, dn_conv_w, sc_conv_w, ssd_conv_w, lanev, ssdv]
    batched_in = [0]
    in_specs = [pl.BlockSpec((spg, seq_len, D_IN_PAD), lambda b: (b, 0, 0), pipeline_mode=pl.Buffered(1))
                if seq_len > 512 else pl.BlockSpec((spg, seq_len, D_IN_PAD), lambda b: (b, 0, 0)),
                full(dn_conv_w), full(sc_conv_w), full(ssd_conv_w), full(lanev), full(ssdv)]
    if not zero_init:
        batched_in += [len(args), len(args) + 1]
        args += [s_dn0, s_ssd0]
        in_specs += [pl.BlockSpec((spg, 2, H_A, DK_A, DV_A), lambda b: (b, 0, 0, 0, 0)),
                     pl.BlockSpec((spg, 2, H_C, P_C, N_C), lambda b: (b, 0, 0, 0, 0))]
    out_shape = [jax.ShapeDtypeStruct((nb, seq_len, D_MODEL), BF16)]
    out_specs = [pl.BlockSpec((spg, seq_len, D_MODEL), lambda b: (b, 0, 0))]
    aliases = {}
    state_layer, depth, prev = state_out if state_out is not None else (None, None, None)
    assert state_layer in (None, layer)
    if state_out is not None:
        out_shape += [jax.ShapeDtypeStruct((nb, depth, 2, H_A, DK_A, DV_A), F32),
                      jax.ShapeDtypeStruct((nb, depth, 2, H_C, P_C, N_C), F32)]
        if prev is None:
            out_specs += [pl.BlockSpec((spg, depth, 2, H_A, DK_A, DV_A), lambda b: (b, 0, 0, 0, 0, 0)),
                          pl.BlockSpec((spg, depth, 2, H_C, P_C, N_C), lambda b: (b, 0, 0, 0, 0, 0))]
        else:
            aliases = {len(args): 1, len(args) + 1: 2}
            args += list(prev)
            in_specs += [pl.BlockSpec(memory_space=pl.ANY), pl.BlockSpec(memory_space=pl.ANY)]
            out_specs += [pl.BlockSpec((spg, None, 2, H_A, DK_A, DV_A), lambda b: (b, layer, 0, 0, 0, 0)),
                          pl.BlockSpec((spg, None, 2, H_C, P_C, N_C), lambda b: (b, layer, 0, 0, 0, 0))]
    scratch = [pltpu.VMEM((seq_len, 3 * A_W), F32),
               pltpu.VMEM((seq_len, C_W), F32),
               pltpu.VMEM((seq_len, G_C * LANES), F32),
               pltpu.VMEM((seq_len, G_C * LANES), F32),
               pltpu.VMEM((2, seq_len, LANES), F32),
               pltpu.VMEM((nc, 8, LANES), F32),
               pltpu.VMEM((2, nc, 8, LANES), F32),
               pltpu.VMEM((nc, 8, LANES), F32),
               pltpu.VMEM((seq_len, LANES), F32),
               pltpu.VMEM((seq_len, LANES), F32),
               pltpu.VMEM((2, nc, H_A, CHUNK, DV_A), F32),
               pltpu.VMEM((2, nc, H_A, 2 * CHUNK, DK_A), BF16),
               pltpu.VMEM((2, nc, H_A, CHUNK, DK_A), BF16),
               pltpu.VMEM((nc, H_A, CHUNK, 2 * CHUNK), BF16),
               pltpu.VMEM((seq_len, A_W), F32),
               pltpu.VMEM((seq_len, C_W), F32),
               pltpu.VMEM((2 * H_A, DK_A, DV_A), F32),
               pltpu.VMEM((2, G_C, HPG * P_C, HPG * N_C), F32)]
    return pl.pallas_call(
        functools.partial(_mixer_kernel, seqs_per_step=spg, batched_in=tuple(batched_in), n_in=len(args),
                          n_out=len(out_shape), seq_len=seq_len, zero_init=zero_init, state_layer=state_layer,
                          state_layers=depth, state_aliased=prev is not None),
        grid=(nb // spg,),
        in_specs=in_specs,
        out_specs=out_specs,
        out_shape=out_shape,
        input_output_aliases=aliases,
        scratch_shapes=scratch * spg,
        compiler_params=_params(("arbitrary",)),
        name="mixer",
    )(*args)


def _top2_sum(a, b, c, d):
    hi1, lo1 = jnp.maximum(a, b), jnp.minimum(a, b)
    hi2, lo2 = jnp.maximum(c, d), jnp.minimum(c, d)
    return jnp.maximum(hi1, hi2) + jnp.maximum(jnp.minimum(hi1, hi2), jnp.maximum(lo1, lo2))


def _outproj_kernel(y_ref, x_ref, mod_ref, w_ref, g_ref, rw_ref, rb_ref, x1_ref, h2_ref, comb_ref, wb_ref):
    @pl.when(pl.program_id(0) == 0)
    def _():
        wb_ref[...] = w_ref[...].astype(wb_ref.dtype)

    m = mod_ref[0]
    gate1, shift2, scale2 = m[2:3, :], m[3:4, :], m[4:5, :]
    x1 = x_ref[...] + gate1 * _dot(y_ref[...], wb_ref[...])
    x1_ref[...] = x1
    h2 = _rms(x1, g_ref[...]) * (1.0 + scale2) + shift2
    h2_ref[...] = h2.astype(h2_ref.dtype)

    h_hi = h2.astype(BF16)
    h_lo = (h2 - h_hi.astype(F32)).astype(BF16)
    rw_hi, rw_lo = rw_ref[0], rw_ref[1]
    hl = jnp.dot(jnp.concatenate([h_hi, h_lo], axis=0), rw_hi, preferred_element_type=F32)
    tm = h2.shape[0]
    logits = hl[:tm] + hl[tm:] + jnp.dot(h_hi, rw_lo, preferred_element_type=F32)
    scores = _sigmoid(logits.T[:N_EXPERTS, :])
    biased = scores + rb_ref[...]
    sc = [scores[e:e + 1, :] for e in range(N_EXPERTS)]
    bi = [biased[e:e + 1, :] for e in range(N_EXPERTS)]
    gs = [_top2_sum(*bi[EPG * g:EPG * (g + 1)]) for g in range(N_GROUPS)]
    gmax = functools.reduce(jnp.maximum, gs)
    first = []
    taken = None
    for g in range(N_GROUPS):
        hit = gs[g] == gmax
        if taken is None:
            first.append(hit)
            taken = hit
        else:
            first.append(hit & jnp.logical_not(taken))
            taken = taken | hit

    def pick(vals, j):
        out = vals[EPG * (N_GROUPS - 1) + j]
        for g in range(N_GROUPS - 2, -1, -1):
            out = jnp.where(first[g], vals[EPG * g + j], out)
        return out

    ib = [pick(bi, j) for j in range(EPG)]
    isc = [pick(sc, j) for j in range(EPG)]
    sel = []
    for j in range(EPG):
        cnt = jnp.zeros_like(ib[j])
        for i in range(EPG):
            if i == j:
                continue
            ahead = (ib[i] > ib[j]) | ((ib[i] == ib[j]) if i < j else False)
            cnt = cnt + jnp.where(ahead, 1.0, 0.0)
        sel.append(cnt < 2.0)
    wj = [jnp.where(sel[j], isc[j], 0.0) for j in range(EPG)]
    denom = functools.reduce(lambda a, b: a + b, wj)
    for g in range(N_GROUPS):
        for j in range(EPG):
            comb_ref[EPG * g + j:EPG * g + j + 1, :] = jnp.where(first[g], wj[j] / denom, 0.0)


def _outproj(ycat, x, mod6, seq_rows, w_out, layer, norm_g, router_w_pad, router_b):
    t, d = x.shape
    tm = _tile(seq_rows, 512)
    return pl.pallas_call(
        _outproj_kernel,
        grid=(t // tm,),
        in_specs=[pl.BlockSpec((tm, d), lambda i: (i, 0)),
                  pl.BlockSpec((tm, d), lambda i: (i, 0)),
                  pl.BlockSpec((1, 6, d), lambda i: ((i * tm) // seq_rows, 0, 0)),
                  pl.BlockSpec((None, d, d), lambda i: (layer, 0, 0)),
                  pl.BlockSpec((1, d), lambda i: (0, 0)),
                  pl.BlockSpec((2, d, LANES), lambda i: (0, 0, 0)),
                  pl.BlockSpec((N_EXPERTS, 1), lambda i: (0, 0))],
        out_specs=[pl.BlockSpec((tm, d), lambda i: (i, 0)),
                   pl.BlockSpec((tm, d), lambda i: (i, 0)),
                   pl.BlockSpec((N_EXPERTS, tm), lambda i: (0, i))],
        out_shape=[jax.ShapeDtypeStruct((t, d), F32),
                   jax.ShapeDtypeStruct((t, d), BF16),
                   jax.ShapeDtypeStruct((N_EXPERTS, t), F32)],
        scratch_shapes=[pltpu.VMEM((d, d), BF16)],
        compiler_params=_params(("arbitrary",)),
        name="outproj_route",
    )(ycat, x, mod6, w_out, norm_g.reshape(1, d), router_w_pad, router_b.reshape(N_EXPERTS, 1))


def _moe_kernel(h_ref, comb_ref, wg_ref, wu_ref, wd_ref, x1_ref, mod_ref, fg_ref, o_ref, *, final):
    g = pl.program_id(1)

    @pl.when(g == 0)
    def _():
        o_ref[...] = jnp.zeros_like(o_ref)

    h = h_ref[...]
    comb = comb_ref[...]
    acts = [(_silu(_dot(h, wg_ref[j])) * _dot(h, wu_ref[j]) * comb[:, j:j + 1]).astype(BF16) for j in range(EPG)]
    o_ref[...] += _dot(jnp.concatenate(acts, axis=1), wd_ref[...])

    @pl.when(g == pl.num_programs(1) - 1)
    def _():
        x2 = x1_ref[...] + mod_ref[0][5:6, :] * o_ref[...]
        o_ref[...] = _rms(x2, fg_ref[...]) if final else x2


def _moe(h2, comb, w_gate, w_up, w_down, layer, x1, mod6, seq_rows, final_g, final):
    t, d = x1.shape
    tm = _tile(seq_rows, 1024)
    return pl.pallas_call(
        functools.partial(_moe_kernel, final=final),
        grid=(t // tm, N_GROUPS),
        in_specs=[pl.BlockSpec((tm, d), lambda i, g: (i, 0)),
                  pl.BlockSpec((None, tm, EPG), lambda i, g: (g, i, 0)),
                  pl.BlockSpec((None, EPG, d, D_EXPERT), lambda i, g: (layer, g, 0, 0)),
                  pl.BlockSpec((None, EPG, d, D_EXPERT), lambda i, g: (layer, g, 0, 0)),
                  pl.BlockSpec((None, None, EPG * D_EXPERT, d), lambda i, g: (layer, g, 0, 0)),
                  pl.BlockSpec((tm, d), lambda i, g: (i, 0)),
                  pl.BlockSpec((1, 6, d), lambda i, g: ((i * tm) // seq_rows, 0, 0)),
                  pl.BlockSpec((1, d), lambda i, g: (0, 0))],
        out_specs=pl.BlockSpec((tm, d), lambda i, g: (i, 0)),
        out_shape=jax.ShapeDtypeStruct((t, d), F32),
        compiler_params=_params(("arbitrary", "arbitrary")),
        name="experts",
    )(h2, comb, w_gate, w_up, w_down, x1, mod6, final_g.reshape(1, d))


def _lane_rows(dn, ssd):
    depth = dn.shape[0]
    pad = lambda n: jnp.zeros((depth, n), F32)
    return jnp.concatenate([pad(LANE_ALPHA), dn.reshape(depth, -1).astype(F32), ssd.reshape(depth, -1).astype(F32),
                            pad(LANES - LANE_DT - 2 * H_C)], axis=1)


def kernel(x_prompt, x_sample, state_delta, state_ssd, c, c_ctx, mod_w, mod_b, norm1_g, norm2_g, w_in, w_out,
           dn_conv_w, dn_a_log, dn_dt_bias, dn_norm_g, sc_conv_w, ssd_conv_w, ssd_a_log, ssd_dt_bias, ssd_d,
           ssd_norm_g, router_w, router_b, exp_w_gate, exp_w_up, exp_w_down, final_norm_g):
    depth = mod_w.shape[0]
    n_ctx, seq, d = x_prompt.shape
    n_dec, dec_seq, _ = x_sample.shape
    assert seq % CHUNK == 0 and dec_seq % CHUNK == 0 and d == D_MODEL

    n_rows = -(-(1 + n_dec) // 8) * 8
    cond_rows = jnp.zeros((n_rows, d), F32).at[0].set(c_ctx).at[1:1 + n_dec].set(c)
    mod = _modulation(cond_rows, mod_w, mod_b).reshape(depth, n_rows, 6, d)

    w_in_b = w_in.astype(BF16)
    w_in_r = jnp.concatenate(
        [w_in_b[:, :, :SRC_SMALL_A], w_in_b[:, :, SRC_SCH:SRC_SMALL_B], w_in_b[:, :, SRC_SMALL_A:SRC_SCH],
         w_in_b[:, :, SRC_SMALL_B:], jnp.zeros((depth, d, D_IN_PAD - D_IN), BF16)], axis=-1)
    w_down_g = exp_w_down.reshape(depth, N_GROUPS, EPG * D_EXPERT, d)
    rw = jnp.pad(router_w.astype(F32), ((0, 0), (0, LANES - N_EXPERTS)))
    rw_hi = rw.astype(BF16)
    router_w_pad = jnp.stack([rw_hi, (rw - rw_hi.astype(F32)).astype(BF16)])

    xp = x_prompt.reshape(n_ctx * seq, d)
    pos = jnp.asarray(_grid_pos_embed(dec_seq, d), dtype=x_sample.dtype)
    xs = _add_pos(x_sample, pos).reshape(n_dec * dec_seq, d)

    lanev = jnp.stack([_lane_rows(dn_a_log, ssd_a_log), _lane_rows(dn_dt_bias, ssd_dt_bias), dn_norm_g.astype(F32)]
                      + [jnp.zeros((depth, LANES), F32)] * 5, axis=1)
    ssdv = jnp.stack([jnp.repeat(ssd_d.astype(F32), P_C, axis=1), ssd_norm_g.astype(F32)]
                     + [jnp.zeros((depth, C_W), F32)] * 6, axis=1)

    states = None
    for l in range(depth):
        final = l == depth - 1

        def block(x, mod6, nb, seq_len, seq_rows, s_dn0, s_ssd0, state_out):
            proj = _inproj(x, mod6, seq_rows, norm1_g[l], w_in_r, l).reshape(nb, seq_len, D_IN_PAD)
            outs = _mixer(proj, dn_conv_w, sc_conv_w, ssd_conv_w, lanev, ssdv, l, s_dn0, s_ssd0, state_out)
            ycat = outs[0].reshape(nb * seq_len, d)
            x1, h2, comb_t = _outproj(ycat, x, mod6, seq_rows, w_out, l, norm2_g[l], router_w_pad, router_b)
            comb = comb_t.reshape(N_GROUPS, EPG, -1).transpose(0, 2, 1)
            x2 = _moe(h2, comb, exp_w_gate, exp_w_up, w_down_g, l, x1, mod6, seq_rows, final_norm_g, final)
            return x2, outs[1:]

        xp, states = block(xp, mod[l, 0:1], n_ctx, seq, n_ctx * seq, None, None, (l, depth, states))
        xs, _ = block(xs, mod[l, 1:1 + n_dec], n_dec, dec_seq, dec_seq,
                      state_delta[:, l].astype(F32), state_ssd[:, l].astype(F32), None)

    return (xp.reshape(n_ctx, seq, d), xs.reshape(n_dec, dec_seq, d), states[0], states[1])
```

```python
import functools
import math
import types

import jax
import jax.numpy as jnp
import numpy as np
from jax import lax
from jax.experimental import pallas as pl
from jax.experimental.pallas import tpu as pltpu

F32 = jnp.float32
BF16 = jnp.bfloat16

D_MODEL = 1024
GRID_W = 64
POS_BASE = 10000.0
H_A, DK_A, DV_A = 4, 128, 128
A_W = H_A * DV_A
H_C, P_C, N_C, G_C = 4, 64, 64, 2
HPG = H_C // G_C
C_W = H_C * P_C
B_W = D_MODEL - A_W - C_W
XBC_W = C_W + 2 * G_C * N_C
CHUNK = 64
N_EXPERTS = 16
N_GROUPS = 4
EPG = N_EXPERTS // N_GROUPS
D_EXPERT = 256
EPS = 1e-6
LANES = 128

COL_Q, COL_K, COL_V, COL_GATE = 0, A_W, 2 * A_W, 3 * A_W
COL_SCH = 4 * A_W
COL_SCB = COL_SCH + B_W
COL_SCC = COL_SCB + B_W
COL_Z = COL_SCC + B_W
COL_XBC = COL_Z + C_W
COL_SMALL = COL_XBC + XBC_W
D_IN_PAD = COL_SMALL + LANES
LANE_BETA, LANE_ALPHA, LANE_DT = 0, 2 * H_A, 4 * H_A
SRC_SMALL_A = 4 * A_W
SRC_SCH = SRC_SMALL_A + 4 * H_A
SRC_SMALL_B = SRC_SCH + 3 * B_W + C_W + XBC_W
D_IN = SRC_SMALL_B + 2 * H_C

assert HPG == 2 and P_C == N_C == CHUNK and HPG * P_C == LANES and G_C * N_C == LANES and DK_A == DV_A == LANES

V7X_VMEM_BYTES = 64 * 1024 * 1024
VMEM_LIMIT = V7X_VMEM_BYTES * 7 // 8
STATIC_PREP_CHUNKS = 4


def _dot(a, b):
    return jnp.dot(a.astype(BF16), b.astype(BF16), preferred_element_type=F32)


def _dot_nt(a, b):
    return lax.dot_general(a.astype(BF16), b.astype(BF16), (((1,), (1,)), ((), ())), preferred_element_type=F32)


def _dot_tn(a, b):
    return lax.dot_general(a.astype(BF16), b.astype(BF16), (((0,), (0,)), ((), ())), preferred_element_type=F32)


def _dot_f32(a, b):
    return jnp.dot(a, b, precision=lax.Precision.HIGHEST, preferred_element_type=F32)


def _dot_nt_f32(a, b):
    return lax.dot_general(a, b, (((1,), (1,)), ((), ())), precision=lax.Precision.HIGHEST,
                           preferred_element_type=F32)


def _silu(x):
    h = 0.5 * x
    return h + h * jnp.tanh(h)


def _sigmoid(x):
    return 1.0 / (1.0 + jnp.exp(-x))


def _softplus(x):
    return jnp.maximum(x, 0.0) + jnp.log1p(jnp.exp(-jnp.abs(x)))


def _rms(x, g):
    return x * lax.rsqrt(jnp.mean(x * x, axis=-1, keepdims=True) + EPS) * g


def _tile(n, pref):
    t = min(n, pref)
    while n % t:
        t -= 8
    assert t > 0 and t % 8 == 0, (n, pref)
    return t


def _params(sem):
    return pltpu.CompilerParams(dimension_semantics=sem, vmem_limit_bytes=VMEM_LIMIT)


def _mod_kernel(cond_ref, w_ref, b_ref, o_ref):
    s = _silu(cond_ref[...])
    o_ref[0] = _dot_f32(s, w_ref[0]) + b_ref[0]


def _modulation(cond_rows, mod_w, mod_b):
    depth, d, n = mod_w.shape
    r = cond_rows.shape[0]
    tn = _tile(n, 1536)
    return pl.pallas_call(
        _mod_kernel,
        grid=(depth, n // tn),
        in_specs=[pl.BlockSpec((r, d), lambda l, j: (0, 0)),
                  pl.BlockSpec((1, d, tn), lambda l, j: (l, 0, j)),
                  pl.BlockSpec((1, 1, tn), lambda l, j: (l, 0, j))],
        out_specs=pl.BlockSpec((1, r, tn), lambda l, j: (l, 0, j)),
        out_shape=jax.ShapeDtypeStruct((depth, r, n), F32),
        compiler_params=_params(("arbitrary", "arbitrary")),
        name="modulation",
    )(cond_rows, mod_w, mod_b.reshape(depth, 1, n))


def _add_kernel(x_ref, p_ref, o_ref):
    o_ref[0] = x_ref[0] + p_ref[...]


def _add_pos(x, pos):
    nb, l, d = x.shape
    tl = _tile(l, 512)
    return pl.pallas_call(
        _add_kernel,
        grid=(nb, l // tl),
        in_specs=[pl.BlockSpec((1, tl, d), lambda b, i: (b, i, 0)),
                  pl.BlockSpec((tl, d), lambda b, i: (i, 0))],
        out_specs=pl.BlockSpec((1, tl, d), lambda b, i: (b, i, 0)),
        out_shape=jax.ShapeDtypeStruct(x.shape, x.dtype),
        compiler_params=_params(("arbitrary", "arbitrary")),
        name="add_pos",
    )(x, pos)


def _grid_pos_embed(n_tok, dim):
    rows = n_tok // GRID_W
    rr, cc = np.meshgrid(np.arange(rows, dtype=np.float64), np.arange(GRID_W, dtype=np.float64), indexing="ij")
    quarter = dim // 4
    omega = 1.0 / (POS_BASE ** (np.arange(quarter, dtype=np.float64) / quarter))
    ang_r = rr.reshape(-1, 1) * omega
    ang_c = cc.reshape(-1, 1) * omega
    return np.concatenate([np.sin(ang_r), np.cos(ang_r), np.sin(ang_c), np.cos(ang_c)], axis=-1)


def _inproj_kernel(x_ref, mod_ref, g_ref, w_ref, o_ref):
    m = mod_ref[0]
    shift, scale = m[0:1, :], m[1:2, :]
    h = _rms(x_ref[...], g_ref[...]) * (1.0 + scale) + shift
    o_ref[...] = _dot(h, w_ref[...])


def _inproj(x, mod6, seq_rows, norm_g, w_in_r, layer):
    t, d = x.shape
    tm = _tile(seq_rows, 512)
    return pl.pallas_call(
        _inproj_kernel,
        grid=(t // tm,),
        in_specs=[pl.BlockSpec((tm, d), lambda i: (i, 0)),
                  pl.BlockSpec((1, 6, d), lambda i: ((i * tm) // seq_rows, 0, 0)),
                  pl.BlockSpec((1, d), lambda i: (0, 0)),
                  pl.BlockSpec((None, d, D_IN_PAD), lambda i: (layer, 0, 0))],
        out_specs=pl.BlockSpec((tm, D_IN_PAD), lambda i: (i, 0)),
        out_shape=jax.ShapeDtypeStruct((t, D_IN_PAD), F32),
        compiler_params=_params(("arbitrary",)),
        name="inproj",
    )(x, mod6, norm_g.reshape(1, d), w_in_r)


CONV_WIN = CHUNK + 16


def _conv_select(r0, w0):
    ti = lax.broadcasted_iota(jnp.int32, (2 * CHUNK, 2 * CONV_WIN), 0)
    tj = lax.broadcasted_iota(jnp.int32, (2 * CHUNK, 2 * CONV_WIN), 1)
    want = r0 + jnp.where(ti < CHUNK, ti - 1, ti - CHUNK + 1)
    have = w0 + jnp.where(tj < CONV_WIN, tj, tj - CONV_WIN)
    return jnp.where(want == have, 1.0, 0.0).astype(BF16)


def _conv3(load, r0, w0, sel, w):
    cur = load(pl.ds(r0, CHUNK))
    win = load(pl.ds(w0, CONV_WIN))
    hi = win.astype(BF16)
    lo = (win - hi.astype(F32)).astype(BF16)
    nb = jnp.dot(sel, jnp.concatenate([hi, lo], axis=0), preferred_element_type=F32)
    return w[0:1, :] * nb[:CHUNK] + w[1:2, :] * cur + w[2:3, :] * nb[CHUNK:]


def _mixer_kernel(*refs, seqs_per_step, batched_in, n_in, n_out, **static):
    ins, outs, scratch = refs[:n_in], refs[n_in:n_in + n_out], refs[n_in + n_out:]
    n_scr = len(scratch) // seqs_per_step
    seqs = []
    for s in range(seqs_per_step):
        one = lambda r, s=s: r.at[pl.ds(s, 1)]
        seqs.append(_mixer_phases([one(r) if i in batched_in else r for i, r in enumerate(ins)]
                                  + [one(r) for r in outs] + list(scratch[s * n_scr:(s + 1) * n_scr]), **static))
    nc, cpi = seqs[0].nc, seqs[0].cpi
    if nc <= STATIC_PREP_CHUNKS:
        for q in seqs:
            for z in range(nc):
                q.prep(z, 0)
            q.init()
        for i in range(nc // cpi):
            _staged(seqs, "delta", i)
        for z in range(nc):
            _staged(seqs, "scan", z)
        for q in seqs:
            for z in range(nc):
                q.finish(z, 0)
    else:
        for q in seqs:
            lax.fori_loop(0, nc, q.prep, 0, unroll=2)
            q.init()

        def staged(phase):
            def body(i, carry):
                _staged(seqs, phase, i)
                return carry
            return body

        lax.fori_loop(0, nc // cpi, staged("delta"), 0)
        lax.fori_loop(0, nc, staged("scan"), 0, unroll=4)
        for q in seqs:
            lax.fori_loop(0, nc, q.finish, 0, unroll=2)
    for q in seqs:
        q.emit()


def _staged(seqs, phase, i):
    begin, stages = f"{phase}_begin", f"{phase}_stages"
    states = [getattr(q, begin)(i) for q in seqs]
    for stage in range(len(getattr(seqs[0], stages))):
        for q, st in zip(seqs, states):
            getattr(q, stages)[stage](st)


def _mixer_phases(refs, seq_len, zero_init, state_layer, state_layers, state_aliased):
    refs = list(refs)
    proj, dnw, scw, ssw, lanev, ssdv = refs[:6]
    k = 6
    if not zero_init:
        sdn0, sssd0 = refs[k:k + 2]
        k += 2
    emit_state = state_layer is not None
    if state_aliased:
        k += 2
    ycat = refs[k]
    k += 1
    if emit_state:
        sdn_out, sssd_out = refs[k:k + 2]
        k += 2
    (qkv_s, xbc_s, bdup_s, cdup_s, gc_s, dpair_s, gcpair_s, tot_s, sp_s, beta_s, u_s, wq_s, kd_s, qk_s, o_s, y_s,
     st_s, hs_s) = refs[k:]

    nc = seq_len // CHUNK
    ri = lax.broadcasted_iota(jnp.int32, (CHUNK, CHUNK), 0)
    ci = lax.broadcasted_iota(jnp.int32, (CHUNK, CHUNK), 1)
    tril = (ri >= ci).astype(F32)
    r128 = lax.broadcasted_iota(jnp.int32, (LANES, LANES), 0)
    c128 = lax.broadcasted_iota(jnp.int32, (LANES, LANES), 1)
    eye128 = (r128 == c128).astype(F32)
    n_lvl = int(math.log2(CHUNK))
    alog = lanev[0:1, :]
    bias = lanev[1:2, :]
    pw = 2 * CHUNK
    ri2 = lax.broadcasted_iota(jnp.int32, (CHUNK, pw), 0)
    ci2 = lax.broadcasted_iota(jnp.int32, (CHUNK, pw), 1)
    lane_hi = ci2 >= CHUNK
    tj = ci2 & (CHUNK - 1)
    incl2 = (ri2 >= tj, ri2 <= tj)
    ahead = jnp.where(lane_hi, tj - ri2, ri2 - tj)
    incl_fb = ahead >= 0
    strict_fb = ahead > 0
    eye2 = (ri2 == tj).astype(F32)
    lvl2 = [((ri2 >> s) == (tj >> s)) & ((ri2 >> (s - 1)) != (tj >> (s - 1))) for s in range(1, n_lvl + 1)]

    def block_diag(x):
        return jnp.concatenate([jnp.where(lane_hi, 0.0, x), jnp.where(lane_hi, x, 0.0)], axis=0).astype(BF16)

    def chunk_rows(z):
        return pl.ds(z * CHUNK if isinstance(z, int) else pl.multiple_of(z * CHUNK, CHUNK), CHUNK)

    def prep(z, carry):
        r0 = z * CHUNK if isinstance(z, int) else pl.multiple_of(z * CHUNK, CHUNK)
        rows = pl.ds(r0, CHUNK)
        if isinstance(z, int):
            w0 = min(max(r0 - 8, 0), seq_len - CONV_WIN)
        else:
            w0 = pl.multiple_of(jnp.clip(r0 - 8, 0, seq_len - CONV_WIN), 8)
        sel = _conv_select(r0, w0)
        cw = 2 * LANES
        for jj in range(3 * A_W // cw):
            c0 = jj * cw
            a2 = _silu(_conv3(lambda rs: proj[0, rs, c0:c0 + cw], r0, w0, sel, dnw[:, c0:c0 + cw]))
            for half in range(cw // LANES):
                a = a2[:, half * LANES:(half + 1) * LANES]
                j = jj * (cw // LANES) + half
                if j < 2 * H_A:
                    a = a * lax.rsqrt(jnp.sum(a * a, axis=-1, keepdims=True) + EPS)
                if j < H_A:
                    a = a * (DK_A ** -0.5)
                qkv_s[rows, j * LANES:(j + 1) * LANES] = a
        for jj in range(B_W // cw):
            c0 = jj * cw
            cv = _conv3(lambda rs: proj[0, rs, COL_SCC + c0:COL_SCC + c0 + cw]
                        * proj[0, rs, COL_SCH + c0:COL_SCH + c0 + cw], r0, w0, sel, scw[:, c0:c0 + cw])
            yb = proj[0, rows, COL_SCB + c0:COL_SCB + c0 + cw] * cv
            ycat[0, rows, A_W + c0:A_W + c0 + cw] = yb.astype(ycat.dtype)
        for jj in range(XBC_W // cw):
            c1 = jj * cw
            a2 = _silu(_conv3(lambda rs: proj[0, rs, COL_XBC + c1:COL_XBC + c1 + cw], r0, w0, sel,
                              ssw[:, c1:c1 + cw]))
            if c1 < C_W:
                xbc_s[rows, c1:c1 + cw] = a2
                continue
            for half, dup_s in enumerate((bdup_s, cdup_s)):
                a = a2[:, half * LANES:(half + 1) * LANES]
                swapped = pltpu.roll(a, N_C, 1)
                lo = lax.broadcasted_iota(jnp.int32, a.shape, 1) < N_C
                dup_s[rows, 0:LANES] = jnp.where(lo, a, swapped)
                dup_s[rows, LANES:2 * LANES] = jnp.where(lo, swapped, a)
        sm = proj[0, rows, COL_SMALL:COL_SMALL + LANES]
        sp = _softplus(sm + bias)
        g = -jnp.exp(alog) * sp
        pre = _dot_f32(tril, g)
        tot = pre[CHUNK - 1:CHUNK, :]
        suf = tot - pre + g
        sp_s[rows, :] = sp
        beta_s[rows, :] = _sigmoid(sm)
        gc_s[0, rows, :] = pre
        gc_s[1, rows, :] = suf
        grs = [_dot_nt_f32(eye128, gcd) for gcd in (pre, suf)]
        for d, gr in enumerate(grs):
            for g in range(G_C):
                ln = LANE_DT + d * H_C + g * HPG
                gcpair_s[d, z, g:g + 1, :] = jnp.concatenate([gr[ln:ln + 1, :], gr[ln + 1:ln + 2, :]], axis=1)
        for h in range(H_A):
            lf, lb = LANE_ALPHA + h, LANE_ALPHA + H_A + h
            dpair_s[z, h:h + 1, :] = jnp.concatenate([grs[0][lf:lf + 1, :], grs[1][lb:lb + 1, :]], axis=1)
        tot_s[z] = jnp.broadcast_to(tot, (8, LANES))
        o_s[rows, :] = jnp.zeros((CHUNK, A_W), F32)
        y_s[rows, :] = jnp.zeros((CHUNK, C_W), F32)
        return carry

    def init():
        for d in range(2):
            for h in range(H_A):
                st_s[d * H_A + h] = jnp.zeros((DK_A, DV_A), F32) if zero_init else sdn0[0, d, h]
            for g in range(G_C):
                if zero_init:
                    hs_s[d, g] = jnp.zeros((HPG * P_C, HPG * N_C), F32)
                else:
                    zero = jnp.zeros((P_C, N_C), F32)
                    hs_s[d, g] = jnp.concatenate(
                        [jnp.concatenate([sssd0[0, d, g * HPG], zero], axis=1),
                         jnp.concatenate([zero, sssd0[0, d, g * HPG + 1]], axis=1)], axis=0)

    cpi = 4 if nc % 4 == 0 else 2

    def delta_begin(i):
        units = []
        for zz in range(cpi):
            z = cpi * i + zz
            rows = chunk_rows(z)
            tot = tot_s[z][0:1, :]
            beta = beta_s[rows, :]
            gcs = [gc_s[d, rows, :] for d in range(2)]
            for h in range(H_A):
                q_h = qkv_s[rows, COL_Q + h * DK_A:COL_Q + (h + 1) * DK_A]
                k_h = qkv_s[rows, COL_K + h * DK_A:COL_K + (h + 1) * DK_A]
                v_h = qkv_s[rows, COL_V + h * DV_A:COL_V + (h + 1) * DV_A]
                units.append(dict(z=z, h=h, q=q_h, k=k_h, v=v_h, tot=tot, beta=beta, gcs=gcs))
        return dict(units=units)

    def delta_gram(t):
        t["qkk"] = [_dot_nt(jnp.concatenate([p["q"], p["k"]], axis=0), jnp.concatenate([p["k"], p["k"]], axis=0))
                    for p in t["units"]]

    def delta_operands(t):
        ms, rhss = [], []
        for p, qk_kk in zip(t["units"], t["qkk"]):
            z, h = p["z"], p["h"]
            lf, lb = LANE_ALPHA + h, LANE_ALPHA + H_A + h
            bf, bb = LANE_BETA + h, LANE_BETA + H_A + h
            a_pair = jnp.where(lane_hi, p["gcs"][1][:, lb:lb + 1], p["gcs"][0][:, lf:lf + 1])
            b_pair = jnp.where(lane_hi, p["beta"][:, bb:bb + 1], p["beta"][:, bf:bf + 1])
            decay = jnp.exp(jnp.where(incl_fb, a_pair - dpair_s[z, h:h + 1, :], -1e30))
            ms.append(jnp.where(strict_fb, qk_kk[CHUNK:] * b_pair * decay, 0.0))
            qk_s[z, h] = (qk_kk[:CHUNK] * decay).astype(qk_s.dtype)
            rhs_d = []
            for d, (ln, bl) in enumerate(((lf, bf), (lb, bb))):
                a_col = p["gcs"][d][:, ln:ln + 1]
                b_col = p["beta"][:, bl:bl + 1]
                eg = jnp.exp(a_col)
                rhs_d.append(jnp.concatenate([p["v"] * b_col, p["k"] * (b_col * eg)], axis=1))
                kd_s[d, z, h] = (p["k"] * jnp.exp(p["tot"][:, ln:ln + 1] - a_col)).astype(kd_s.dtype)
                wq_s[d, z, h, CHUNK:, :] = (p["q"] * eg).astype(wq_s.dtype)
            zero = jnp.zeros_like(rhs_d[0])
            rhss.append(jnp.concatenate([jnp.concatenate([rhs_d[0], zero], axis=1),
                                         jnp.concatenate([zero, rhs_d[1]], axis=1)], axis=0))
        t.update(ms=ms, rhss=rhss, t_inv=[eye2 - jnp.where(lvl2[0], m, 0.0) for m in ms])

    def delta_level_a(s, t):
        t["x"] = [_dot(jnp.where(lvl2[s], m, 0.0), block_diag(ti)) for m, ti in zip(t["ms"], t["t_inv"])]

    def delta_level_b(t):
        t["t_inv"] = [ti - _dot(ti, block_diag(xx)) for ti, xx in zip(t["t_inv"], t["x"])]

    def delta_solve(t):
        t["uw"] = [_dot(ti, rhs) for ti, rhs in zip(t["t_inv"], t["rhss"])]

    def delta_store(t):
        for p, r in zip(t["units"], t["uw"]):
            z, h = p["z"], p["h"]
            for d in range(2):
                c0 = d * (DV_A + DK_A)
                u_s[d, z, h] = r[:, c0:c0 + DV_A]
                wq_s[d, z, h, :CHUNK, :] = r[:, c0 + DV_A:c0 + DV_A + DK_A].astype(wq_s.dtype)

    delta_stages = [delta_gram, delta_operands]
    for s in range(1, n_lvl):
        delta_stages += [functools.partial(delta_level_a, s), delta_level_b]
    delta_stages += [delta_solve, delta_store]

    rb = lax.broadcasted_iota(jnp.int32, (pw, pw), 0) >= P_C
    cbk = lax.broadcasted_iota(jnp.int32, (pw, pw), 1) >= N_C
    diag_blk = rb == cbk

    def scan_begin(z):
        dus = [(d, (z if d == 0 else nc - 1 - z), h) for d in range(2) for h in range(H_A)]
        s_prev = [st_s[d * H_A + h] for d, _, h in dus]
        us = []
        for d in range(2):
            zc = z if d == 0 else nc - 1 - z
            rows = chunk_rows(zc)
            sp = sp_s[rows, :]
            tot = tot_s[zc][0:1, :]
            gc = gc_s[d, rows, :]
            for g in range(G_C):
                ln = LANE_DT + d * H_C + g * HPG
                gsl = slice(g * pw, (g + 1) * pw)
                a_pair = jnp.where(lane_hi, gc[:, ln + 1:ln + 2], gc[:, ln:ln + 1])
                t_pair = jnp.where(lane_hi, tot[:, ln + 1:ln + 2], tot[:, ln:ln + 1])
                lmat = jnp.exp(jnp.where(incl2[d], a_pair - gcpair_s[d, zc, g:g + 1, :], -1e30))
                xdt = xbc_s[rows, gsl] * jnp.where(lane_hi, sp[:, ln + 1:ln + 2], sp[:, ln:ln + 1])
                b_dup = bdup_s[rows, gsl]
                c_dup = cdup_s[rows, gsl]
                us.append(dict(d=d, g=g, rows=rows, gsl=gsl, lmat=lmat, xdt=xdt, c_dup=c_dup,
                               c_lo=jnp.where(lane_hi, 0.0, c_dup), b_st=jnp.concatenate([b_dup, b_dup], axis=0),
                               bdec=b_dup * jnp.exp(t_pair - a_pair), ea=jnp.exp(a_pair),
                               dec=jnp.where(rb, jnp.exp(tot[:, ln + 1:ln + 2]), jnp.exp(tot[:, ln:ln + 1])),
                               x_bd=jnp.concatenate([jnp.where(lane_hi, 0.0, xdt), jnp.where(lane_hi, xdt, 0.0)],
                                                    axis=0),
                               h_prev=hs_s[d, g]))
        return dict(dus=dus, s_prev=s_prev, us=us)

    def scan_delta_1(t):
        t["ws_qs"] = [_dot(wq_s[idx], s) for idx, s in zip(t["dus"], t["s_prev"])]

    def scan_ssd_1(t):
        t["st"] = [_dot_tn(p["xdt"], p["bdec"]) for p in t["us"]]
        t["cb"] = [_dot_nt(p["c_lo"], p["b_st"]) for p in t["us"]]
        t["y_off"] = [_dot_nt(p["c_dup"], p["h_prev"]) for p in t["us"]]

    def scan_delta_2(t):
        v_new = [u_s[idx] - r[:CHUNK] for idx, r in zip(t["dus"], t["ws_qs"])]
        zero = jnp.zeros((CHUNK, DV_A), F32)
        t["o_in"] = [_dot(qk_s[zc, h], jnp.concatenate([v, zero] if d == 0 else [zero, v], axis=0))
                     for (d, zc, h), v in zip(t["dus"], v_new)]
        t["s_add"] = [_dot_tn(kd_s[idx], v) for idx, v in zip(t["dus"], v_new)]

    def scan_ssd_2(t):
        t["y_diag"] = [_dot(cbd * p["lmat"], p["x_bd"]) for p, cbd in zip(t["us"], t["cb"])]

    def scan_store(t):
        for p, yd, yo, s in zip(t["us"], t["y_diag"], t["y_off"], t["st"]):
            y_s[p["rows"], p["gsl"]] = y_s[p["rows"], p["gsl"]] + yd + yo * p["ea"]
            hs_s[p["d"], p["g"]] = p["h_prev"] * p["dec"] + jnp.where(diag_blk, s, 0.0)
        for (d, zc, h), r, oi, sa, s in zip(t["dus"], t["ws_qs"], t["o_in"], t["s_add"], t["s_prev"]):
            ln = LANE_ALPHA + d * H_A + h
            cs = slice(h * DV_A, (h + 1) * DV_A)
            o_s[chunk_rows(zc), cs] = o_s[chunk_rows(zc), cs] + r[CHUNK:] + oi
            st_s[d * H_A + h] = s * jnp.exp(tot_s[zc][0:1, ln:ln + 1]) + sa


    def finish(z, carry):
        rows = chunk_rows(z)
        for h in range(H_A):
            cs = slice(h * DV_A, (h + 1) * DV_A)
            o = _rms(o_s[rows, cs], lanev[2:3, :])
            o = o * _silu(proj[0, rows, COL_GATE + h * DV_A:COL_GATE + (h + 1) * DV_A])
            ycat[0, rows, cs] = o.astype(ycat.dtype)
        y = y_s[rows, :] + ssdv[0:1, :] * xbc_s[rows, 0:C_W]
        y = _rms(y * _silu(proj[0, rows, COL_Z:COL_Z + C_W]), ssdv[1:2, :])
        ycat[0, rows, A_W + B_W:] = y.astype(ycat.dtype)
        return carry

    def emit():
        if not emit_state:
            return
        slots = (None,) if state_aliased else range(state_layers)
        for slot in slots:
            dn_slot = sdn_out.at[0] if slot is None else sdn_out.at[0, slot]
            ssd_slot = sssd_out.at[0] if slot is None else sssd_out.at[0, slot]
            mine = slot is None or slot == state_layer
            for d in range(2):
                for h in range(H_A):
                    dn_slot[d, h] = st_s[d * H_A + h] if mine else jnp.zeros((DK_A, DV_A), F32)
                for h in range(H_C):
                    k0 = (h % HPG) * P_C
                    ssd_slot[d, h] = (hs_s[d, h // HPG][k0:k0 + P_C, k0:k0 + N_C] if mine
                                      else jnp.zeros((P_C, N_C), F32))

    return types.SimpleNamespace(
        nc=nc, cpi=cpi, prep=prep, init=init, delta_begin=delta_begin, delta_stages=delta_stages,
        scan_begin=scan_begin,
        scan_stages=(scan_delta_1, scan_ssd_1, scan_delta_2, scan_ssd_2, scan_store), finish=finish, emit=emit)


def _mixer(proj, dn_conv_w, sc_conv_w, ssd_conv_w, lanev, ssdv, layer, s_dn0, s_ssd0, state_out):
    nb, seq_len, _ = proj.shape
    nc = seq_len // CHUNK
    assert nc % 2 == 0
    zero_init = s_dn0 is None
    spg = 2 if nc <= STATIC_PREP_CHUNKS and nb % 2 == 0 else 1
    full = lambda a: pl.BlockSpec((None,) + a.shape[1:], lambda b: (layer,) + (0,) * (a.ndim - 1))
    args = [proj, dn_conv_w, sc_conv_w, ssd_conv_w, lanev, ssdv]
    batched_in = [0]
    in_specs = [pl.BlockSpec((spg, seq_len, D_IN_PAD), lambda b: (b, 0, 0), pipeline_mode=pl.Buffered(1))
                if seq_len > 512 else pl.BlockSpec((spg, seq_len, D_IN_PAD), lambda b: (b, 0, 0)),
                full(dn_conv_w), full(sc_conv_w), full(ssd_conv_w), full(lanev), full(ssdv)]
    if not zero_init:
        batched_in += [len(args), len(args) + 1]
        args += [s_dn0, s_ssd0]
        in_specs += [pl.BlockSpec((spg, 2, H_A, DK_A, DV_A), lambda b: (b, 0, 0, 0, 0)),
                     pl.BlockSpec((spg, 2, H_C, P_C, N_C), lambda b: (b, 0, 0, 0, 0))]
    out_shape = [jax.ShapeDtypeStruct((nb, seq_len, D_MODEL), BF16)]
    out_specs = [pl.BlockSpec((spg, seq_len, D_MODEL), lambda b: (b, 0, 0))]
    aliases = {}
    state_layer, depth, prev = state_out if state_out is not None else (None, None, None)
    assert state_layer in (None, layer)
    if state_out is not None:
        out_shape += [jax.ShapeDtypeStruct((nb, depth, 2, H_A, DK_A, DV_A), F32),
                      jax.ShapeDtypeStruct((nb, depth, 2, H_C, P_C, N_C), F32)]
        if prev is None:
            out_specs += [pl.BlockSpec((spg, depth, 2, H_A, DK_A, DV_A), lambda b: (b, 0, 0, 0, 0, 0)),
                          pl.BlockSpec((spg, depth, 2, H_C, P_C, N_C), lambda b: (b, 0, 0, 0, 0, 0))]
        else:
            aliases = {len(args): 1, len(args) + 1: 2}
            args += list(prev)
            in_specs += [pl.BlockSpec(memory_space=pl.ANY), pl.BlockSpec(memory_space=pl.ANY)]
            out_specs += [pl.BlockSpec((spg, None, 2, H_A, DK_A, DV_A), lambda b: (b, layer, 0, 0, 0, 0)),
                          pl.BlockSpec((spg, None, 2, H_C, P_C, N_C), lambda b: (b, layer, 0, 0, 0, 0))]
    scratch = [pltpu.VMEM((seq_len, 3 * A_W), F32),
               pltpu.VMEM((seq_len, C_W), F32),
               pltpu.VMEM((seq_len, G_C * LANES), F32),
               pltpu.VMEM((seq_len, G_C * LANES), F32),
               pltpu.VMEM((2, seq_len, LANES), F32),
               pltpu.VMEM((nc, 8, LANES), F32),
               pltpu.VMEM((2, nc, 8, LANES), F32),
               pltpu.VMEM((nc, 8, LANES), F32),
               pltpu.VMEM((seq_len, LANES), F32),
               pltpu.VMEM((seq_len, LANES), F32),
               pltpu.VMEM((2, nc, H_A, CHUNK, DV_A), F32),
               pltpu.VMEM((2, nc, H_A, 2 * CHUNK, DK_A), BF16),
               pltpu.VMEM((2, nc, H_A, CHUNK, DK_A), BF16),
               pltpu.VMEM((nc, H_A, CHUNK, 2 * CHUNK), BF16),
               pltpu.VMEM((seq_len, A_W), F32),
               pltpu.VMEM((seq_len, C_W), F32),
               pltpu.VMEM((2 * H_A, DK_A, DV_A), F32),
               pltpu.VMEM((2, G_C, HPG * P_C, HPG * N_C), F32)]
    return pl.pallas_call(
        functools.partial(_mixer_kernel, seqs_per_step=spg, batched_in=tuple(batched_in), n_in=len(args),
                          n_out=len(out_shape), seq_len=seq_len, zero_init=zero_init, state_layer=state_layer,
                          state_layers=depth, state_aliased=prev is not None),
        grid=(nb // spg,),
        in_specs=in_specs,
        out_specs=out_specs,
        out_shape=out_shape,
        input_output_aliases=aliases,
        scratch_shapes=scratch * spg,
        compiler_params=_params(("arbitrary",)),
        name="mixer",
    )(*args)


def _top2_sum(a, b, c, d):
    hi1, lo1 = jnp.maximum(a, b), jnp.minimum(a, b)
    hi2, lo2 = jnp.maximum(c, d), jnp.minimum(c, d)
    return jnp.maximum(hi1, hi2) + jnp.maximum(jnp.minimum(hi1, hi2), jnp.maximum(lo1, lo2))


def _outproj_kernel(y_ref, x_ref, mod_ref, w_ref, g_ref, rw_ref, rb_ref, x1_ref, h2_ref, comb_ref, wb_ref):
    @pl.when(pl.program_id(0) == 0)
    def _():
        wb_ref[...] = w_ref[...].astype(wb_ref.dtype)

    m = mod_ref[0]
    gate1, shift2, scale2 = m[2:3, :], m[3:4, :], m[4:5, :]
    x1 = x_ref[...] + gate1 * _dot(y_ref[...], wb_ref[...])
    x1_ref[...] = x1
    h2 = _rms(x1, g_ref[...]) * (1.0 + scale2) + shift2
    h2_ref[...] = h2.astype(h2_ref.dtype)

    h_hi = h2.astype(BF16)
    h_lo = (h2 - h_hi.astype(F32)).astype(BF16)
    rw_hi, rw_lo = rw_ref[0], rw_ref[1]
    hl = jnp.dot(jnp.concatenate([h_hi, h_lo], axis=0), rw_hi, preferred_element_type=F32)
    tm = h2.shape[0]
    logits = hl[:tm] + hl[tm:] + jnp.dot(h_hi, rw_lo, preferred_element_type=F32)
    scores = _sigmoid(logits.T[:N_EXPERTS, :])
    biased = scores + rb_ref[...]
    sc = [scores[e:e + 1, :] for e in range(N_EXPERTS)]
    bi = [biased[e:e + 1, :] for e in range(N_EXPERTS)]
    gs = [_top2_sum(*bi[EPG * g:EPG * (g + 1)]) for g in range(N_GROUPS)]
    gmax = functools.reduce(jnp.maximum, gs)
    first = []
    taken = None
    for g in range(N_GROUPS):
        hit = gs[g] == gmax
        if taken is None:
            first.append(hit)
            taken = hit
        else:
            first.append(hit & jnp.logical_not(taken))
            taken = taken | hit

    def pick(vals, j):
        out = vals[EPG * (N_GROUPS - 1) + j]
        for g in range(N_GROUPS - 2, -1, -1):
            out = jnp.where(first[g], vals[EPG * g + j], out)
        return out

    ib = [pick(bi, j) for j in range(EPG)]
    isc = [pick(sc, j) for j in range(EPG)]
    sel = []
    for j in range(EPG):
        cnt = jnp.zeros_like(ib[j])
        for i in range(EPG):
            if i == j:
                continue
            ahead = (ib[i] > ib[j]) | ((ib[i] == ib[j]) if i < j else False)
            cnt = cnt + jnp.where(ahead, 1.0, 0.0)
        sel.append(cnt < 2.0)
    wj = [jnp.where(sel[j], isc[j], 0.0) for j in range(EPG)]
    denom = functools.reduce(lambda a, b: a + b, wj)
    for g in range(N_GROUPS):
        for j in range(EPG):
            comb_ref[EPG * g + j:EPG * g + j + 1, :] = jnp.where(first[g], wj[j] / denom, 0.0)


def _outproj(ycat, x, mod6, seq_rows, w_out, layer, norm_g, router_w_pad, router_b):
    t, d = x.shape
    tm = _tile(seq_rows, 512)
    return pl.pallas_call(
        _outproj_kernel,
        grid=(t // tm,),
        in_specs=[pl.BlockSpec((tm, d), lambda i: (i, 0)),
                  pl.BlockSpec((tm, d), lambda i: (i, 0)),
                  pl.BlockSpec((1, 6, d), lambda i: ((i * tm) // seq_rows, 0, 0)),
                  pl.BlockSpec((None, d, d), lambda i: (layer, 0, 0)),
                  pl.BlockSpec((1, d), lambda i: (0, 0)),
                  pl.BlockSpec((2, d, LANES), lambda i: (0, 0, 0)),
                  pl.BlockSpec((N_EXPERTS, 1), lambda i: (0, 0))],
        out_specs=[pl.BlockSpec((tm, d), lambda i: (i, 0)),
                   pl.BlockSpec((tm, d), lambda i: (i, 0)),
                   pl.BlockSpec((N_EXPERTS, tm), lambda i: (0, i))],
        out_shape=[jax.ShapeDtypeStruct((t, d), F32),
                   jax.ShapeDtypeStruct((t, d), BF16),
                   jax.ShapeDtypeStruct((N_EXPERTS, t), F32)],
        scratch_shapes=[pltpu.VMEM((d, d), BF16)],
        compiler_params=_params(("arbitrary",)),
        name="outproj_route",
    )(ycat, x, mod6, w_out, norm_g.reshape(1, d), router_w_pad, router_b.reshape(N_EXPERTS, 1))


def _moe_kernel(h_ref, comb_ref, wg_ref, wu_ref, wd_ref, x1_ref, mod_ref, fg_ref, o_ref, *, final):
    g = pl.program_id(1)

    @pl.when(g == 0)
    def _():
        o_ref[...] = jnp.zeros_like(o_ref)

    h = h_ref[...]
    comb = comb_ref[...]
    acts = [(_silu(_dot(h, wg_ref[j])) * _dot(h, wu_ref[j]) * comb[:, j:j + 1]).astype(BF16) for j in range(EPG)]
    o_ref[...] += _dot(jnp.concatenate(acts, axis=1), wd_ref[...])

    @pl.when(g == pl.num_programs(1) - 1)
    def _():
        x2 = x1_ref[...] + mod_ref[0][5:6, :] * o_ref[...]
        o_ref[...] = _rms(x2, fg_ref[...]) if final else x2


def _moe(h2, comb, w_gate, w_up, w_down, layer, x1, mod6, seq_rows, final_g, final):
    t, d = x1.shape
    tm = _tile(seq_rows, 1024)
    return pl.pallas_call(
        functools.partial(_moe_kernel, final=final),
        grid=(t // tm, N_GROUPS),
        in_specs=[pl.BlockSpec((tm, d), lambda i, g: (i, 0)),
                  pl.BlockSpec((None, tm, EPG), lambda i, g: (g, i, 0)),
                  pl.BlockSpec((None, EPG, d, D_EXPERT), lambda i, g: (layer, g, 0, 0)),
                  pl.BlockSpec((None, EPG, d, D_EXPERT), lambda i, g: (layer, g, 0, 0)),
                  pl.BlockSpec((None, None, EPG * D_EXPERT, d), lambda i, g: (layer, g, 0, 0)),
                  pl.BlockSpec((tm, d), lambda i, g: (i, 0)),
                  pl.BlockSpec((1, 6, d), lambda i, g: ((i * tm) // seq_rows, 0, 0)),
                  pl.BlockSpec((1, d), lambda i, g: (0, 0))],
        out_specs=pl.BlockSpec((tm, d), lambda i, g: (i, 0)),
        out_shape=jax.ShapeDtypeStruct((t, d), F32),
        compiler_params=_params(("arbitrary", "arbitrary")),
        name="experts",
    )(h2, comb, w_gate, w_up, w_down, x1, mod6, final_g.reshape(1, d))


def _lane_rows(dn, ssd):
    depth = dn.shape[0]
    pad = lambda n: jnp.zeros((depth, n), F32)
    return jnp.concatenate([pad(LANE_ALPHA), dn.reshape(depth, -1).astype(F32), ssd.reshape(depth, -1).astype(F32),
                            pad(LANES - LANE_DT - 2 * H_C)], axis=1)


def kernel(x_prompt, x_sample, state_delta, state_ssd, c, c_ctx, mod_w, mod_b, norm1_g, norm2_g, w_in, w_out,
           dn_conv_w, dn_a_log, dn_dt_bias, dn_norm_g, sc_conv_w, ssd_conv_w, ssd_a_log, ssd_dt_bias, ssd_d,
           ssd_norm_g, router_w, router_b, exp_w_gate, exp_w_up, exp_w_down, final_norm_g):
    depth = mod_w.shape[0]
    n_ctx, seq, d = x_prompt.shape
    n_dec, dec_seq, _ = x_sample.shape
    assert seq % CHUNK == 0 and dec_seq % CHUNK == 0 and d == D_MODEL

    n_rows = -(-(1 + n_dec) // 8) * 8
    cond_rows = jnp.zeros((n_rows, d), F32).at[0].set(c_ctx).at[1:1 + n_dec].set(c)
    mod = _modulation(cond_rows, mod_w, mod_b).reshape(depth, n_rows, 6, d)

    w_in_b = w_in.astype(BF16)
    w_in_r = jnp.concatenate(
        [w_in_b[:, :, :SRC_SMALL_A], w_in_b[:, :, SRC_SCH:SRC_SMALL_B], w_in_b[:, :, SRC_SMALL_A:SRC_SCH],
         w_in_b[:, :, SRC_SMALL_B:], jnp.zeros((depth, d, D_IN_PAD - D_IN), BF16)], axis=-1)
    w_down_g = exp_w_down.reshape(depth, N_GROUPS, EPG * D_EXPERT, d)
    rw = jnp.pad(router_w.astype(F32), ((0, 0), (0, LANES - N_EXPERTS)))
    rw_hi = rw.astype(BF16)
    router_w_pad = jnp.stack([rw_hi, (rw - rw_hi.astype(F32)).astype(BF16)])

    xp = x_prompt.reshape(n_ctx * seq, d)
    pos = jnp.asarray(_grid_pos_embed(dec_seq, d), dtype=x_sample.dtype)
    xs = _add_pos(x_sample, pos).reshape(n_dec * dec_seq, d)

    lanev = jnp.stack([_lane_rows(dn_a_log, ssd_a_log), _lane_rows(dn_dt_bias, ssd_dt_bias), dn_norm_g.astype(F32)]
                      + [jnp.zeros((depth, LANES), F32)] * 5, axis=1)
    ssdv = jnp.stack([jnp.repeat(ssd_d.astype(F32), P_C, axis=1), ssd_norm_g.astype(F32)]
                     + [jnp.zeros((depth, C_W), F32)] * 6, axis=1)

    states = None
    for l in range(depth):
        final = l == depth - 1

        def block(x, mod6, nb, seq_len, seq_rows, s_dn0, s_ssd0, state_out):
            proj = _inproj(x, mod6, seq_rows, norm1_g[l], w_in_r, l).reshape(nb, seq_len, D_IN_PAD)
            outs = _mixer(proj, dn_conv_w, sc_conv_w, ssd_conv_w, lanev, ssdv, l, s_dn0, s_ssd0, state_out)
            ycat = outs[0].reshape(nb * seq_len, d)
            x1, h2, comb_t = _outproj(ycat, x, mod6, seq_rows, w_out, l, norm2_g[l], router_w_pad, router_b)
            comb = comb_t.reshape(N_GROUPS, EPG, -1).transpose(0, 2, 1)
            x2 = _moe(h2, comb, exp_w_gate, exp_w_up, w_down_g, l, x1, mod6, seq_rows, final_norm_g, final)
            return x2, outs[1:]

        xp, states = block(xp, mod[l, 0:1], n_ctx, seq, n_ctx * seq, None, None, (l, depth, states))
        xs, _ = block(xs, mod[l, 1:1 + n_dec], n_dec, dec_seq, dec_seq,
                      state_delta[:, l].astype(F32), state_ssd[:, l].astype(F32), None)

    return (xp.reshape(n_ctx, seq, d), xs.reshape(n_dec, dec_seq, d), states[0], states[1])
```

```python
import functools
import math
import types

import jax
import jax.numpy as jnp
import numpy as np
from jax import lax
from jax.experimental import pallas as pl
from jax.experimental.pallas import tpu as pltpu

F32 = jnp.float32
BF16 = jnp.bfloat16

D_MODEL = 1024
GRID_W = 64
POS_BASE = 10000.0
H_A, DK_A, DV_A = 4, 128, 128
A_W = H_A * DV_A
H_C, P_C, N_C, G_C = 4, 64, 64, 2
HPG = H_C // G_C
C_W = H_C * P_C
B_W = D_MODEL - A_W - C_W
XBC_W = C_W + 2 * G_C * N_C
CHUNK = 64
N_EXPERTS = 16
N_GROUPS = 4
EPG = N_EXPERTS // N_GROUPS
D_EXPERT = 256
EPS = 1e-6
LANES = 128

COL_Q, COL_K, COL_V, COL_GATE = 0, A_W, 2 * A_W, 3 * A_W
COL_SCH = 4 * A_W
COL_SCB = COL_SCH + B_W
COL_SCC = COL_SCB + B_W
COL_Z = COL_SCC + B_W
COL_XBC = COL_Z + C_W
COL_SMALL = COL_XBC + XBC_W
D_IN_PAD = COL_SMALL + LANES
LANE_BETA, LANE_ALPHA, LANE_DT = 0, 2 * H_A, 4 * H_A
SRC_SMALL_A = 4 * A_W
SRC_SCH = SRC_SMALL_A + 4 * H_A
SRC_SMALL_B = SRC_SCH + 3 * B_W + C_W + XBC_W
D_IN = SRC_SMALL_B + 2 * H_C

assert HPG == 2 and P_C == N_C == CHUNK and HPG * P_C == LANES and G_C * N_C == LANES and DK_A == DV_A == LANES

V7X_VMEM_BYTES = 64 * 1024 * 1024
VMEM_LIMIT = V7X_VMEM_BYTES * 7 // 8
STATIC_PREP_CHUNKS = 4


def _dot(a, b):
    return jnp.dot(a.astype(BF16), b.astype(BF16), preferred_element_type=F32)


def _dot_nt(a, b):
    return lax.dot_general(a.astype(BF16), b.astype(BF16), (((1,), (1,)), ((), ())), preferred_element_type=F32)


def _dot_tn(a, b):
    return lax.dot_general(a.astype(BF16), b.astype(BF16), (((0,), (0,)), ((), ())), preferred_element_type=F32)


def _dot_f32(a, b):
    return jnp.dot(a, b, precision=lax.Precision.HIGHEST, preferred_element_type=F32)


def _dot_nt_f32(a, b):
    return lax.dot_general(a, b, (((1,), (1,)), ((), ())), precision=lax.Precision.HIGHEST,
                           preferred_element_type=F32)


def _silu(x):
    h = 0.5 * x
    return h + h * jnp.tanh(h)


def _sigmoid(x):
    return 1.0 / (1.0 + jnp.exp(-x))


def _softplus(x):
    return jnp.maximum(x, 0.0) + jnp.log1p(jnp.exp(-jnp.abs(x)))


def _rms(x, g):
    return x * lax.rsqrt(jnp.mean(x * x, axis=-1, keepdims=True) + EPS) * g


def _tile(n, pref):
    t = min(n, pref)
    while n % t:
        t -= 8
    assert t > 0 and t % 8 == 0, (n, pref)
    return t


def _params(sem):
    return pltpu.CompilerParams(dimension_semantics=sem, vmem_limit_bytes=VMEM_LIMIT)


def _mod_kernel(cond_ref, w_ref, b_ref, o_ref):
    s = _silu(cond_ref[...])
    o_ref[0] = _dot_f32(s, w_ref[0]) + b_ref[0]


def _modulation(cond_rows, mod_w, mod_b):
    depth, d, n = mod_w.shape
    r = cond_rows.shape[0]
    tn = _tile(n, 1536)
    return pl.pallas_call(
        _mod_kernel,
        grid=(depth, n // tn),
        in_specs=[pl.BlockSpec((r, d), lambda l, j: (0, 0)),
                  pl.BlockSpec((1, d, tn), lambda l, j: (l, 0, j)),
                  pl.BlockSpec((1, 1, tn), lambda l, j: (l, 0, j))],
        out_specs=pl.BlockSpec((1, r, tn), lambda l, j: (l, 0, j)),
        out_shape=jax.ShapeDtypeStruct((depth, r, n), F32),
        compiler_params=_params(("arbitrary", "arbitrary")),
        name="modulation",
    )(cond_rows, mod_w, mod_b.reshape(depth, 1, n))


def _add_kernel(x_ref, p_ref, o_ref):
    o_ref[0] = x_ref[0] + p_ref[...]


def _add_pos(x, pos):
    nb, l, d = x.shape
    tl = _tile(l, 512)
    return pl.pallas_call(
        _add_kernel,
        grid=(nb, l // tl),
        in_specs=[pl.BlockSpec((1, tl, d), lambda b, i: (b, i, 0)),
                  pl.BlockSpec((tl, d), lambda b, i: (i, 0))],
        out_specs=pl.BlockSpec((1, tl, d), lambda b, i: (b, i, 0)),
        out_shape=jax.ShapeDtypeStruct(x.shape, x.dtype),
        compiler_params=_params(("arbitrary", "arbitrary")),
        name="add_pos",
    )(x, pos)


def _grid_pos_embed(n_tok, dim):
    rows = n_tok // GRID_W
    rr, cc = np.meshgrid(np.arange(rows, dtype=np.float64), np.arange(GRID_W, dtype=np.float64), indexing="ij")
    quarter = dim // 4
    omega = 1.0 / (POS_BASE ** (np.arange(quarter, dtype=np.float64) / quarter))
    ang_r = rr.reshape(-1, 1) * omega
    ang_c = cc.reshape(-1, 1) * omega
    return np.concatenate([np.sin(ang_r), np.cos(ang_r), np.sin(ang_c), np.cos(ang_c)], axis=-1)


def _inproj_kernel(x_ref, mod_ref, g_ref, w_ref, o_ref):
    m = mod_ref[0]
    shift, scale = m[0:1, :], m[1:2, :]
    h = _rms(x_ref[...], g_ref[...]) * (1.0 + scale) + shift
    o_ref[...] = _dot(h, w_ref[...])


def _inproj(x, mod6, seq_rows, norm_g, w_in_r, layer):
    t, d = x.shape
    tm = _tile(seq_rows, 512)
    return pl.pallas_call(
        _inproj_kernel,
        grid=(t // tm,),
        in_specs=[pl.BlockSpec((tm, d), lambda i: (i, 0)),
                  pl.BlockSpec((1, 6, d), lambda i: ((i * tm) // seq_rows, 0, 0)),
                  pl.BlockSpec((1, d), lambda i: (0, 0)),
                  pl.BlockSpec((None, d, D_IN_PAD), lambda i: (layer, 0, 0))],
        out_specs=pl.BlockSpec((tm, D_IN_PAD), lambda i: (i, 0)),
        out_shape=jax.ShapeDtypeStruct((t, D_IN_PAD), F32),
        compiler_params=_params(("arbitrary",)),
        name="inproj",
    )(x, mod6, norm_g.reshape(1, d), w_in_r)


CONV_WIN = CHUNK + 16


def _conv_select(r0, w0):
    ti = lax.broadcasted_iota(jnp.int32, (2 * CHUNK, 2 * CONV_WIN), 0)
    tj = lax.broadcasted_iota(jnp.int32, (2 * CHUNK, 2 * CONV_WIN), 1)
    want = r0 + jnp.where(ti < CHUNK, ti - 1, ti - CHUNK + 1)
    have = w0 + jnp.where(tj < CONV_WIN, tj, tj - CONV_WIN)
    return jnp.where(want == have, 1.0, 0.0).astype(BF16)


def _conv3(load, r0, w0, sel, w):
    cur = load(pl.ds(r0, CHUNK))
    win = load(pl.ds(w0, CONV_WIN))
    hi = win.astype(BF16)
    lo = (win - hi.astype(F32)).astype(BF16)
    nb = jnp.dot(sel, jnp.concatenate([hi, lo], axis=0), preferred_element_type=F32)
    return w[0:1, :] * nb[:CHUNK] + w[1:2, :] * cur + w[2:3, :] * nb[CHUNK:]


def _mixer_kernel(*refs, seqs_per_step, batched_in, n_in, n_out, **static):
    ins, outs, scratch = refs[:n_in], refs[n_in:n_in + n_out], refs[n_in + n_out:]
    n_scr = len(scratch) // seqs_per_step
    seqs = []
    for s in range(seqs_per_step):
        one = lambda r, s=s: r.at[pl.ds(s, 1)]
        seqs.append(_mixer_phases([one(r) if i in batched_in else r for i, r in enumerate(ins)]
                                  + [one(r) for r in outs] + list(scratch[s * n_scr:(s + 1) * n_scr]), **static))
    nc, cpi = seqs[0].nc, seqs[0].cpi
    if nc <= STATIC_PREP_CHUNKS:
        for q in seqs:
            for z in range(nc):
                q.prep(z, 0)
            q.init()
        for i in range(nc // cpi):
            _staged(seqs, "delta", i)
        for z in range(nc):
            _staged(seqs, "scan", z)
        for q in seqs:
            for z in range(nc):
                q.finish(z, 0)
    else:
        for q in seqs:
            lax.fori_loop(0, nc, q.prep, 0, unroll=2)
            q.init()

        def staged(phase):
            def body(i, carry):
                _staged(seqs, phase, i)
                return carry
            return body

        lax.fori_loop(0, nc // cpi, staged("delta"), 0)
        lax.fori_loop(0, nc, staged("scan"), 0, unroll=8)
        for q in seqs:
            lax.fori_loop(0, nc, q.finish, 0, unroll=2)
    for q in seqs:
        q.emit()


def _staged(seqs, phase, i):
    begin, stages = f"{phase}_begin", f"{phase}_stages"
    states = [getattr(q, begin)(i) for q in seqs]
    for stage in range(len(getattr(seqs[0], stages))):
        for q, st in zip(seqs, states):
            getattr(q, stages)[stage](st)


def _mixer_phases(refs, seq_len, zero_init, state_layer, state_layers, state_aliased):
    refs = list(refs)
    proj, dnw, scw, ssw, lanev, ssdv = refs[:6]
    k = 6
    if not zero_init:
        sdn0, sssd0 = refs[k:k + 2]
        k += 2
    emit_state = state_layer is not None
    if state_aliased:
        k += 2
    ycat = refs[k]
    k += 1
    if emit_state:
        sdn_out, sssd_out = refs[k:k + 2]
        k += 2
    (qkv_s, xbc_s, bdup_s, cdup_s, gc_s, dpair_s, gcpair_s, tot_s, sp_s, beta_s, u_s, wq_s, kd_s, qk_s, o_s, y_s,
     st_s, hs_s) = refs[k:]

    nc = seq_len // CHUNK
    ri = lax.broadcasted_iota(jnp.int32, (CHUNK, CHUNK), 0)
    ci = lax.broadcasted_iota(jnp.int32, (CHUNK, CHUNK), 1)
    tril = (ri >= ci).astype(F32)
    r128 = lax.broadcasted_iota(jnp.int32, (LANES, LANES), 0)
    c128 = lax.broadcasted_iota(jnp.int32, (LANES, LANES), 1)
    eye128 = (r128 == c128).astype(F32)
    n_lvl = int(math.log2(CHUNK))
    alog = lanev[0:1, :]
    bias = lanev[1:2, :]
    pw = 2 * CHUNK
    ri2 = lax.broadcasted_iota(jnp.int32, (CHUNK, pw), 0)
    ci2 = lax.broadcasted_iota(jnp.int32, (CHUNK, pw), 1)
    lane_hi = ci2 >= CHUNK
    tj = ci2 & (CHUNK - 1)
    incl2 = (ri2 >= tj, ri2 <= tj)
    ahead = jnp.where(lane_hi, tj - ri2, ri2 - tj)
    incl_fb = ahead >= 0
    strict_fb = ahead > 0
    eye2 = (ri2 == tj).astype(F32)
    lvl2 = [((ri2 >> s) == (tj >> s)) & ((ri2 >> (s - 1)) != (tj >> (s - 1))) for s in range(1, n_lvl + 1)]

    def block_diag(x):
        return jnp.concatenate([jnp.where(lane_hi, 0.0, x), jnp.where(lane_hi, x, 0.0)], axis=0).astype(BF16)

    def chunk_rows(z):
        return pl.ds(z * CHUNK if isinstance(z, int) else pl.multiple_of(z * CHUNK, CHUNK), CHUNK)

    def prep(z, carry):
        r0 = z * CHUNK if isinstance(z, int) else pl.multiple_of(z * CHUNK, CHUNK)
        rows = pl.ds(r0, CHUNK)
        if isinstance(z, int):
            w0 = min(max(r0 - 8, 0), seq_len - CONV_WIN)
        else:
            w0 = pl.multiple_of(jnp.clip(r0 - 8, 0, seq_len - CONV_WIN), 8)
        sel = _conv_select(r0, w0)
        cw = 2 * LANES
        for jj in range(3 * A_W // cw):
            c0 = jj * cw
            a2 = _silu(_conv3(lambda rs: proj[0, rs, c0:c0 + cw], r0, w0, sel, dnw[:, c0:c0 + cw]))
            for half in range(cw // LANES):
                a = a2[:, half * LANES:(half + 1) * LANES]
                j = jj * (cw // LANES) + half
                if j < 2 * H_A:
                    a = a * lax.rsqrt(jnp.sum(a * a, axis=-1, keepdims=True) + EPS)
                if j < H_A:
                    a = a * (DK_A ** -0.5)
                qkv_s[rows, j * LANES:(j + 1) * LANES] = a
        for jj in range(B_W // cw):
            c0 = jj * cw
            cv = _conv3(lambda rs: proj[0, rs, COL_SCC + c0:COL_SCC + c0 + cw]
                        * proj[0, rs, COL_SCH + c0:COL_SCH + c0 + cw], r0, w0, sel, scw[:, c0:c0 + cw])
            yb = proj[0, rows, COL_SCB + c0:COL_SCB + c0 + cw] * cv
            ycat[0, rows, A_W + c0:A_W + c0 + cw] = yb.astype(ycat.dtype)
        for jj in range(XBC_W // cw):
            c1 = jj * cw
            a2 = _silu(_conv3(lambda rs: proj[0, rs, COL_XBC + c1:COL_XBC + c1 + cw], r0, w0, sel,
                              ssw[:, c1:c1 + cw]))
            if c1 < C_W:
                xbc_s[rows, c1:c1 + cw] = a2
                continue
            for half, dup_s in enumerate((bdup_s, cdup_s)):
                a = a2[:, half * LANES:(half + 1) * LANES]
                swapped = pltpu.roll(a, N_C, 1)
                lo = lax.broadcasted_iota(jnp.int32, a.shape, 1) < N_C
                dup_s[rows, 0:LANES] = jnp.where(lo, a, swapped)
                dup_s[rows, LANES:2 * LANES] = jnp.where(lo, swapped, a)
        sm = proj[0, rows, COL_SMALL:COL_SMALL + LANES]
        sp = _softplus(sm + bias)
        g = -jnp.exp(alog) * sp
        pre = _dot_f32(tril, g)
        tot = pre[CHUNK - 1:CHUNK, :]
        suf = tot - pre + g
        sp_s[rows, :] = sp
        beta_s[rows, :] = _sigmoid(sm)
        gc_s[0, rows, :] = pre
        gc_s[1, rows, :] = suf
        grs = [_dot_nt_f32(eye128, gcd) for gcd in (pre, suf)]
        for d, gr in enumerate(grs):
            for g in range(G_C):
                ln = LANE_DT + d * H_C + g * HPG
                gcpair_s[d, z, g:g + 1, :] = jnp.concatenate([gr[ln:ln + 1, :], gr[ln + 1:ln + 2, :]], axis=1)
        for h in range(H_A):
            lf, lb = LANE_ALPHA + h, LANE_ALPHA + H_A + h
            dpair_s[z, h:h + 1, :] = jnp.concatenate([grs[0][lf:lf + 1, :], grs[1][lb:lb + 1, :]], axis=1)
        tot_s[z] = jnp.broadcast_to(tot, (8, LANES))
        o_s[rows, :] = jnp.zeros((CHUNK, A_W), F32)
        y_s[rows, :] = jnp.zeros((CHUNK, C_W), F32)
        return carry

    def init():
        for d in range(2):
            for h in range(H_A):
                st_s[d * H_A + h] = jnp.zeros((DK_A, DV_A), F32) if zero_init else sdn0[0, d, h]
            for g in range(G_C):
                if zero_init:
                    hs_s[d, g] = jnp.zeros((HPG * P_C, HPG * N_C), F32)
                else:
                    zero = jnp.zeros((P_C, N_C), F32)
                    hs_s[d, g] = jnp.concatenate(
                        [jnp.concatenate([sssd0[0, d, g * HPG], zero], axis=1),
                         jnp.concatenate([zero, sssd0[0, d, g * HPG + 1]], axis=1)], axis=0)

    cpi = 4 if nc % 4 == 0 else 2

    def delta_begin(i):
        units = []
        for zz in range(cpi):
            z = cpi * i + zz
            rows = chunk_rows(z)
            tot = tot_s[z][0:1, :]
            beta = beta_s[rows, :]
            gcs = [gc_s[d, rows, :] for d in range(2)]
            for h in range(H_A):
                q_h = qkv_s[rows, COL_Q + h * DK_A:COL_Q + (h + 1) * DK_A]
                k_h = qkv_s[rows, COL_K + h * DK_A:COL_K + (h + 1) * DK_A]
                v_h = qkv_s[rows, COL_V + h * DV_A:COL_V + (h + 1) * DV_A]
                units.append(dict(z=z, h=h, q=q_h, k=k_h, v=v_h, tot=tot, beta=beta, gcs=gcs))
        return dict(units=units)

    def delta_gram(t):
        t["qkk"] = [_dot_nt(jnp.concatenate([p["q"], p["k"]], axis=0), jnp.concatenate([p["k"], p["k"]], axis=0))
                    for p in t["units"]]

    def delta_operands(t):
        ms, rhss = [], []
        for p, qk_kk in zip(t["units"], t["qkk"]):
            z, h = p["z"], p["h"]
            lf, lb = LANE_ALPHA + h, LANE_ALPHA + H_A + h
            bf, bb = LANE_BETA + h, LANE_BETA + H_A + h
            a_pair = jnp.where(lane_hi, p["gcs"][1][:, lb:lb + 1], p["gcs"][0][:, lf:lf + 1])
            b_pair = jnp.where(lane_hi, p["beta"][:, bb:bb + 1], p["beta"][:, bf:bf + 1])
            decay = jnp.exp(jnp.where(incl_fb, a_pair - dpair_s[z, h:h + 1, :], -1e30))
            ms.append(jnp.where(strict_fb, qk_kk[CHUNK:] * b_pair * decay, 0.0))
            qk_s[z, h] = (qk_kk[:CHUNK] * decay).astype(qk_s.dtype)
            rhs_d = []
            for d, (ln, bl) in enumerate(((lf, bf), (lb, bb))):
                a_col = p["gcs"][d][:, ln:ln + 1]
                b_col = p["beta"][:, bl:bl + 1]
                eg = jnp.exp(a_col)
                rhs_d.append(jnp.concatenate([p["v"] * b_col, p["k"] * (b_col * eg)], axis=1))
                kd_s[d, z, h] = (p["k"] * jnp.exp(p["tot"][:, ln:ln + 1] - a_col)).astype(kd_s.dtype)
                wq_s[d, z, h, CHUNK:, :] = (p["q"] * eg).astype(wq_s.dtype)
            zero = jnp.zeros_like(rhs_d[0])
            rhss.append(jnp.concatenate([jnp.concatenate([rhs_d[0], zero], axis=1),
                                         jnp.concatenate([zero, rhs_d[1]], axis=1)], axis=0))
        t.update(ms=ms, rhss=rhss, t_inv=[eye2 - jnp.where(lvl2[0], m, 0.0) for m in ms])

    def delta_level_a(s, t):
        t["x"] = [_dot(jnp.where(lvl2[s], m, 0.0), block_diag(ti)) for m, ti in zip(t["ms"], t["t_inv"])]

    def delta_level_b(t):
        t["t_inv"] = [ti - _dot(ti, block_diag(xx)) for ti, xx in zip(t["t_inv"], t["x"])]

    def delta_solve(t):
        t["uw"] = [_dot(ti, rhs) for ti, rhs in zip(t["t_inv"], t["rhss"])]

    def delta_store(t):
        for p, r in zip(t["units"], t["uw"]):
            z, h = p["z"], p["h"]
            for d in range(2):
                c0 = d * (DV_A + DK_A)
                u_s[d, z, h] = r[:, c0:c0 + DV_A]
                wq_s[d, z, h, :CHUNK, :] = r[:, c0 + DV_A:c0 + DV_A + DK_A].astype(wq_s.dtype)

    delta_stages = [delta_gram, delta_operands]
    for s in range(1, n_lvl):
        delta_stages += [functools.partial(delta_level_a, s), delta_level_b]
    delta_stages += [delta_solve, delta_store]

    rb = lax.broadcasted_iota(jnp.int32, (pw, pw), 0) >= P_C
    cbk = lax.broadcasted_iota(jnp.int32, (pw, pw), 1) >= N_C
    diag_blk = rb == cbk

    def scan_begin(z):
        dus = [(d, (z if d == 0 else nc - 1 - z), h) for d in range(2) for h in range(H_A)]
        s_prev = [st_s[d * H_A + h] for d, _, h in dus]
        us = []
        for d in range(2):
            zc = z if d == 0 else nc - 1 - z
            rows = chunk_rows(zc)
            sp = sp_s[rows, :]
            tot = tot_s[zc][0:1, :]
            gc = gc_s[d, rows, :]
            for g in range(G_C):
                ln = LANE_DT + d * H_C + g * HPG
                gsl = slice(g * pw, (g + 1) * pw)
                a_pair = jnp.where(lane_hi, gc[:, ln + 1:ln + 2], gc[:, ln:ln + 1])
                t_pair = jnp.where(lane_hi, tot[:, ln + 1:ln + 2], tot[:, ln:ln + 1])
                lmat = jnp.exp(jnp.where(incl2[d], a_pair - gcpair_s[d, zc, g:g + 1, :], -1e30))
                xdt = xbc_s[rows, gsl] * jnp.where(lane_hi, sp[:, ln + 1:ln + 2], sp[:, ln:ln + 1])
                b_dup = bdup_s[rows, gsl]
                c_dup = cdup_s[rows, gsl]
                us.append(dict(d=d, g=g, rows=rows, gsl=gsl, lmat=lmat, xdt=xdt, c_dup=c_dup,
                               c_lo=jnp.where(lane_hi, 0.0, c_dup), b_st=jnp.concatenate([b_dup, b_dup], axis=0),
                               bdec=b_dup * jnp.exp(t_pair - a_pair), ea=jnp.exp(a_pair),
                               dec=jnp.where(rb, jnp.exp(tot[:, ln + 1:ln + 2]), jnp.exp(tot[:, ln:ln + 1])),
                               x_bd=jnp.concatenate([jnp.where(lane_hi, 0.0, xdt), jnp.where(lane_hi, xdt, 0.0)],
                                                    axis=0),
                               h_prev=hs_s[d, g]))
        return dict(dus=dus, s_prev=s_prev, us=us)

    def scan_delta_1(t):
        t["ws_qs"] = [_dot(wq_s[idx], s) for idx, s in zip(t["dus"], t["s_prev"])]

    def scan_ssd_1(t):
        t["st"] = [_dot_tn(p["xdt"], p["bdec"]) for p in t["us"]]
        t["cb"] = [_dot_nt(p["c_lo"], p["b_st"]) for p in t["us"]]
        t["y_off"] = [_dot_nt(p["c_dup"], p["h_prev"]) for p in t["us"]]

    def scan_delta_2(t):
        v_new = [u_s[idx] - r[:CHUNK] for idx, r in zip(t["dus"], t["ws_qs"])]
        zero = jnp.zeros((CHUNK, DV_A), F32)
        t["o_in"] = [_dot(qk_s[zc, h], jnp.concatenate([v, zero] if d == 0 else [zero, v], axis=0))
                     for (d, zc, h), v in zip(t["dus"], v_new)]
        t["s_add"] = [_dot_tn(kd_s[idx], v) for idx, v in zip(t["dus"], v_new)]

    def scan_ssd_2(t):
        t["y_diag"] = [_dot(cbd * p["lmat"], p["x_bd"]) for p, cbd in zip(t["us"], t["cb"])]

    def scan_store(t):
        for p, yd, yo, s in zip(t["us"], t["y_diag"], t["y_off"], t["st"]):
            y_s[p["rows"], p["gsl"]] = y_s[p["rows"], p["gsl"]] + yd + yo * p["ea"]
            hs_s[p["d"], p["g"]] = p["h_prev"] * p["dec"] + jnp.where(diag_blk, s, 0.0)
        for (d, zc, h), r, oi, sa, s in zip(t["dus"], t["ws_qs"], t["o_in"], t["s_add"], t["s_prev"]):
            ln = LANE_ALPHA + d * H_A + h
            cs = slice(h * DV_A, (h + 1) * DV_A)
            o_s[chunk_rows(zc), cs] = o_s[chunk_rows(zc), cs] + r[CHUNK:] + oi
            st_s[d * H_A + h] = s * jnp.exp(tot_s[zc][0:1, ln:ln + 1]) + sa


    def finish(z, carry):
        rows = chunk_rows(z)
        for h in range(H_A):
            cs = slice(h * DV_A, (h + 1) * DV_A)
            o = _rms(o_s[rows, cs], lanev[2:3, :])
            o = o * _silu(proj[0, rows, COL_GATE + h * DV_A:COL_GATE + (h + 1) * DV_A])
            ycat[0, rows, cs] = o.astype(ycat.dtype)
        y = y_s[rows, :] + ssdv[0:1, :] * xbc_s[rows, 0:C_W]
        y = _rms(y * _silu(proj[0, rows, COL_Z:COL_Z + C_W]), ssdv[1:2, :])
        ycat[0, rows, A_W + B_W:] = y.astype(ycat.dtype)
        return carry

    def emit():
        if not emit_state:
            return
        slots = (None,) if state_aliased else range(state_layers)
        for slot in slots:
            dn_slot = sdn_out.at[0] if slot is None else sdn_out.at[0, slot]
            ssd_slot = sssd_out.at[0] if slot is None else sssd_out.at[0, slot]
            mine = slot is None or slot == state_layer
            for d in range(2):
                for h in range(H_A):
                    dn_slot[d, h] = st_s[d * H_A + h] if mine else jnp.zeros((DK_A, DV_A), F32)
                for h in range(H_C):
                    k0 = (h % HPG) * P_C
                    ssd_slot[d, h] = (hs_s[d, h // HPG][k0:k0 + P_C, k0:k0 + N_C] if mine
                                      else jnp.zeros((P_C, N_C), F32))

    return types.SimpleNamespace(
        nc=nc, cpi=cpi, prep=prep, init=init, delta_begin=delta_begin, delta_stages=delta_stages,
        scan_begin=scan_begin,
        scan_stages=(scan_delta_1, scan_ssd_1, scan_delta_2, scan_ssd_2, scan_store), finish=finish, emit=emit)


def _mixer(proj, dn_conv_w, sc_conv_w, ssd_conv_w, lanev, ssdv, layer, s_dn0, s_ssd0, state_out):
    nb, seq_len, _ = proj.shape
    nc = seq_len // CHUNK
    assert nc % 2 == 0
    zero_init = s_dn0 is None
    spg = 2 if nc <= STATIC_PREP_CHUNKS and nb % 2 == 0 else 1
    full = lambda a: pl.BlockSpec((None,) + a.shape[1:], lambda b: (layer,) + (0,) * (a.ndim - 1))
    args = [proj, dn_conv_w, sc_conv_w, ssd_conv_w, lanev, ssdv]
    batched_in = [0]
    in_specs = [pl.BlockSpec((spg, seq_len, D_IN_PAD), lambda b: (b, 0, 0), pipeline_mode=pl.Buffered(1))
                if seq_len > 512 else pl.BlockSpec((spg, seq_len, D_IN_PAD), lambda b: (b, 0, 0)),
                full(dn_conv_w), full(sc_conv_w), full(ssd_conv_w), full(lanev), full(ssdv)]
    if not zero_init:
        batched_in += [len(args), len(args) + 1]
        args += [s_dn0, s_ssd0]
        in_specs += [pl.BlockSpec((spg, 2, H_A, DK_A, DV_A), lambda b: (b, 0, 0, 0, 0)),
                     pl.BlockSpec((spg, 2, H_C, P_C, N_C), lambda b: (b, 0, 0, 0, 0))]
    out_shape = [jax.ShapeDtypeStruct((nb, seq_len, D_MODEL), BF16)]
    out_specs = [pl.BlockSpec((spg, seq_len, D_MODEL), lambda b: (b, 0, 0))]
    aliases = {}
    state_layer, depth, prev = state_out if state_out is not None else (None, None, None)
    assert state_layer in (None, layer)
    if state_out is not None:
        out_shape += [jax.ShapeDtypeStruct((nb, depth, 2, H_A, DK_A, DV_A), F32),
                      jax.ShapeDtypeStruct((nb, depth, 2, H_C, P_C, N_C), F32)]
        if prev is None:
            out_specs += [pl.BlockSpec((spg, depth, 2, H_A, DK_A, DV_A), lambda b: (b, 0, 0, 0, 0, 0)),
                          pl.BlockSpec((spg, depth, 2, H_C, P_C, N_C), lambda b: (b, 0, 0, 0, 0, 0))]
        else:
            aliases = {len(args): 1, len(args) + 1: 2}
            args += list(prev)
            in_specs += [pl.BlockSpec(memory_space=pl.ANY), pl.BlockSpec(memory_space=pl.ANY)]
            out_specs += [pl.BlockSpec((spg, None, 2, H_A, DK_A, DV_A), lambda b: (b, layer, 0, 0, 0, 0)),
                          pl.BlockSpec((spg, None, 2, H_C, P_C, N_C), lambda b: (b, layer, 0, 0, 0, 0))]
    scratch = [pltpu.VMEM((seq_len, 3 * A_W), F32),
               pltpu.VMEM((seq_len, C_W), F32),
               pltpu.VMEM((seq_len, G_C * LANES), F32),
               pltpu.VMEM((seq_len, G_C * LANES), F32),
               pltpu.VMEM((2, seq_len, LANES), F32),
               pltpu.VMEM((nc, 8, LANES), F32),
               pltpu.VMEM((2, nc, 8, LANES), F32),
               pltpu.VMEM((nc, 8, LANES), F32),
               pltpu.VMEM((seq_len, LANES), F32),
               pltpu.VMEM((seq_len, LANES), F32),
               pltpu.VMEM((2, nc, H_A, CHUNK, DV_A), F32),
               pltpu.VMEM((2, nc, H_A, 2 * CHUNK, DK_A), BF16),
               pltpu.VMEM((2, nc, H_A, CHUNK, DK_A), BF16),
               pltpu.VMEM((nc, H_A, CHUNK, 2 * CHUNK), BF16),
               pltpu.VMEM((seq_len, A_W), F32),
               pltpu.VMEM((seq_len, C_W), F32),
               pltpu.VMEM((2 * H_A, DK_A, DV_A), F32),
               pltpu.VMEM((2, G_C, HPG * P_C, HPG * N_C), F32)]
    return pl.pallas_call(
        functools.partial(_mixer_kernel, seqs_per_step=spg, batched_in=tuple(batched_in), n_in=len(args),
                          n_out=len(out_shape), seq_len=seq_len, zero_init=zero_init, state_layer=state_layer,
                          state_layers=depth, state_aliased=prev is not None),
        grid=(nb // spg,),
        in_specs=in_specs,
        out_specs=out_specs,
        out_shape=out_shape,
        input_output_aliases=aliases,
        scratch_shapes=scratch * spg,
        compiler_params=_params(("arbitrary",)),
        name="mixer",
    )(*args)


def _top2_sum(a, b, c, d):
    hi1, lo1 = jnp.maximum(a, b), jnp.minimum(a, b)
    hi2, lo2 = jnp.maximum(c, d), jnp.minimum(c, d)
    return jnp.maximum(hi1, hi2) + jnp.maximum(jnp.minimum(hi1, hi2), jnp.maximum(lo1, lo2))


def _outproj_kernel(y_ref, x_ref, mod_ref, w_ref, g_ref, rw_ref, rb_ref, x1_ref, h2_ref, comb_ref, wb_ref):
    @pl.when(pl.program_id(0) == 0)
    def _():
        wb_ref[...] = w_ref[...].astype(wb_ref.dtype)

    m = mod_ref[0]
    gate1, shift2, scale2 = m[2:3, :], m[3:4, :], m[4:5, :]
    x1 = x_ref[...] + gate1 * _dot(y_ref[...], wb_ref[...])
    x1_ref[...] = x1
    h2 = _rms(x1, g_ref[...]) * (1.0 + scale2) + shift2
    h2_ref[...] = h2.astype(h2_ref.dtype)

    h_hi = h2.astype(BF16)
    h_lo = (h2 - h_hi.astype(F32)).astype(BF16)
    rw_hi, rw_lo = rw_ref[0], rw_ref[1]
    hl = jnp.dot(jnp.concatenate([h_hi, h_lo], axis=0), rw_hi, preferred_element_type=F32)
    tm = h2.shape[0]
    logits = hl[:tm] + hl[tm:] + jnp.dot(h_hi, rw_lo, preferred_element_type=F32)
    scores = _sigmoid(logits.T[:N_EXPERTS, :])
    biased = scores + rb_ref[...]
    sc = [scores[e:e + 1, :] for e in range(N_EXPERTS)]
    bi = [biased[e:e + 1, :] for e in range(N_EXPERTS)]
    gs = [_top2_sum(*bi[EPG * g:EPG * (g + 1)]) for g in range(N_GROUPS)]
    gmax = functools.reduce(jnp.maximum, gs)
    first = []
    taken = None
    for g in range(N_GROUPS):
        hit = gs[g] == gmax
        if taken is None:
            first.append(hit)
            taken = hit
        else:
            first.append(hit & jnp.logical_not(taken))
            taken = taken | hit

    def pick(vals, j):
        out = vals[EPG * (N_GROUPS - 1) + j]
        for g in range(N_GROUPS - 2, -1, -1):
            out = jnp.where(first[g], vals[EPG * g + j], out)
        return out

    ib = [pick(bi, j) for j in range(EPG)]
    isc = [pick(sc, j) for j in range(EPG)]
    sel = []
    for j in range(EPG):
        cnt = jnp.zeros_like(ib[j])
        for i in range(EPG):
            if i == j:
                continue
            ahead = (ib[i] > ib[j]) | ((ib[i] == ib[j]) if i < j else False)
            cnt = cnt + jnp.where(ahead, 1.0, 0.0)
        sel.append(cnt < 2.0)
    wj = [jnp.where(sel[j], isc[j], 0.0) for j in range(EPG)]
    denom = functools.reduce(lambda a, b: a + b, wj)
    for g in range(N_GROUPS):
        for j in range(EPG):
            comb_ref[EPG * g + j:EPG * g + j + 1, :] = jnp.where(first[g], wj[j] / denom, 0.0)


def _outproj(ycat, x, mod6, seq_rows, w_out, layer, norm_g, router_w_pad, router_b):
    t, d = x.shape
    tm = _tile(seq_rows, 1024)
    return pl.pallas_call(
        _outproj_kernel,
        grid=(t // tm,),
        in_specs=[pl.BlockSpec((tm, d), lambda i: (i, 0)),
                  pl.BlockSpec((tm, d), lambda i: (i, 0)),
                  pl.BlockSpec((1, 6, d), lambda i: ((i * tm) // seq_rows, 0, 0)),
                  pl.BlockSpec((None, d, d), lambda i: (layer, 0, 0)),
                  pl.BlockSpec((1, d), lambda i: (0, 0)),
                  pl.BlockSpec((2, d, LANES), lambda i: (0, 0, 0)),
                  pl.BlockSpec((N_EXPERTS, 1), lambda i: (0, 0))],
        out_specs=[pl.BlockSpec((tm, d), lambda i: (i, 0)),
                   pl.BlockSpec((tm, d), lambda i: (i, 0)),
                   pl.BlockSpec((N_EXPERTS, tm), lambda i: (0, i))],
        out_shape=[jax.ShapeDtypeStruct((t, d), F32),
                   jax.ShapeDtypeStruct((t, d), BF16),
                   jax.ShapeDtypeStruct((N_EXPERTS, t), F32)],
        scratch_shapes=[pltpu.VMEM((d, d), BF16)],
        compiler_params=_params(("arbitrary",)),
        name="outproj_route",
    )(ycat, x, mod6, w_out, norm_g.reshape(1, d), router_w_pad, router_b.reshape(N_EXPERTS, 1))


def _moe_kernel(h_ref, comb_ref, wg_ref, wu_ref, wd_ref, x1_ref, mod_ref, fg_ref, o_ref, *, final):
    g = pl.program_id(1)

    @pl.when(g == 0)
    def _():
        o_ref[...] = jnp.zeros_like(o_ref)

    h = h_ref[...]
    comb = comb_ref[...]
    acts = [(_silu(_dot(h, wg_ref[j])) * _dot(h, wu_ref[j]) * comb[:, j:j + 1]).astype(BF16) for j in range(EPG)]
    o_ref[...] += _dot(jnp.concatenate(acts, axis=1), wd_ref[...])

    @pl.when(g == pl.num_programs(1) - 1)
    def _():
        x2 = x1_ref[...] + mod_ref[0][5:6, :] * o_ref[...]
        o_ref[...] = _rms(x2, fg_ref[...]) if final else x2


def _moe(h2, comb, w_gate, w_up, w_down, layer, x1, mod6, seq_rows, final_g, final):
    t, d = x1.shape
    tm = _tile(seq_rows, 1024)
    return pl.pallas_call(
        functools.partial(_moe_kernel, final=final),
        grid=(t // tm, N_GROUPS),
        in_specs=[pl.BlockSpec((tm, d), lambda i, g: (i, 0)),
                  pl.BlockSpec((None, tm, EPG), lambda i, g: (g, i, 0)),
                  pl.BlockSpec((None, EPG, d, D_EXPERT), lambda i, g: (layer, g, 0, 0)),
                  pl.BlockSpec((None, EPG, d, D_EXPERT), lambda i, g: (layer, g, 0, 0)),
                  pl.BlockSpec((None, None, EPG * D_EXPERT, d), lambda i, g: (layer, g, 0, 0)),
                  pl.BlockSpec((tm, d), lambda i, g: (i, 0)),
                  pl.BlockSpec((1, 6, d), lambda i, g: ((i * tm) // seq_rows, 0, 0)),
                  pl.BlockSpec((1, d), lambda i, g: (0, 0))],
        out_specs=pl.BlockSpec((tm, d), lambda i, g: (i, 0)),
        out_shape=jax.ShapeDtypeStruct((t, d), F32),
        compiler_params=_params(("arbitrary", "arbitrary")),
        name="experts",
    )(h2, comb, w_gate, w_up, w_down, x1, mod6, final_g.reshape(1, d))


def _lane_rows(dn, ssd):
    depth = dn.shape[0]
    pad = lambda n: jnp.zeros((depth, n), F32)
    return jnp.concatenate([pad(LANE_ALPHA), dn.reshape(depth, -1).astype(F32), ssd.reshape(depth, -1).astype(F32),
                            pad(LANES - LANE_DT - 2 * H_C)], axis=1)


def kernel(x_prompt, x_sample, state_delta, state_ssd, c, c_ctx, mod_w, mod_b, norm1_g, norm2_g, w_in, w_out,
           dn_conv_w, dn_a_log, dn_dt_bias, dn_norm_g, sc_conv_w, ssd_conv_w, ssd_a_log, ssd_dt_bias, ssd_d,
           ssd_norm_g, router_w, router_b, exp_w_gate, exp_w_up, exp_w_down, final_norm_g):
    depth = mod_w.shape[0]
    n_ctx, seq, d = x_prompt.shape
    n_dec, dec_seq, _ = x_sample.shape
    assert seq % CHUNK == 0 and dec_seq % CHUNK == 0 and d == D_MODEL

    n_rows = -(-(1 + n_dec) // 8) * 8
    cond_rows = jnp.zeros((n_rows, d), F32).at[0].set(c_ctx).at[1:1 + n_dec].set(c)
    mod = _modulation(cond_rows, mod_w, mod_b).reshape(depth, n_rows, 6, d)

    w_in_b = w_in.astype(BF16)
    w_in_r = jnp.concatenate(
        [w_in_b[:, :, :SRC_SMALL_A], w_in_b[:, :, SRC_SCH:SRC_SMALL_B], w_in_b[:, :, SRC_SMALL_A:SRC_SCH],
         w_in_b[:, :, SRC_SMALL_B:], jnp.zeros((depth, d, D_IN_PAD - D_IN), BF16)], axis=-1)
    w_down_g = exp_w_down.reshape(depth, N_GROUPS, EPG * D_EXPERT, d)
    rw = jnp.pad(router_w.astype(F32), ((0, 0), (0, LANES - N_EXPERTS)))
    rw_hi = rw.astype(BF16)
    router_w_pad = jnp.stack([rw_hi, (rw - rw_hi.astype(F32)).astype(BF16)])

    xp = x_prompt.reshape(n_ctx * seq, d)
    pos = jnp.asarray(_grid_pos_embed(dec_seq, d), dtype=x_sample.dtype)
    xs = _add_pos(x_sample, pos).reshape(n_dec * dec_seq, d)

    lanev = jnp.stack([_lane_rows(dn_a_log, ssd_a_log), _lane_rows(dn_dt_bias, ssd_dt_bias), dn_norm_g.astype(F32)]
                      + [jnp.zeros((depth, LANES), F32)] * 5, axis=1)
    ssdv = jnp.stack([jnp.repeat(ssd_d.astype(F32), P_C, axis=1), ssd_norm_g.astype(F32)]
                     + [jnp.zeros((depth, C_W), F32)] * 6, axis=1)

    states = None
    for l in range(depth):
        final = l == depth - 1

        def block(x, mod6, nb, seq_len, seq_rows, s_dn0, s_ssd0, state_out):
            proj = _inproj(x, mod6, seq_rows, norm1_g[l], w_in_r, l).reshape(nb, seq_len, D_IN_PAD)
            outs = _mixer(proj, dn_conv_w, sc_conv_w, ssd_conv_w, lanev, ssdv, l, s_dn0, s_ssd0, state_out)
            ycat = outs[0].reshape(nb * seq_len, d)
            x1, h2, comb_t = _outproj(ycat, x, mod6, seq_rows, w_out, l, norm2_g[l], router_w_pad, router_b)
            comb = comb_t.reshape(N_GROUPS, EPG, -1).transpose(0, 2, 1)
            x2 = _moe(h2, comb, exp_w_gate, exp_w_up, w_down_g, l, x1, mod6, seq_rows, final_norm_g, final)
            return x2, outs[1:]

        xp, states = block(xp, mod[l, 0:1], n_ctx, seq, n_ctx * seq, None, None, (l, depth, states))
        xs, _ = block(xs, mod[l, 1:1 + n_dec], n_dec, dec_seq, dec_seq,
                      state_delta[:, l].astype(F32), state_ssd[:, l].astype(F32), None)

    return (xp.reshape(n_ctx, seq, d), xs.reshape(n_dec, dec_seq, d), states[0], states[1])
```

```python
import functools
import math
import types

import jax
import jax.numpy as jnp
import numpy as np
from jax import lax
from jax.experimental import pallas as pl
from jax.experimental.pallas import tpu as pltpu

F32 = jnp.float32
BF16 = jnp.bfloat16

D_MODEL = 1024
GRID_W = 64
POS_BASE = 10000.0
H_A, DK_A, DV_A = 4, 128, 128
A_W = H_A * DV_A
H_C, P_C, N_C, G_C = 4, 64, 64, 2
HPG = H_C // G_C
C_W = H_C * P_C
B_W = D_MODEL - A_W - C_W
XBC_W = C_W + 2 * G_C * N_C
CHUNK = 64
N_EXPERTS = 16
N_GROUPS = 4
EPG = N_EXPERTS // N_GROUPS
D_EXPERT = 256
EPS = 1e-6
LANES = 128

COL_Q, COL_K, COL_V, COL_GATE = 0, A_W, 2 * A_W, 3 * A_W
COL_SCH = 4 * A_W
COL_SCB = COL_SCH + B_W
COL_SCC = COL_SCB + B_W
COL_Z = COL_SCC + B_W
COL_XBC = COL_Z + C_W
COL_SMALL = COL_XBC + XBC_W
D_IN_PAD = COL_SMALL + LANES
LANE_BETA, LANE_ALPHA, LANE_DT = 0, 2 * H_A, 4 * H_A
SRC_SMALL_A = 4 * A_W
SRC_SCH = SRC_SMALL_A + 4 * H_A
SRC_SMALL_B = SRC_SCH + 3 * B_W + C_W + XBC_W
D_IN = SRC_SMALL_B + 2 * H_C

assert HPG == 2 and P_C == N_C == CHUNK and HPG * P_C == LANES and G_C * N_C == LANES and DK_A == DV_A == LANES

VMEM_LIMIT = 56 * 1024 * 1024
STATIC_PREP_CHUNKS = 4


def _dot(a, b):
    return jnp.dot(a.astype(BF16), b.astype(BF16), preferred_element_type=F32)


def _dot_nt(a, b):
    return lax.dot_general(a.astype(BF16), b.astype(BF16), (((1,), (1,)), ((), ())), preferred_element_type=F32)


def _dot_tn(a, b):
    return lax.dot_general(a.astype(BF16), b.astype(BF16), (((0,), (0,)), ((), ())), preferred_element_type=F32)


def _dot_f32(a, b):
    return jnp.dot(a, b, precision=lax.Precision.HIGHEST, preferred_element_type=F32)


def _dot_nt_f32(a, b):
    return lax.dot_general(a, b, (((1,), (1,)), ((), ())), precision=lax.Precision.HIGHEST,
                           preferred_element_type=F32)


def _silu(x):
    h = 0.5 * x
    return h + h * jnp.tanh(h)


def _sigmoid(x):
    return 1.0 / (1.0 + jnp.exp(-x))


def _softplus(x):
    return jnp.maximum(x, 0.0) + jnp.log1p(jnp.exp(-jnp.abs(x)))


def _rms(x, g):
    return x * lax.rsqrt(jnp.mean(x * x, axis=-1, keepdims=True) + EPS) * g


def _tile(n, pref):
    t = min(n, pref)
    while n % t:
        t -= 8
    assert t > 0 and t % 8 == 0, (n, pref)
    return t


def _params(sem):
    return pltpu.CompilerParams(dimension_semantics=sem, vmem_limit_bytes=VMEM_LIMIT)


def _mod_kernel(cond_ref, w_ref, b_ref, o_ref):
    s = _silu(cond_ref[...])
    o_ref[0] = _dot_f32(s, w_ref[0]) + b_ref[0]


def _modulation(cond_rows, mod_w, mod_b):
    depth, d, n = mod_w.shape
    r = cond_rows.shape[0]
    tn = _tile(n, 1536)
    return pl.pallas_call(
        _mod_kernel,
        grid=(depth, n // tn),
        in_specs=[pl.BlockSpec((r, d), lambda l, j: (0, 0)),
                  pl.BlockSpec((1, d, tn), lambda l, j: (l, 0, j)),
                  pl.BlockSpec((1, 1, tn), lambda l, j: (l, 0, j))],
        out_specs=pl.BlockSpec((1, r, tn), lambda l, j: (l, 0, j)),
        out_shape=jax.ShapeDtypeStruct((depth, r, n), F32),
        compiler_params=_params(("arbitrary", "arbitrary")),
        name="modulation",
    )(cond_rows, mod_w, mod_b.reshape(depth, 1, n))


def _add_kernel(x_ref, p_ref, o_ref):
    o_ref[0] = x_ref[0] + p_ref[...]


def _add_pos(x, pos):
    nb, l, d = x.shape
    tl = _tile(l, 512)
    return pl.pallas_call(
        _add_kernel,
        grid=(nb, l // tl),
        in_specs=[pl.BlockSpec((1, tl, d), lambda b, i: (b, i, 0)),
                  pl.BlockSpec((tl, d), lambda b, i: (i, 0))],
        out_specs=pl.BlockSpec((1, tl, d), lambda b, i: (b, i, 0)),
        out_shape=jax.ShapeDtypeStruct(x.shape, x.dtype),
        compiler_params=_params(("arbitrary", "arbitrary")),
        name="add_pos",
    )(x, pos)


def _grid_pos_embed(n_tok, dim):
    rows = n_tok // GRID_W
    rr, cc = np.meshgrid(np.arange(rows, dtype=np.float64), np.arange(GRID_W, dtype=np.float64), indexing="ij")
    quarter = dim // 4
    omega = 1.0 / (POS_BASE ** (np.arange(quarter, dtype=np.float64) / quarter))
    ang_r = rr.reshape(-1, 1) * omega
    ang_c = cc.reshape(-1, 1) * omega
    return np.concatenate([np.sin(ang_r), np.cos(ang_r), np.sin(ang_c), np.cos(ang_c)], axis=-1)


def _inproj_kernel(x_ref, mod_ref, g_ref, w_ref, o_ref):
    m = mod_ref[0]
    shift, scale = m[0:1, :], m[1:2, :]
    h = _rms(x_ref[...], g_ref[...]) * (1.0 + scale) + shift
    o_ref[...] = _dot(h, w_ref[...])


def _inproj(x, mod6, seq_rows, norm_g, w_in_r, layer):
    t, d = x.shape
    tm = _tile(seq_rows, 512)
    return pl.pallas_call(
        _inproj_kernel,
        grid=(t // tm,),
        in_specs=[pl.BlockSpec((tm, d), lambda i: (i, 0)),
                  pl.BlockSpec((1, 6, d), lambda i: ((i * tm) // seq_rows, 0, 0)),
                  pl.BlockSpec((1, d), lambda i: (0, 0)),
                  pl.BlockSpec((None, d, D_IN_PAD), lambda i: (layer, 0, 0))],
        out_specs=pl.BlockSpec((tm, D_IN_PAD), lambda i: (i, 0)),
        out_shape=jax.ShapeDtypeStruct((t, D_IN_PAD), F32),
        compiler_params=_params(("arbitrary",)),
        name="inproj",
    )(x, mod6, norm_g.reshape(1, d), w_in_r)


CONV_WIN = CHUNK + 16


def _conv_select(r0, w0):
    ti = lax.broadcasted_iota(jnp.int32, (2 * CHUNK, 2 * CONV_WIN), 0)
    tj = lax.broadcasted_iota(jnp.int32, (2 * CHUNK, 2 * CONV_WIN), 1)
    want = r0 + jnp.where(ti < CHUNK, ti - 1, ti - CHUNK + 1)
    have = w0 + jnp.where(tj < CONV_WIN, tj, tj - CONV_WIN)
    return jnp.where(want == have, 1.0, 0.0).astype(BF16)


def _conv3(load, r0, w0, sel, w):
    cur = load(pl.ds(r0, CHUNK))
    win = load(pl.ds(w0, CONV_WIN))
    hi = win.astype(BF16)
    lo = (win - hi.astype(F32)).astype(BF16)
    nb = jnp.dot(sel, jnp.concatenate([hi, lo], axis=0), preferred_element_type=F32)
    return w[0:1, :] * nb[:CHUNK] + w[1:2, :] * cur + w[2:3, :] * nb[CHUNK:]


def _mixer_kernel(*refs, seqs_per_step, batched_in, n_in, n_out, **static):
    ins, outs, scratch = refs[:n_in], refs[n_in:n_in + n_out], refs[n_in + n_out:]
    n_scr = len(scratch) // seqs_per_step
    seqs = []
    for s in range(seqs_per_step):
        one = lambda r, s=s: r.at[pl.ds(s, 1)]
        seqs.append(_mixer_phases([one(r) if i in batched_in else r for i, r in enumerate(ins)]
                                  + [one(r) for r in outs] + list(scratch[s * n_scr:(s + 1) * n_scr]), **static))
    nc, cpi = seqs[0].nc, seqs[0].cpi
    if nc <= STATIC_PREP_CHUNKS:
        for q in seqs:
            for z in range(nc):
                q.prep(z, 0)
            q.init()
        for q in seqs:
            for i in range(nc // cpi):
                q.delta_prep(i, 0)
        for z in range(nc):
            _scan_step(seqs, z)
        for q in seqs:
            for z in range(nc):
                q.finish(z, 0)
    else:
        for q in seqs:
            lax.fori_loop(0, nc, q.prep, 0, unroll=2)
            q.init()
            lax.fori_loop(0, nc // cpi, q.delta_prep, 0)

        def scan(z, carry):
            _scan_step(seqs, z)
            return carry

        lax.fori_loop(0, nc, scan, 0, unroll=4)
        for q in seqs:
            lax.fori_loop(0, nc, q.finish, 0, unroll=2)
    for q in seqs:
        q.emit()


def _scan_step(seqs, z):
    states = [q.scan_begin(z) for q in seqs]
    for stage in range(len(seqs[0].scan_stages)):
        for q, st in zip(seqs, states):
            q.scan_stages[stage](st)


def _mixer_phases(refs, seq_len, zero_init, state_layer, state_layers, state_aliased):
    refs = list(refs)
    proj, dnw, scw, ssw, lanev, ssdv = refs[:6]
    k = 6
    if not zero_init:
        sdn0, sssd0 = refs[k:k + 2]
        k += 2
    emit_state = state_layer is not None
    if state_aliased:
        k += 2
    ycat = refs[k]
    k += 1
    if emit_state:
        sdn_out, sssd_out = refs[k:k + 2]
        k += 2
    (qkv_s, xbc_s, bdup_s, cdup_s, gc_s, dpair_s, gcpair_s, tot_s, sp_s, beta_s, u_s, wq_s, kd_s, qk_s, o_s, y_s,
     st_s, hs_s) = refs[k:]

    nc = seq_len // CHUNK
    ri = lax.broadcasted_iota(jnp.int32, (CHUNK, CHUNK), 0)
    ci = lax.broadcasted_iota(jnp.int32, (CHUNK, CHUNK), 1)
    tril = (ri >= ci).astype(F32)
    r128 = lax.broadcasted_iota(jnp.int32, (LANES, LANES), 0)
    c128 = lax.broadcasted_iota(jnp.int32, (LANES, LANES), 1)
    eye128 = (r128 == c128).astype(F32)
    n_lvl = int(math.log2(CHUNK))
    alog = lanev[0:1, :]
    bias = lanev[1:2, :]
    pw = 2 * CHUNK
    ri2 = lax.broadcasted_iota(jnp.int32, (CHUNK, pw), 0)
    ci2 = lax.broadcasted_iota(jnp.int32, (CHUNK, pw), 1)
    lane_hi = ci2 >= CHUNK
    tj = ci2 & (CHUNK - 1)
    incl2 = (ri2 >= tj, ri2 <= tj)
    ahead = jnp.where(lane_hi, tj - ri2, ri2 - tj)
    incl_fb = ahead >= 0
    strict_fb = ahead > 0
    eye2 = (ri2 == tj).astype(F32)
    lvl2 = [((ri2 >> s) == (tj >> s)) & ((ri2 >> (s - 1)) != (tj >> (s - 1))) for s in range(1, n_lvl + 1)]

    def block_diag(x):
        return jnp.concatenate([jnp.where(lane_hi, 0.0, x), jnp.where(lane_hi, x, 0.0)], axis=0).astype(BF16)

    def chunk_rows(z):
        return pl.ds(z * CHUNK if isinstance(z, int) else pl.multiple_of(z * CHUNK, CHUNK), CHUNK)

    def prep(z, carry):
        r0 = z * CHUNK if isinstance(z, int) else pl.multiple_of(z * CHUNK, CHUNK)
        rows = pl.ds(r0, CHUNK)
        if isinstance(z, int):
            w0 = min(max(r0 - 8, 0), seq_len - CONV_WIN)
        else:
            w0 = pl.multiple_of(jnp.clip(r0 - 8, 0, seq_len - CONV_WIN), 8)
        sel = _conv_select(r0, w0)
        cw = 2 * LANES
        for jj in range(3 * A_W // cw):
            c0 = jj * cw
            a2 = _silu(_conv3(lambda rs: proj[0, rs, c0:c0 + cw], r0, w0, sel, dnw[:, c0:c0 + cw]))
            for half in range(cw // LANES):
                a = a2[:, half * LANES:(half + 1) * LANES]
                j = jj * (cw // LANES) + half
                if j < 2 * H_A:
                    a = a * lax.rsqrt(jnp.sum(a * a, axis=-1, keepdims=True) + EPS)
                if j < H_A:
                    a = a * (DK_A ** -0.5)
                qkv_s[rows, j * LANES:(j + 1) * LANES] = a
        for jj in range(B_W // cw):
            c0 = jj * cw
            cv = _conv3(lambda rs: proj[0, rs, COL_SCC + c0:COL_SCC + c0 + cw]
                        * proj[0, rs, COL_SCH + c0:COL_SCH + c0 + cw], r0, w0, sel, scw[:, c0:c0 + cw])
            yb = proj[0, rows, COL_SCB + c0:COL_SCB + c0 + cw] * cv
            ycat[0, rows, A_W + c0:A_W + c0 + cw] = yb.astype(ycat.dtype)
        for jj in range(XBC_W // cw):
            c1 = jj * cw
            a2 = _silu(_conv3(lambda rs: proj[0, rs, COL_XBC + c1:COL_XBC + c1 + cw], r0, w0, sel,
                              ssw[:, c1:c1 + cw]))
            if c1 < C_W:
                xbc_s[rows, c1:c1 + cw] = a2
                continue
            for half, dup_s in enumerate((bdup_s, cdup_s)):
                a = a2[:, half * LANES:(half + 1) * LANES]
                swapped = pltpu.roll(a, N_C, 1)
                lo = lax.broadcasted_iota(jnp.int32, a.shape, 1) < N_C
                dup_s[rows, 0:LANES] = jnp.where(lo, a, swapped)
                dup_s[rows, LANES:2 * LANES] = jnp.where(lo, swapped, a)
        sm = proj[0, rows, COL_SMALL:COL_SMALL + LANES]
        sp = _softplus(sm + bias)
        g = -jnp.exp(alog) * sp
        pre = _dot_f32(tril, g)
        tot = pre[CHUNK - 1:CHUNK, :]
        suf = tot - pre + g
        sp_s[rows, :] = sp
        beta_s[rows, :] = _sigmoid(sm)
        gc_s[0, rows, :] = pre
        gc_s[1, rows, :] = suf
        grs = [_dot_nt_f32(eye128, gcd) for gcd in (pre, suf)]
        for d, gr in enumerate(grs):
            for g in range(G_C):
                ln = LANE_DT + d * H_C + g * HPG
                gcpair_s[d, z, g:g + 1, :] = jnp.concatenate([gr[ln:ln + 1, :], gr[ln + 1:ln + 2, :]], axis=1)
        for h in range(H_A):
            lf, lb = LANE_ALPHA + h, LANE_ALPHA + H_A + h
            dpair_s[z, h:h + 1, :] = jnp.concatenate([grs[0][lf:lf + 1, :], grs[1][lb:lb + 1, :]], axis=1)
        tot_s[z] = jnp.broadcast_to(tot, (8, LANES))
        o_s[rows, :] = jnp.zeros((CHUNK, A_W), F32)
        y_s[rows, :] = jnp.zeros((CHUNK, C_W), F32)
        return carry

    def init():
        for d in range(2):
            for h in range(H_A):
                st_s[d * H_A + h] = jnp.zeros((DK_A, DV_A), F32) if zero_init else sdn0[0, d, h]
            for g in range(G_C):
                if zero_init:
                    hs_s[d, g] = jnp.zeros((HPG * P_C, HPG * N_C), F32)
                else:
                    zero = jnp.zeros((P_C, N_C), F32)
                    hs_s[d, g] = jnp.concatenate(
                        [jnp.concatenate([sssd0[0, d, g * HPG], zero], axis=1),
                         jnp.concatenate([zero, sssd0[0, d, g * HPG + 1]], axis=1)], axis=0)

    cpi = 4 if nc % 4 == 0 else 2

    def delta_prep(i, carry):
        units = []
        for zz in range(cpi):
            z = cpi * i + zz
            rows = chunk_rows(z)
            tot = tot_s[z][0:1, :]
            beta = beta_s[rows, :]
            gcs = [gc_s[d, rows, :] for d in range(2)]
            for h in range(H_A):
                q_h = qkv_s[rows, COL_Q + h * DK_A:COL_Q + (h + 1) * DK_A]
                k_h = qkv_s[rows, COL_K + h * DK_A:COL_K + (h + 1) * DK_A]
                v_h = qkv_s[rows, COL_V + h * DV_A:COL_V + (h + 1) * DV_A]
                units.append(dict(z=z, h=h, q=q_h, k=k_h, v=v_h, tot=tot, beta=beta, gcs=gcs))
        qkk = [_dot_nt(jnp.concatenate([p["q"], p["k"]], axis=0), jnp.concatenate([p["k"], p["k"]], axis=0))
               for p in units]
        ms, rhss = [], []
        for p, qk_kk in zip(units, qkk):
            z, h = p["z"], p["h"]
            lf, lb = LANE_ALPHA + h, LANE_ALPHA + H_A + h
            bf, bb = LANE_BETA + h, LANE_BETA + H_A + h
            a_pair = jnp.where(lane_hi, p["gcs"][1][:, lb:lb + 1], p["gcs"][0][:, lf:lf + 1])
            b_pair = jnp.where(lane_hi, p["beta"][:, bb:bb + 1], p["beta"][:, bf:bf + 1])
            decay = jnp.exp(jnp.where(incl_fb, a_pair - dpair_s[z, h:h + 1, :], -1e30))
            ms.append(jnp.where(strict_fb, qk_kk[CHUNK:] * b_pair * decay, 0.0))
            qk_s[z, h] = (qk_kk[:CHUNK] * decay).astype(qk_s.dtype)
            rhs_d = []
            for d, (ln, bl) in enumerate(((lf, bf), (lb, bb))):
                a_col = p["gcs"][d][:, ln:ln + 1]
                b_col = p["beta"][:, bl:bl + 1]
                eg = jnp.exp(a_col)
                rhs_d.append(jnp.concatenate([p["v"] * b_col, p["k"] * (b_col * eg)], axis=1))
                kd_s[d, z, h] = (p["k"] * jnp.exp(p["tot"][:, ln:ln + 1] - a_col)).astype(kd_s.dtype)
                wq_s[d, z, h, CHUNK:, :] = (p["q"] * eg).astype(wq_s.dtype)
            zero = jnp.zeros_like(rhs_d[0])
            rhss.append(jnp.concatenate([jnp.concatenate([rhs_d[0], zero], axis=1),
                                         jnp.concatenate([zero, rhs_d[1]], axis=1)], axis=0))
        t_inv = [eye2 - jnp.where(lvl2[0], m, 0.0) for m in ms]
        for s in range(1, n_lvl):
            x = [_dot(jnp.where(lvl2[s], m, 0.0), block_diag(t)) for m, t in zip(ms, t_inv)]
            t_inv = [t - _dot(t, block_diag(xx)) for t, xx in zip(t_inv, x)]
        uw = [_dot(t, rhs) for t, rhs in zip(t_inv, rhss)]
        for p, r in zip(units, uw):
            z, h = p["z"], p["h"]
            for d in range(2):
                c0 = d * (DV_A + DK_A)
                u_s[d, z, h] = r[:, c0:c0 + DV_A]
                wq_s[d, z, h, :CHUNK, :] = r[:, c0 + DV_A:c0 + DV_A + DK_A].astype(wq_s.dtype)
        return carry

    rb =lax.broadcasted_iota(jnp.int32, (pw, pw), 0) >= P_C
    cbk = lax.broadcasted_iota(jnp.int32, (pw, pw), 1) >= N_C
    diag_blk = rb == cbk

    def scan_begin(z):
        dus = [(d, (z if d == 0 else nc - 1 - z), h) for d in range(2) for h in range(H_A)]
        s_prev = [st_s[d * H_A + h] for d, _, h in dus]
        us = []
        for d in range(2):
            zc = z if d == 0 else nc - 1 - z
            rows = chunk_rows(zc)
            sp = sp_s[rows, :]
            tot = tot_s[zc][0:1, :]
            gc = gc_s[d, rows, :]
            for g in range(G_C):
                ln = LANE_DT + d * H_C + g * HPG
                gsl = slice(g * pw, (g + 1) * pw)
                a_pair = jnp.where(lane_hi, gc[:, ln + 1:ln + 2], gc[:, ln:ln + 1])
                t_pair = jnp.where(lane_hi, tot[:, ln + 1:ln + 2], tot[:, ln:ln + 1])
                lmat = jnp.exp(jnp.where(incl2[d], a_pair - gcpair_s[d, zc, g:g + 1, :], -1e30))
                xdt = xbc_s[rows, gsl] * jnp.where(lane_hi, sp[:, ln + 1:ln + 2], sp[:, ln:ln + 1])
                b_dup = bdup_s[rows, gsl]
                c_dup = cdup_s[rows, gsl]
                us.append(dict(d=d, g=g, rows=rows, gsl=gsl, lmat=lmat, xdt=xdt, c_dup=c_dup,
                               c_lo=jnp.where(lane_hi, 0.0, c_dup), b_st=jnp.concatenate([b_dup, b_dup], axis=0),
                               bdec=b_dup * jnp.exp(t_pair - a_pair), ea=jnp.exp(a_pair),
                               dec=jnp.where(rb, jnp.exp(tot[:, ln + 1:ln + 2]), jnp.exp(tot[:, ln:ln + 1])),
                               x_bd=jnp.concatenate([jnp.where(lane_hi, 0.0, xdt), jnp.where(lane_hi, xdt, 0.0)],
                                                    axis=0),
                               h_prev=hs_s[d, g]))
        return dict(dus=dus, s_prev=s_prev, us=us)

    def scan_delta_1(t):
        t["ws_qs"] = [_dot(wq_s[idx], s) for idx, s in zip(t["dus"], t["s_prev"])]

    def scan_ssd_1(t):
        t["st"] = [_dot_tn(p["xdt"], p["bdec"]) for p in t["us"]]
        t["cb"] = [_dot_nt(p["c_lo"], p["b_st"]) for p in t["us"]]
        t["y_off"] = [_dot_nt(p["c_dup"], p["h_prev"]) for p in t["us"]]

    def scan_delta_2(t):
        v_new = [u_s[idx] - r[:CHUNK] for idx, r in zip(t["dus"], t["ws_qs"])]
        zero = jnp.zeros((CHUNK, DV_A), F32)
        t["o_in"] = [_dot(qk_s[zc, h], jnp.concatenate([v, zero] if d == 0 else [zero, v], axis=0))
                     for (d, zc, h), v in zip(t["dus"], v_new)]
        t["s_add"] = [_dot_tn(kd_s[idx], v) for idx, v in zip(t["dus"], v_new)]

    def scan_ssd_2(t):
        t["y_diag"] = [_dot(cbd * p["lmat"], p["x_bd"]) for p, cbd in zip(t["us"], t["cb"])]

    def scan_store(t):
        for p, yd, yo, s in zip(t["us"], t["y_diag"], t["y_off"], t["st"]):
            y_s[p["rows"], p["gsl"]] = y_s[p["rows"], p["gsl"]] + yd + yo * p["ea"]
            hs_s[p["d"], p["g"]] = p["h_prev"] * p["dec"] + jnp.where(diag_blk, s, 0.0)
        for (d, zc, h), r, oi, sa, s in zip(t["dus"], t["ws_qs"], t["o_in"], t["s_add"], t["s_prev"]):
            ln = LANE_ALPHA + d * H_A + h
            cs = slice(h * DV_A, (h + 1) * DV_A)
            o_s[chunk_rows(zc), cs] = o_s[chunk_rows(zc), cs] + r[CHUNK:] + oi
            st_s[d * H_A + h] = s * jnp.exp(tot_s[zc][0:1, ln:ln + 1]) + sa


    def finish(z, carry):
        rows = chunk_rows(z)
        for h in range(H_A):
            cs = slice(h * DV_A, (h + 1) * DV_A)
            o = _rms(o_s[rows, cs], lanev[2:3, :])
            o = o * _silu(proj[0, rows, COL_GATE + h * DV_A:COL_GATE + (h + 1) * DV_A])
            ycat[0, rows, cs] = o.astype(ycat.dtype)
        y = y_s[rows, :] + ssdv[0:1, :] * xbc_s[rows, 0:C_W]
        y = _rms(y * _silu(proj[0, rows, COL_Z:COL_Z + C_W]), ssdv[1:2, :])
        ycat[0, rows, A_W + B_W:] = y.astype(ycat.dtype)
        return carry

    def emit():
        if not emit_state:
            return
        slots = (None,) if state_aliased else range(state_layers)
        for slot in slots:
            dn_slot = sdn_out.at[0] if slot is None else sdn_out.at[0, slot]
            ssd_slot = sssd_out.at[0] if slot is None else sssd_out.at[0, slot]
            mine = slot is None or slot == state_layer
            for d in range(2):
                for h in range(H_A):
                    dn_slot[d, h] = st_s[d * H_A + h] if mine else jnp.zeros((DK_A, DV_A), F32)
                for h in range(H_C):
                    k0 = (h % HPG) * P_C
                    ssd_slot[d, h] = (hs_s[d, h // HPG][k0:k0 + P_C, k0:k0 + N_C] if mine
                                      else jnp.zeros((P_C, N_C), F32))

    return types.SimpleNamespace(
        nc=nc, cpi=cpi, prep=prep, init=init, delta_prep=delta_prep, scan_begin=scan_begin,
        scan_stages=(scan_delta_1, scan_ssd_1, scan_delta_2, scan_ssd_2, scan_store), finish=finish, emit=emit)


def _mixer(proj, dn_conv_w, sc_conv_w, ssd_conv_w, lanev, ssdv, s_dn0, s_ssd0, state_out):
    nb, seq_len, _ = proj.shape
    nc = seq_len // CHUNK
    assert nc % 2 == 0
    zero_init = s_dn0 is None
    spg = 2 if nc <= STATIC_PREP_CHUNKS and nb % 2 == 0 else 1
    full = lambda a: pl.BlockSpec(a.shape, lambda b: (0,) * a.ndim)
    args = [proj, dn_conv_w, sc_conv_w, ssd_conv_w, lanev, ssdv]
    batched_in = [0]
    in_specs = [pl.BlockSpec((spg, seq_len, D_IN_PAD), lambda b: (b, 0, 0), pipeline_mode=pl.Buffered(1))
                if seq_len > 512 else pl.BlockSpec((spg, seq_len, D_IN_PAD), lambda b: (b, 0, 0)),
                full(dn_conv_w), full(sc_conv_w), full(ssd_conv_w), full(lanev), full(ssdv)]
    if not zero_init:
        batched_in += [len(args), len(args) + 1]
        args += [s_dn0, s_ssd0]
        in_specs += [pl.BlockSpec((spg, 2, H_A, DK_A, DV_A), lambda b: (b, 0, 0, 0, 0)),
                     pl.BlockSpec((spg, 2, H_C, P_C, N_C), lambda b: (b, 0, 0, 0, 0))]
    out_shape = [jax.ShapeDtypeStruct((nb, seq_len, D_MODEL), BF16)]
    out_specs = [pl.BlockSpec((spg, seq_len, D_MODEL), lambda b: (b, 0, 0))]
    aliases = {}
    layer, depth, prev = state_out if state_out is not None else (None, None, None)
    if state_out is not None:
        out_shape += [jax.ShapeDtypeStruct((nb, depth, 2, H_A, DK_A, DV_A), F32),
                      jax.ShapeDtypeStruct((nb, depth, 2, H_C, P_C, N_C), F32)]
        if prev is None:
            out_specs += [pl.BlockSpec((spg, depth, 2, H_A, DK_A, DV_A), lambda b: (b, 0, 0, 0, 0, 0)),
                          pl.BlockSpec((spg, depth, 2, H_C, P_C, N_C), lambda b: (b, 0, 0, 0, 0, 0))]
        else:
            aliases = {len(args): 1, len(args) + 1: 2}
            args += list(prev)
            in_specs += [pl.BlockSpec(memory_space=pl.ANY), pl.BlockSpec(memory_space=pl.ANY)]
            out_specs += [pl.BlockSpec((spg, None, 2, H_A, DK_A, DV_A), lambda b: (b, layer, 0, 0, 0, 0)),
                          pl.BlockSpec((spg, None, 2, H_C, P_C, N_C), lambda b: (b, layer, 0, 0, 0, 0))]
    scratch = [pltpu.VMEM((seq_len, 3 * A_W), F32),
               pltpu.VMEM((seq_len, C_W), F32),
               pltpu.VMEM((seq_len, G_C * LANES), F32),
               pltpu.VMEM((seq_len, G_C * LANES), F32),
               pltpu.VMEM((2, seq_len, LANES), F32),
               pltpu.VMEM((nc, 8, LANES), F32),
               pltpu.VMEM((2, nc, 8, LANES), F32),
               pltpu.VMEM((nc, 8, LANES), F32),
               pltpu.VMEM((seq_len, LANES), F32),
               pltpu.VMEM((seq_len, LANES), F32),
               pltpu.VMEM((2, nc, H_A, CHUNK, DV_A), F32),
               pltpu.VMEM((2, nc, H_A, 2 * CHUNK, DK_A), BF16),
               pltpu.VMEM((2, nc, H_A, CHUNK, DK_A), BF16),
               pltpu.VMEM((nc, H_A, CHUNK, 2 * CHUNK), BF16),
               pltpu.VMEM((seq_len, A_W), F32),
               pltpu.VMEM((seq_len, C_W), F32),
               pltpu.VMEM((2 * H_A, DK_A, DV_A), F32),
               pltpu.VMEM((2, G_C, HPG * P_C, HPG * N_C), F32)]
    return pl.pallas_call(
        functools.partial(_mixer_kernel, seqs_per_step=spg, batched_in=tuple(batched_in), n_in=len(args),
                          n_out=len(out_shape), seq_len=seq_len, zero_init=zero_init, state_layer=layer,
                          state_layers=depth, state_aliased=prev is not None),
        grid=(nb // spg,),
        in_specs=in_specs,
        out_specs=out_specs,
        out_shape=out_shape,
        input_output_aliases=aliases,
        scratch_shapes=scratch * spg,
        compiler_params=_params(("arbitrary",)),
        name="mixer",
    )(*args)


def _top2_sum(a, b, c, d):
    hi1, lo1 = jnp.maximum(a, b), jnp.minimum(a, b)
    hi2, lo2 = jnp.maximum(c, d), jnp.minimum(c, d)
    return jnp.maximum(hi1, hi2) + jnp.maximum(jnp.minimum(hi1, hi2), jnp.maximum(lo1, lo2))


def _outproj_kernel(y_ref, x_ref, mod_ref, w_ref, g_ref, rw_ref, rb_ref, x1_ref, h2_ref, comb_ref, wb_ref):
    @pl.when(pl.program_id(0) == 0)
    def _():
        wb_ref[...] = w_ref[...].astype(wb_ref.dtype)

    m = mod_ref[0]
    gate1, shift2, scale2 = m[2:3, :], m[3:4, :], m[4:5, :]
    x1 = x_ref[...] + gate1 * _dot(y_ref[...], wb_ref[...])
    x1_ref[...] = x1
    h2 = _rms(x1, g_ref[...]) * (1.0 + scale2) + shift2
    h2_ref[...] = h2.astype(h2_ref.dtype)

    h_hi = h2.astype(BF16)
    h_lo = (h2 - h_hi.astype(F32)).astype(BF16)
    rw_hi, rw_lo = rw_ref[0], rw_ref[1]
    hl = jnp.dot(jnp.concatenate([h_hi, h_lo], axis=0), rw_hi, preferred_element_type=F32)
    tm = h2.shape[0]
    logits = hl[:tm] + hl[tm:] + jnp.dot(h_hi, rw_lo, preferred_element_type=F32)
    scores = _sigmoid(logits.T[:N_EXPERTS, :])
    biased = scores + rb_ref[...]
    sc = [scores[e:e + 1, :] for e in range(N_EXPERTS)]
    bi = [biased[e:e + 1, :] for e in range(N_EXPERTS)]
    gs = [_top2_sum(*bi[EPG * g:EPG * (g + 1)]) for g in range(N_GROUPS)]
    gmax = functools.reduce(jnp.maximum, gs)
    first = []
    taken = None
    for g in range(N_GROUPS):
        hit = gs[g] == gmax
        if taken is None:
            first.append(hit)
            taken = hit
        else:
            first.append(hit & jnp.logical_not(taken))
            taken = taken | hit

    def pick(vals, j):
        out = vals[EPG * (N_GROUPS - 1) + j]
        for g in range(N_GROUPS - 2, -1, -1):
            out = jnp.where(first[g], vals[EPG * g + j], out)
        return out

    ib = [pick(bi, j) for j in range(EPG)]
    isc = [pick(sc, j) for j in range(EPG)]
    sel = []
    for j in range(EPG):
        cnt = jnp.zeros_like(ib[j])
        for i in range(EPG):
            if i == j:
                continue
            ahead = (ib[i] > ib[j]) | ((ib[i] == ib[j]) if i < j else False)
            cnt = cnt + jnp.where(ahead, 1.0, 0.0)
        sel.append(cnt < 2.0)
    wj = [jnp.where(sel[j], isc[j], 0.0) for j in range(EPG)]
    denom = functools.reduce(lambda a, b: a + b, wj)
    for g in range(N_GROUPS):
        for j in range(EPG):
            comb_ref[EPG * g + j:EPG * g + j + 1, :] = jnp.where(first[g], wj[j] / denom, 0.0)


def _outproj(ycat, x, mod6, seq_rows, w_out, layer, norm_g, router_w_pad, router_b):
    t, d = x.shape
    tm = _tile(seq_rows, 512)
    return pl.pallas_call(
        _outproj_kernel,
        grid=(t // tm,),
        in_specs=[pl.BlockSpec((tm, d), lambda i: (i, 0)),
                  pl.BlockSpec((tm, d), lambda i: (i, 0)),
                  pl.BlockSpec((1, 6, d), lambda i: ((i * tm) // seq_rows, 0, 0)),
                  pl.BlockSpec((None, d, d), lambda i: (layer, 0, 0)),
                  pl.BlockSpec((1, d), lambda i: (0, 0)),
                  pl.BlockSpec((2, d, LANES), lambda i: (0, 0, 0)),
                  pl.BlockSpec((N_EXPERTS, 1), lambda i: (0, 0))],
        out_specs=[pl.BlockSpec((tm, d), lambda i: (i, 0)),
                   pl.BlockSpec((tm, d), lambda i: (i, 0)),
                   pl.BlockSpec((N_EXPERTS, tm), lambda i: (0, i))],
        out_shape=[jax.ShapeDtypeStruct((t, d), F32),
                   jax.ShapeDtypeStruct((t, d), BF16),
                   jax.ShapeDtypeStruct((N_EXPERTS, t), F32)],
        scratch_shapes=[pltpu.VMEM((d, d), BF16)],
        compiler_params=_params(("arbitrary",)),
        name="outproj_route",
    )(ycat, x, mod6, w_out, norm_g.reshape(1, d), router_w_pad, router_b.reshape(N_EXPERTS, 1))


def _moe_kernel(h_ref, comb_ref, wg_ref, wu_ref, wd_ref, x1_ref, mod_ref, fg_ref, o_ref, *, final):
    g = pl.program_id(1)

    @pl.when(g == 0)
    def _():
        o_ref[...] = jnp.zeros_like(o_ref)

    h = h_ref[...]
    comb = comb_ref[...].T
    acts = [(_silu(_dot(h, wg_ref[j])) * _dot(h, wu_ref[j]) * comb[:, j:j + 1]).astype(BF16) for j in range(EPG)]
    o_ref[...] += _dot(jnp.concatenate(acts, axis=1), wd_ref[...])

    @pl.when(g == pl.num_programs(1) - 1)
    def _():
        x2 = x1_ref[...] + mod_ref[0][5:6, :] * o_ref[...]
        o_ref[...] = _rms(x2, fg_ref[...]) if final else x2


def _moe(h2, comb, w_gate, w_up, w_down, layer, x1, mod6, seq_rows, final_g, final):
    t, d = x1.shape
    tm = _tile(seq_rows, 1024)
    return pl.pallas_call(
        functools.partial(_moe_kernel, final=final),
        grid=(t // tm, N_GROUPS),
        in_specs=[pl.BlockSpec((tm, d), lambda i, g: (i, 0)),
                  pl.BlockSpec((None, EPG, tm), lambda i, g: (g, 0, i)),
                  pl.BlockSpec((None, EPG, d, D_EXPERT), lambda i, g: (layer, g, 0, 0)),
                  pl.BlockSpec((None, EPG, d, D_EXPERT), lambda i, g: (layer, g, 0, 0)),
                  pl.BlockSpec((None, None, EPG * D_EXPERT, d), lambda i, g: (layer, g, 0, 0)),
                  pl.BlockSpec((tm, d), lambda i, g: (i, 0)),
                  pl.BlockSpec((1, 6, d), lambda i, g: ((i * tm) // seq_rows, 0, 0)),
                  pl.BlockSpec((1, d), lambda i, g: (0, 0))],
        out_specs=pl.BlockSpec((tm, d), lambda i, g: (i, 0)),
        out_shape=jax.ShapeDtypeStruct((t, d), F32),
        compiler_params=_params(("arbitrary", "arbitrary")),
        name="experts",
    )(h2, comb, w_gate, w_up, w_down, x1, mod6, final_g.reshape(1, d))


def _lane_row(*pieces):
    row = jnp.zeros((LANES,), F32)
    for lane, vals in pieces:
        row = lax.dynamic_update_slice(row, vals.reshape(-1).astype(F32), (lane,))
    return row


def kernel(x_prompt, x_sample, state_delta, state_ssd, c, c_ctx, mod_w, mod_b, norm1_g, norm2_g, w_in, w_out,
           dn_conv_w, dn_a_log, dn_dt_bias, dn_norm_g, sc_conv_w, ssd_conv_w, ssd_a_log, ssd_dt_bias, ssd_d,
           ssd_norm_g, router_w, router_b, exp_w_gate, exp_w_up, exp_w_down, final_norm_g):
    depth = mod_w.shape[0]
    n_ctx, seq, d = x_prompt.shape
    n_dec, dec_seq, _ = x_sample.shape
    assert seq % CHUNK == 0 and dec_seq % CHUNK == 0 and d == D_MODEL

    n_rows = -(-(1 + n_dec) // 8) * 8
    cond_rows = jnp.zeros((n_rows, d), F32).at[0].set(c_ctx).at[1:1 + n_dec].set(c)
    mod = _modulation(cond_rows, mod_w, mod_b).reshape(depth, n_rows, 6, d)

    w_in_b = w_in.astype(BF16)
    w_in_r = jnp.concatenate(
        [w_in_b[:, :, :SRC_SMALL_A], w_in_b[:, :, SRC_SCH:SRC_SMALL_B], w_in_b[:, :, SRC_SMALL_A:SRC_SCH],
         w_in_b[:, :, SRC_SMALL_B:], jnp.zeros((depth, d, D_IN_PAD - D_IN), BF16)], axis=-1)
    w_down_g = exp_w_down.reshape(depth, N_GROUPS, EPG * D_EXPERT, d)
    rw = jnp.pad(router_w.astype(F32), ((0, 0), (0, LANES - N_EXPERTS)))
    rw_hi = rw.astype(BF16)
    router_w_pad = jnp.stack([rw_hi, (rw - rw_hi.astype(F32)).astype(BF16)])

    xp = x_prompt.reshape(n_ctx * seq, d)
    pos = jnp.asarray(_grid_pos_embed(dec_seq, d), dtype=x_sample.dtype)
    xs = _add_pos(x_sample, pos).reshape(n_dec * dec_seq, d)

    states = None
    for l in range(depth):
        lanev = jnp.zeros((8, LANES), F32)
        lanev = lanev.at[0].set(_lane_row((LANE_ALPHA, dn_a_log[l]), (LANE_DT, ssd_a_log[l])))
        lanev = lanev.at[1].set(_lane_row((LANE_ALPHA, dn_dt_bias[l]), (LANE_DT, ssd_dt_bias[l])))
        lanev = lanev.at[2].set(dn_norm_g[l].astype(F32))
        ssdv = jnp.zeros((8, C_W), F32).at[0].set(jnp.repeat(ssd_d[l].astype(F32), P_C)).at[1].set(ssd_norm_g[l])
        final = l == depth - 1

        def block(x, mod6, nb, seq_len, seq_rows, s_dn0, s_ssd0, state_out):
            proj = _inproj(x, mod6, seq_rows, norm1_g[l], w_in_r, l).reshape(nb, seq_len, D_IN_PAD)
            outs = _mixer(proj, dn_conv_w[l], sc_conv_w[l], ssd_conv_w[l], lanev, ssdv, s_dn0, s_ssd0, state_out)
            ycat = outs[0].reshape(nb * seq_len, d)
            x1, h2, comb_t = _outproj(ycat, x, mod6, seq_rows, w_out, l, norm2_g[l], router_w_pad, router_b)
            comb = comb_t.reshape(N_GROUPS, EPG, -1)
            x2 = _moe(h2, comb, exp_w_gate, exp_w_up, w_down_g, l, x1, mod6, seq_rows, final_norm_g, final)
            return x2, outs[1:]

        xp, states = block(xp, mod[l, 0:1], n_ctx, seq, n_ctx * seq, None, None, (l, depth, states))
        xs, _ = block(xs, mod[l, 1:1 + n_dec], n_dec, dec_seq, dec_seq,
                      state_delta[:, l].astype(F32), state_ssd[:, l].astype(F32), None)

    return (xp.reshape(n_ctx, seq, d), xs.reshape(n_dec, dec_seq, d), states[0], states[1])
```

```python
import functools
import math
import types

import jax
import jax.numpy as jnp
import numpy as np
from jax import lax
from jax.experimental import pallas as pl
from jax.experimental.pallas import tpu as pltpu

F32 = jnp.float32
BF16 = jnp.bfloat16

D_MODEL = 1024
GRID_W = 64
POS_BASE = 10000.0
H_A, DK_A, DV_A = 4, 128, 128
A_W = H_A * DV_A
H_C, P_C, N_C, G_C = 4, 64, 64, 2
HPG = H_C // G_C
C_W = H_C * P_C
B_W = D_MODEL - A_W - C_W
XBC_W = C_W + 2 * G_C * N_C
CHUNK = 64
N_EXPERTS = 16
N_GROUPS = 4
EPG = N_EXPERTS // N_GROUPS
D_EXPERT = 256
EPS = 1e-6
LANES = 128

COL_Q, COL_K, COL_V, COL_GATE = 0, A_W, 2 * A_W, 3 * A_W
COL_SCH = 4 * A_W
COL_SCB = COL_SCH + B_W
COL_SCC = COL_SCB + B_W
COL_Z = COL_SCC + B_W
COL_XBC = COL_Z + C_W
COL_SMALL = COL_XBC + XBC_W
D_IN_PAD = COL_SMALL + LANES
LANE_BETA, LANE_ALPHA, LANE_DT = 0, 2 * H_A, 4 * H_A
SRC_SMALL_A = 4 * A_W
SRC_SCH = SRC_SMALL_A + 4 * H_A
SRC_SMALL_B = SRC_SCH + 3 * B_W + C_W + XBC_W
D_IN = SRC_SMALL_B + 2 * H_C

assert HPG == 2 and P_C == N_C == CHUNK and HPG * P_C == LANES and G_C * N_C == LANES and DK_A == DV_A == LANES

VMEM_LIMIT = 56 * 1024 * 1024
STATIC_PREP_CHUNKS = 4


def _dot(a, b):
    return jnp.dot(a.astype(BF16), b.astype(BF16), preferred_element_type=F32)


def _dot_nt(a, b):
    return lax.dot_general(a.astype(BF16), b.astype(BF16), (((1,), (1,)), ((), ())), preferred_element_type=F32)


def _dot_tn(a, b):
    return lax.dot_general(a.astype(BF16), b.astype(BF16), (((0,), (0,)), ((), ())), preferred_element_type=F32)


def _dot_f32(a, b):
    return jnp.dot(a, b, precision=lax.Precision.HIGHEST, preferred_element_type=F32)


def _dot_nt_f32(a, b):
    return lax.dot_general(a, b, (((1,), (1,)), ((), ())), precision=lax.Precision.HIGHEST,
                           preferred_element_type=F32)


def _silu(x):
    h = 0.5 * x
    return h + h * jnp.tanh(h)


def _sigmoid(x):
    return 1.0 / (1.0 + jnp.exp(-x))


def _softplus(x):
    return jnp.maximum(x, 0.0) + jnp.log1p(jnp.exp(-jnp.abs(x)))


def _rms(x, g):
    return x * lax.rsqrt(jnp.mean(x * x, axis=-1, keepdims=True) + EPS) * g


def _tile(n, pref):
    t = min(n, pref)
    while n % t:
        t -= 8
    assert t > 0 and t % 8 == 0, (n, pref)
    return t


def _params(sem):
    return pltpu.CompilerParams(dimension_semantics=sem, vmem_limit_bytes=VMEM_LIMIT)


def _mod_kernel(cond_ref, w_ref, b_ref, o_ref):
    s = _silu(cond_ref[...])
    o_ref[0] = _dot_f32(s, w_ref[0]) + b_ref[0]


def _modulation(cond_rows, mod_w, mod_b):
    depth, d, n = mod_w.shape
    r = cond_rows.shape[0]
    tn = _tile(n, 1536)
    return pl.pallas_call(
        _mod_kernel,
        grid=(depth, n // tn),
        in_specs=[pl.BlockSpec((r, d), lambda l, j: (0, 0)),
                  pl.BlockSpec((1, d, tn), lambda l, j: (l, 0, j)),
                  pl.BlockSpec((1, 1, tn), lambda l, j: (l, 0, j))],
        out_specs=pl.BlockSpec((1, r, tn), lambda l, j: (l, 0, j)),
        out_shape=jax.ShapeDtypeStruct((depth, r, n), F32),
        compiler_params=_params(("arbitrary", "arbitrary")),
        name="modulation",
    )(cond_rows, mod_w, mod_b.reshape(depth, 1, n))


def _add_kernel(x_ref, p_ref, o_ref):
    o_ref[0] = x_ref[0] + p_ref[...]


def _add_pos(x, pos):
    nb, l, d = x.shape
    tl = _tile(l, 512)
    return pl.pallas_call(
        _add_kernel,
        grid=(nb, l // tl),
        in_specs=[pl.BlockSpec((1, tl, d), lambda b, i: (b, i, 0)),
                  pl.BlockSpec((tl, d), lambda b, i: (i, 0))],
        out_specs=pl.BlockSpec((1, tl, d), lambda b, i: (b, i, 0)),
        out_shape=jax.ShapeDtypeStruct(x.shape, x.dtype),
        compiler_params=_params(("arbitrary", "arbitrary")),
        name="add_pos",
    )(x, pos)


def _grid_pos_embed(n_tok, dim):
    rows = n_tok // GRID_W
    rr, cc = np.meshgrid(np.arange(rows, dtype=np.float64), np.arange(GRID_W, dtype=np.float64), indexing="ij")
    quarter = dim // 4
    omega = 1.0 / (POS_BASE ** (np.arange(quarter, dtype=np.float64) / quarter))
    ang_r = rr.reshape(-1, 1) * omega
    ang_c = cc.reshape(-1, 1) * omega
    return np.concatenate([np.sin(ang_r), np.cos(ang_r), np.sin(ang_c), np.cos(ang_c)], axis=-1)


def _inproj_kernel(x_ref, mod_ref, g_ref, w_ref, o_ref, wr_ref):
    @pl.when(pl.program_id(0) == 0)
    def _():
        rb = 256
        for r in range(0, w_ref.shape[0], rb):
            w = w_ref[r:r + rb, :]
            wr_ref[r:r + rb, :COL_SCH] = w[:, :SRC_SMALL_A]
            wr_ref[r:r + rb, COL_SCH:COL_SMALL] = w[:, SRC_SCH:SRC_SMALL_B]
            wr_ref[r:r + rb, COL_SMALL:] = jnp.concatenate(
                [w[:, SRC_SMALL_A:SRC_SCH], w[:, SRC_SMALL_B:], jnp.zeros((rb, D_IN_PAD - D_IN), w.dtype)], axis=1)

    m = mod_ref[0]
    shift, scale = m[0:1, :], m[1:2, :]
    h = _rms(x_ref[...], g_ref[...]) * (1.0 + scale) + shift
    o_ref[...] = _dot(h, wr_ref[...])


def _inproj(x, mod6, seq_rows, norm_g, w_in_r, layer):
    t, d = x.shape
    tm = _tile(seq_rows, 512)
    return pl.pallas_call(
        _inproj_kernel,
        grid=(t // tm,),
        in_specs=[pl.BlockSpec((tm, d), lambda i: (i, 0)),
                  pl.BlockSpec((1, 6, d), lambda i: ((i * tm) // seq_rows, 0, 0)),
                  pl.BlockSpec((1, d), lambda i: (0, 0)),
                  pl.BlockSpec((None, d, D_IN), lambda i: (layer, 0, 0), pipeline_mode=pl.Buffered(1))],
        out_specs=pl.BlockSpec((tm, D_IN_PAD), lambda i: (i, 0)),
        out_shape=jax.ShapeDtypeStruct((t, D_IN_PAD), F32),
        scratch_shapes=[pltpu.VMEM((d, D_IN_PAD), BF16)],
        compiler_params=_params(("arbitrary",)),
        name="inproj",
    )(x, mod6, norm_g.reshape(1, d), w_in_r)


CONV_WIN = CHUNK + 16


def _conv_select(r0, w0):
    ti = lax.broadcasted_iota(jnp.int32, (2 * CHUNK, 2 * CONV_WIN), 0)
    tj = lax.broadcasted_iota(jnp.int32, (2 * CHUNK, 2 * CONV_WIN), 1)
    want = r0 + jnp.where(ti < CHUNK, ti - 1, ti - CHUNK + 1)
    have = w0 + jnp.where(tj < CONV_WIN, tj, tj - CONV_WIN)
    return jnp.where(want == have, 1.0, 0.0).astype(BF16)


def _conv3(load, r0, w0, sel, w):
    cur = load(pl.ds(r0, CHUNK))
    win = load(pl.ds(w0, CONV_WIN))
    hi = win.astype(BF16)
    lo = (win - hi.astype(F32)).astype(BF16)
    nb = jnp.dot(sel, jnp.concatenate([hi, lo], axis=0), preferred_element_type=F32)
    return w[0:1, :] * nb[:CHUNK] + w[1:2, :] * cur + w[2:3, :] * nb[CHUNK:]


def _mixer_kernel(*refs, seqs_per_step, batched_in, n_in, n_out, **static):
    ins, outs, scratch = refs[:n_in], refs[n_in:n_in + n_out], refs[n_in + n_out:]
    n_scr = len(scratch) // seqs_per_step
    seqs = []
    for s in range(seqs_per_step):
        one = lambda r, s=s: r.at[pl.ds(s, 1)]
        seqs.append(_mixer_phases([one(r) if i in batched_in else r for i, r in enumerate(ins)]
                                  + [one(r) for r in outs] + list(scratch[s * n_scr:(s + 1) * n_scr]), **static))
    nc, cpi = seqs[0].nc, seqs[0].cpi
    if nc <= STATIC_PREP_CHUNKS:
        for q in seqs:
            for z in range(nc):
                q.prep(z, 0)
            q.init()
        for q in seqs:
            for i in range(nc // cpi):
                q.delta_prep(i, 0)
        for z in range(nc):
            _scan_step(seqs, z)
        for q in seqs:
            for z in range(nc):
                q.finish(z, 0)
    else:
        for q in seqs:
            lax.fori_loop(0, nc, q.prep, 0, unroll=2)
            q.init()
            lax.fori_loop(0, nc // cpi, q.delta_prep, 0)

        def scan(z, carry):
            _scan_step(seqs, z)
            return carry

        lax.fori_loop(0, nc, scan, 0, unroll=4)
        for q in seqs:
            lax.fori_loop(0, nc, q.finish, 0, unroll=2)
    for q in seqs:
        q.emit()


def _scan_step(seqs, z):
    states = [q.scan_begin(z) for q in seqs]
    for stage in range(len(seqs[0].scan_stages)):
        for q, st in zip(seqs, states):
            q.scan_stages[stage](st)


def _mixer_phases(refs, seq_len, zero_init, state_layer, state_layers, state_aliased):
    refs = list(refs)
    proj, dnw, scw, ssw, lanev, ssdv = refs[:6]
    k = 6
    if not zero_init:
        sdn0, sssd0 = refs[k:k + 2]
        k += 2
    emit_state = state_layer is not None
    if state_aliased:
        k += 2
    ycat = refs[k]
    k += 1
    if emit_state:
        sdn_out, sssd_out = refs[k:k + 2]
        k += 2
    (qkv_s, xbc_s, bdup_s, cdup_s, gc_s, dpair_s, gcpair_s, tot_s, sp_s, beta_s, u_s, wq_s, kd_s, qk_s, o_s, y_s,
     st_s, hs_s) = refs[k:]

    nc = seq_len // CHUNK
    ri = lax.broadcasted_iota(jnp.int32, (CHUNK, CHUNK), 0)
    ci = lax.broadcasted_iota(jnp.int32, (CHUNK, CHUNK), 1)
    tril = (ri >= ci).astype(F32)
    r128 = lax.broadcasted_iota(jnp.int32, (LANES, LANES), 0)
    c128 = lax.broadcasted_iota(jnp.int32, (LANES, LANES), 1)
    eye128 = (r128 == c128).astype(F32)
    n_lvl = int(math.log2(CHUNK))
    alog = lanev[0:1, :]
    bias = lanev[1:2, :]
    pw = 2 * CHUNK
    ri2 = lax.broadcasted_iota(jnp.int32, (CHUNK, pw), 0)
    ci2 = lax.broadcasted_iota(jnp.int32, (CHUNK, pw), 1)
    lane_hi = ci2 >= CHUNK
    tj = ci2 & (CHUNK - 1)
    incl2 = (ri2 >= tj, ri2 <= tj)
    ahead = jnp.where(lane_hi, tj - ri2, ri2 - tj)
    incl_fb = ahead >= 0
    strict_fb = ahead > 0
    eye2 = (ri2 == tj).astype(F32)
    lvl2 = [((ri2 >> s) == (tj >> s)) & ((ri2 >> (s - 1)) != (tj >> (s - 1))) for s in range(1, n_lvl + 1)]

    def block_diag(x):
        return jnp.concatenate([jnp.where(lane_hi, 0.0, x), jnp.where(lane_hi, x, 0.0)], axis=0).astype(BF16)

    def chunk_rows(z):
        return pl.ds(z * CHUNK if isinstance(z, int) else pl.multiple_of(z * CHUNK, CHUNK), CHUNK)

    def prep(z, carry):
        r0 = z * CHUNK if isinstance(z, int) else pl.multiple_of(z * CHUNK, CHUNK)
        rows = pl.ds(r0, CHUNK)
        if isinstance(z, int):
            w0 = min(max(r0 - 8, 0), seq_len - CONV_WIN)
        else:
            w0 = pl.multiple_of(jnp.clip(r0 - 8, 0, seq_len - CONV_WIN), 8)
        sel = _conv_select(r0, w0)
        cw = 2 * LANES
        for jj in range(3 * A_W // cw):
            c0 = jj * cw
            a2 = _silu(_conv3(lambda rs: proj[0, rs, c0:c0 + cw], r0, w0, sel, dnw[:, c0:c0 + cw]))
            for half in range(cw // LANES):
                a = a2[:, half * LANES:(half + 1) * LANES]
                j = jj * (cw // LANES) + half
                if j < 2 * H_A:
                    a = a * lax.rsqrt(jnp.sum(a * a, axis=-1, keepdims=True) + EPS)
                if j < H_A:
                    a = a * (DK_A ** -0.5)
                qkv_s[rows, j * LANES:(j + 1) * LANES] = a
        for jj in range(B_W // cw):
            c0 = jj * cw
            cv = _conv3(lambda rs: proj[0, rs, COL_SCC + c0:COL_SCC + c0 + cw]
                        * proj[0, rs, COL_SCH + c0:COL_SCH + c0 + cw], r0, w0, sel, scw[:, c0:c0 + cw])
            yb = proj[0, rows, COL_SCB + c0:COL_SCB + c0 + cw] * cv
            ycat[0, rows, A_W + c0:A_W + c0 + cw] = yb.astype(ycat.dtype)
        for jj in range(XBC_W // cw):
            c1 = jj * cw
            a2 = _silu(_conv3(lambda rs: proj[0, rs, COL_XBC + c1:COL_XBC + c1 + cw], r0, w0, sel,
                              ssw[:, c1:c1 + cw]))
            if c1 < C_W:
                xbc_s[rows, c1:c1 + cw] = a2
                continue
            for half, dup_s in enumerate((bdup_s, cdup_s)):
                a = a2[:, half * LANES:(half + 1) * LANES]
                swapped = pltpu.roll(a, N_C, 1)
                lo = lax.broadcasted_iota(jnp.int32, a.shape, 1) < N_C
                dup_s[rows, 0:LANES] = jnp.where(lo, a, swapped)
                dup_s[rows, LANES:2 * LANES] = jnp.where(lo, swapped, a)
        sm = proj[0, rows, COL_SMALL:COL_SMALL + LANES]
        sp = _softplus(sm + bias)
        g = -jnp.exp(alog) * sp
        pre = _dot_f32(tril, g)
        tot = pre[CHUNK - 1:CHUNK, :]
        suf = tot - pre + g
        sp_s[rows, :] = sp
        beta_s[rows, :] = _sigmoid(sm)
        gc_s[0, rows, :] = pre
        gc_s[1, rows, :] = suf
        grs = [_dot_nt_f32(eye128, gcd) for gcd in (pre, suf)]
        for d, gr in enumerate(grs):
            for g in range(G_C):
                ln = LANE_DT + d * H_C + g * HPG
                gcpair_s[d, z, g:g + 1, :] = jnp.concatenate([gr[ln:ln + 1, :], gr[ln + 1:ln + 2, :]], axis=1)
        for h in range(H_A):
            lf, lb = LANE_ALPHA + h, LANE_ALPHA + H_A + h
            dpair_s[z, h:h + 1, :] = jnp.concatenate([grs[0][lf:lf + 1, :], grs[1][lb:lb + 1, :]], axis=1)
        tot_s[z] = jnp.broadcast_to(tot, (8, LANES))
        o_s[rows, :] = jnp.zeros((CHUNK, A_W), F32)
        y_s[rows, :] = jnp.zeros((CHUNK, C_W), F32)
        return carry

    def init():
        for d in range(2):
            for h in range(H_A):
                st_s[d * H_A + h] = jnp.zeros((DK_A, DV_A), F32) if zero_init else sdn0[0, d, h]
            for g in range(G_C):
                if zero_init:
                    hs_s[d, g] = jnp.zeros((HPG * P_C, HPG * N_C), F32)
                else:
                    zero = jnp.zeros((P_C, N_C), F32)
                    hs_s[d, g] = jnp.concatenate(
                        [jnp.concatenate([sssd0[0, d, g * HPG], zero], axis=1),
                         jnp.concatenate([zero, sssd0[0, d, g * HPG + 1]], axis=1)], axis=0)

    cpi = 4 if nc % 4 == 0 else 2

    def delta_prep(i, carry):
        units = []
        for zz in range(cpi):
            z = cpi * i + zz
            rows = chunk_rows(z)
            tot = tot_s[z][0:1, :]
            beta = beta_s[rows, :]
            gcs = [gc_s[d, rows, :] for d in range(2)]
            for h in range(H_A):
                q_h = qkv_s[rows, COL_Q + h * DK_A:COL_Q + (h + 1) * DK_A]
                k_h = qkv_s[rows, COL_K + h * DK_A:COL_K + (h + 1) * DK_A]
                v_h = qkv_s[rows, COL_V + h * DV_A:COL_V + (h + 1) * DV_A]
                units.append(dict(z=z, h=h, q=q_h, k=k_h, v=v_h, tot=tot, beta=beta, gcs=gcs))
        qkk = [_dot_nt(jnp.concatenate([p["q"], p["k"]], axis=0), jnp.concatenate([p["k"], p["k"]], axis=0))
               for p in units]
        ms, rhss = [], []
        for p, qk_kk in zip(units, qkk):
            z, h = p["z"], p["h"]
            lf, lb = LANE_ALPHA + h, LANE_ALPHA + H_A + h
            bf, bb = LANE_BETA + h, LANE_BETA + H_A + h
            a_pair = jnp.where(lane_hi, p["gcs"][1][:, lb:lb + 1], p["gcs"][0][:, lf:lf + 1])
            b_pair = jnp.where(lane_hi, p["beta"][:, bb:bb + 1], p["beta"][:, bf:bf + 1])
            decay = jnp.exp(jnp.where(incl_fb, a_pair - dpair_s[z, h:h + 1, :], -1e30))
            ms.append(jnp.where(strict_fb, qk_kk[CHUNK:] * b_pair * decay, 0.0))
            qk_s[z, h] = (qk_kk[:CHUNK] * decay).astype(qk_s.dtype)
            rhs_d = []
            for d, (ln, bl) in enumerate(((lf, bf), (lb, bb))):
                a_col = p["gcs"][d][:, ln:ln + 1]
                b_col = p["beta"][:, bl:bl + 1]
                eg = jnp.exp(a_col)
                rhs_d.append(jnp.concatenate([p["v"] * b_col, p["k"] * (b_col * eg)], axis=1))
                kd_s[d, z, h] = (p["k"] * jnp.exp(p["tot"][:, ln:ln + 1] - a_col)).astype(kd_s.dtype)
                wq_s[d, z, h, CHUNK:, :] = (p["q"] * eg).astype(wq_s.dtype)
            zero = jnp.zeros_like(rhs_d[0])
            rhss.append(jnp.concatenate([jnp.concatenate([rhs_d[0], zero], axis=1),
                                         jnp.concatenate([zero, rhs_d[1]], axis=1)], axis=0))
        t_inv = [eye2 - jnp.where(lvl2[0], m, 0.0) for m in ms]
        for s in range(1, n_lvl):
            x = [_dot(jnp.where(lvl2[s], m, 0.0), block_diag(t)) for m, t in zip(ms, t_inv)]
            t_inv = [t - _dot(t, block_diag(xx)) for t, xx in zip(t_inv, x)]
        uw = [_dot(t, rhs) for t, rhs in zip(t_inv, rhss)]
        for p, r in zip(units, uw):
            z, h = p["z"], p["h"]
            for d in range(2):
                c0 = d * (DV_A + DK_A)
                u_s[d, z, h] = r[:, c0:c0 + DV_A]
                wq_s[d, z, h, :CHUNK, :] = r[:, c0 + DV_A:c0 + DV_A + DK_A].astype(wq_s.dtype)
        return carry

    rb =lax.broadcasted_iota(jnp.int32, (pw, pw), 0) >= P_C
    cbk = lax.broadcasted_iota(jnp.int32, (pw, pw), 1) >= N_C
    diag_blk = rb == cbk

    def scan_begin(z):
        dus = [(d, (z if d == 0 else nc - 1 - z), h) for d in range(2) for h in range(H_A)]
        s_prev = [st_s[d * H_A + h] for d, _, h in dus]
        us = []
        for d in range(2):
            zc = z if d == 0 else nc - 1 - z
            rows = chunk_rows(zc)
            sp = sp_s[rows, :]
            tot = tot_s[zc][0:1, :]
            gc = gc_s[d, rows, :]
            for g in range(G_C):
                ln = LANE_DT + d * H_C + g * HPG
                gsl = slice(g * pw, (g + 1) * pw)
                a_pair = jnp.where(lane_hi, gc[:, ln + 1:ln + 2], gc[:, ln:ln + 1])
                t_pair = jnp.where(lane_hi, tot[:, ln + 1:ln + 2], tot[:, ln:ln + 1])
                lmat = jnp.exp(jnp.where(incl2[d], a_pair - gcpair_s[d, zc, g:g + 1, :], -1e30))
                xdt = xbc_s[rows, gsl] * jnp.where(lane_hi, sp[:, ln + 1:ln + 2], sp[:, ln:ln + 1])
                b_dup = bdup_s[rows, gsl]
                c_dup = cdup_s[rows, gsl]
                us.append(dict(d=d, g=g, rows=rows, gsl=gsl, lmat=lmat, xdt=xdt, c_dup=c_dup,
                               c_lo=jnp.where(lane_hi, 0.0, c_dup), b_st=jnp.concatenate([b_dup, b_dup], axis=0),
                               bdec=b_dup * jnp.exp(t_pair - a_pair), ea=jnp.exp(a_pair),
                               dec=jnp.where(rb, jnp.exp(tot[:, ln + 1:ln + 2]), jnp.exp(tot[:, ln:ln + 1])),
                               x_bd=jnp.concatenate([jnp.where(lane_hi, 0.0, xdt), jnp.where(lane_hi, xdt, 0.0)],
                                                    axis=0),
                               h_prev=hs_s[d, g]))
        return dict(dus=dus, s_prev=s_prev, us=us)

    def scan_delta_1(t):
        t["ws_qs"] = [_dot(wq_s[idx], s) for idx, s in zip(t["dus"], t["s_prev"])]

    def scan_ssd_1(t):
        t["st"] = [_dot_tn(p["xdt"], p["bdec"]) for p in t["us"]]
        t["cb"] = [_dot_nt(p["c_lo"], p["b_st"]) for p in t["us"]]
        t["y_off"] = [_dot_nt(p["c_dup"], p["h_prev"]) for p in t["us"]]

    def scan_delta_2(t):
        v_new = [u_s[idx] - r[:CHUNK] for idx, r in zip(t["dus"], t["ws_qs"])]
        zero = jnp.zeros((CHUNK, DV_A), F32)
        t["o_in"] = [_dot(qk_s[zc, h], jnp.concatenate([v, zero] if d == 0 else [zero, v], axis=0))
                     for (d, zc, h), v in zip(t["dus"], v_new)]
        t["s_add"] = [_dot_tn(kd_s[idx], v) for idx, v in zip(t["dus"], v_new)]

    def scan_ssd_2(t):
        t["y_diag"] = [_dot(cbd * p["lmat"], p["x_bd"]) for p, cbd in zip(t["us"], t["cb"])]

    def scan_store(t):
        for p, yd, yo, s in zip(t["us"], t["y_diag"], t["y_off"], t["st"]):
            y_s[p["rows"], p["gsl"]] = y_s[p["rows"], p["gsl"]] + yd + yo * p["ea"]
            hs_s[p["d"], p["g"]] = p["h_prev"] * p["dec"] + jnp.where(diag_blk, s, 0.0)
        for (d, zc, h), r, oi, sa, s in zip(t["dus"], t["ws_qs"], t["o_in"], t["s_add"], t["s_prev"]):
            ln = LANE_ALPHA + d * H_A + h
            cs = slice(h * DV_A, (h + 1) * DV_A)
            o_s[chunk_rows(zc), cs] = o_s[chunk_rows(zc), cs] + r[CHUNK:] + oi
            st_s[d * H_A + h] = s * jnp.exp(tot_s[zc][0:1, ln:ln + 1]) + sa


    def finish(z, carry):
        rows = chunk_rows(z)
        for h in range(H_A):
            cs = slice(h * DV_A, (h + 1) * DV_A)
            o = _rms(o_s[rows, cs], lanev[2:3, :])
            o = o * _silu(proj[0, rows, COL_GATE + h * DV_A:COL_GATE + (h + 1) * DV_A])
            ycat[0, rows, cs] = o.astype(ycat.dtype)
        y = y_s[rows, :] + ssdv[0:1, :] * xbc_s[rows, 0:C_W]
        y = _rms(y * _silu(proj[0, rows, COL_Z:COL_Z + C_W]), ssdv[1:2, :])
        ycat[0, rows, A_W + B_W:] = y.astype(ycat.dtype)
        return carry

    def emit():
        if not emit_state:
            return
        slots = (None,) if state_aliased else range(state_layers)
        for slot in slots:
            dn_slot = sdn_out.at[0] if slot is None else sdn_out.at[0, slot]
            ssd_slot = sssd_out.at[0] if slot is None else sssd_out.at[0, slot]
            mine = slot is None or slot == state_layer
            for d in range(2):
                for h in range(H_A):
                    dn_slot[d, h] = st_s[d * H_A + h] if mine else jnp.zeros((DK_A, DV_A), F32)
                for h in range(H_C):
                    k0 = (h % HPG) * P_C
                    ssd_slot[d, h] = (hs_s[d, h // HPG][k0:k0 + P_C, k0:k0 + N_C] if mine
                                      else jnp.zeros((P_C, N_C), F32))

    return types.SimpleNamespace(
        nc=nc, cpi=cpi, prep=prep, init=init, delta_prep=delta_prep, scan_begin=scan_begin,
        scan_stages=(scan_delta_1, scan_ssd_1, scan_delta_2, scan_ssd_2, scan_store), finish=finish, emit=emit)


def _mixer(proj, dn_conv_w, sc_conv_w, ssd_conv_w, lanev, ssdv, s_dn0, s_ssd0, state_out):
    nb, seq_len, _ = proj.shape
    nc = seq_len // CHUNK
    assert nc % 2 == 0
    zero_init = s_dn0 is None
    spg = 2 if nc <= STATIC_PREP_CHUNKS and nb % 2 == 0 else 1
    full = lambda a: pl.BlockSpec(a.shape, lambda b: (0,) * a.ndim)
    args = [proj, dn_conv_w, sc_conv_w, ssd_conv_w, lanev, ssdv]
    batched_in = [0]
    in_specs = [pl.BlockSpec((spg, seq_len, D_IN_PAD), lambda b: (b, 0, 0), pipeline_mode=pl.Buffered(1))
                if seq_len > 512 else pl.BlockSpec((spg, seq_len, D_IN_PAD), lambda b: (b, 0, 0)),
                full(dn_conv_w), full(sc_conv_w), full(ssd_conv_w), full(lanev), full(ssdv)]
    if not zero_init:
        batched_in += [len(args), len(args) + 1]
        args += [s_dn0, s_ssd0]
        in_specs += [pl.BlockSpec((spg, 2, H_A, DK_A, DV_A), lambda b: (b, 0, 0, 0, 0)),
                     pl.BlockSpec((spg, 2, H_C, P_C, N_C), lambda b: (b, 0, 0, 0, 0))]
    out_shape = [jax.ShapeDtypeStruct((nb, seq_len, D_MODEL), BF16)]
    out_specs = [pl.BlockSpec((spg, seq_len, D_MODEL), lambda b: (b, 0, 0))]
    aliases = {}
    layer, depth, prev = state_out if state_out is not None else (None, None, None)
    if state_out is not None:
        out_shape += [jax.ShapeDtypeStruct((nb, depth, 2, H_A, DK_A, DV_A), F32),
                      jax.ShapeDtypeStruct((nb, depth, 2, H_C, P_C, N_C), F32)]
        if prev is None:
            out_specs += [pl.BlockSpec((spg, depth, 2, H_A, DK_A, DV_A), lambda b: (b, 0, 0, 0, 0, 0)),
                          pl.BlockSpec((spg, depth, 2, H_C, P_C, N_C), lambda b: (b, 0, 0, 0, 0, 0))]
        else:
            aliases = {len(args): 1, len(args) + 1: 2}
            args += list(prev)
            in_specs += [pl.BlockSpec(memory_space=pl.ANY), pl.BlockSpec(memory_space=pl.ANY)]
            out_specs += [pl.BlockSpec((spg, None, 2, H_A, DK_A, DV_A), lambda b: (b, layer, 0, 0, 0, 0)),
                          pl.BlockSpec((spg, None, 2, H_C, P_C, N_C), lambda b: (b, layer, 0, 0, 0, 0))]
    scratch = [pltpu.VMEM((seq_len, 3 * A_W), F32),
               pltpu.VMEM((seq_len, C_W), F32),
               pltpu.VMEM((seq_len, G_C * LANES), F32),
               pltpu.VMEM((seq_len, G_C * LANES), F32),
               pltpu.VMEM((2, seq_len, LANES), F32),
               pltpu.VMEM((nc, 8, LANES), F32),
               pltpu.VMEM((2, nc, 8, LANES), F32),
               pltpu.VMEM((nc, 8, LANES), F32),
               pltpu.VMEM((seq_len, LANES), F32),
               pltpu.VMEM((seq_len, LANES), F32),
               pltpu.VMEM((2, nc, H_A, CHUNK, DV_A), F32),
               pltpu.VMEM((2, nc, H_A, 2 * CHUNK, DK_A), BF16),
               pltpu.VMEM((2, nc, H_A, CHUNK, DK_A), BF16),
               pltpu.VMEM((nc, H_A, CHUNK, 2 * CHUNK), BF16),
               pltpu.VMEM((seq_len, A_W), F32),
               pltpu.VMEM((seq_len, C_W), F32),
               pltpu.VMEM((2 * H_A, DK_A, DV_A), F32),
               pltpu.VMEM((2, G_C, HPG * P_C, HPG * N_C), F32)]
    return pl.pallas_call(
        functools.partial(_mixer_kernel, seqs_per_step=spg, batched_in=tuple(batched_in), n_in=len(args),
                          n_out=len(out_shape), seq_len=seq_len, zero_init=zero_init, state_layer=layer,
                          state_layers=depth, state_aliased=prev is not None),
        grid=(nb // spg,),
        in_specs=in_specs,
        out_specs=out_specs,
        out_shape=out_shape,
        input_output_aliases=aliases,
        scratch_shapes=scratch * spg,
        compiler_params=_params(("arbitrary",)),
        name="mixer",
    )(*args)


def _top2_sum(a, b, c, d):
    hi1, lo1 = jnp.maximum(a, b), jnp.minimum(a, b)
    hi2, lo2 = jnp.maximum(c, d), jnp.minimum(c, d)
    return jnp.maximum(hi1, hi2) + jnp.maximum(jnp.minimum(hi1, hi2), jnp.maximum(lo1, lo2))


def _outproj_kernel(y_ref, x_ref, mod_ref, w_ref, g_ref, rw_ref, rb_ref, x1_ref, h2_ref, comb_ref, wb_ref):
    @pl.when(pl.program_id(0) == 0)
    def _():
        wb_ref[...] = w_ref[...].astype(wb_ref.dtype)

    m = mod_ref[0]
    gate1, shift2, scale2 = m[2:3, :], m[3:4, :], m[4:5, :]
    x1 = x_ref[...] + gate1 * _dot(y_ref[...], wb_ref[...])
    x1_ref[...] = x1
    h2 = _rms(x1, g_ref[...]) * (1.0 + scale2) + shift2
    h2_ref[...] = h2.astype(h2_ref.dtype)

    h_hi = h2.astype(BF16)
    h_lo = (h2 - h_hi.astype(F32)).astype(BF16)
    rw_hi, rw_lo = rw_ref[0], rw_ref[1]
    hl = jnp.dot(jnp.concatenate([h_hi, h_lo], axis=0), rw_hi, preferred_element_type=F32)
    tm = h2.shape[0]
    logits = hl[:tm] + hl[tm:] + jnp.dot(h_hi, rw_lo, preferred_element_type=F32)
    scores = _sigmoid(logits.T[:N_EXPERTS, :])
    biased = scores + rb_ref[...]
    sc = [scores[e:e + 1, :] for e in range(N_EXPERTS)]
    bi = [biased[e:e + 1, :] for e in range(N_EXPERTS)]
    gs = [_top2_sum(*bi[EPG * g:EPG * (g + 1)]) for g in range(N_GROUPS)]
    gmax = functools.reduce(jnp.maximum, gs)
    first = []
    taken = None
    for g in range(N_GROUPS):
        hit = gs[g] == gmax
        if taken is None:
            first.append(hit)
            taken = hit
        else:
            first.append(hit & jnp.logical_not(taken))
            taken = taken | hit

    def pick(vals, j):
        out = vals[EPG * (N_GROUPS - 1) + j]
        for g in range(N_GROUPS - 2, -1, -1):
            out = jnp.where(first[g], vals[EPG * g + j], out)
        return out

    ib = [pick(bi, j) for j in range(EPG)]
    isc = [pick(sc, j) for j in range(EPG)]
    sel = []
    for j in range(EPG):
        cnt = jnp.zeros_like(ib[j])
        for i in range(EPG):
            if i == j:
                continue
            ahead = (ib[i] > ib[j]) | ((ib[i] == ib[j]) if i < j else False)
            cnt = cnt + jnp.where(ahead, 1.0, 0.0)
        sel.append(cnt < 2.0)
    wj = [jnp.where(sel[j], isc[j], 0.0) for j in range(EPG)]
    denom = functools.reduce(lambda a, b: a + b, wj)
    for g in range(N_GROUPS):
        for j in range(EPG):
            comb_ref[EPG * g + j:EPG * g + j + 1, :] = jnp.where(first[g], wj[j] / denom, 0.0)


def _outproj(ycat, x, mod6, seq_rows, w_out, layer, norm_g, router_w_pad, router_b):
    t, d = x.shape
    tm = _tile(seq_rows, 512)
    return pl.pallas_call(
        _outproj_kernel,
        grid=(t // tm,),
        in_specs=[pl.BlockSpec((tm, d), lambda i: (i, 0)),
                  pl.BlockSpec((tm, d), lambda i: (i, 0)),
                  pl.BlockSpec((1, 6, d), lambda i: ((i * tm) // seq_rows, 0, 0)),
                  pl.BlockSpec((None, d, d), lambda i: (layer, 0, 0)),
                  pl.BlockSpec((1, d), lambda i: (0, 0)),
                  pl.BlockSpec((2, d, LANES), lambda i: (0, 0, 0)),
                  pl.BlockSpec((N_EXPERTS, 1), lambda i: (0, 0))],
        out_specs=[pl.BlockSpec((tm, d), lambda i: (i, 0)),
                   pl.BlockSpec((tm, d), lambda i: (i, 0)),
                   pl.BlockSpec((N_EXPERTS, tm), lambda i: (0, i))],
        out_shape=[jax.ShapeDtypeStruct((t, d), F32),
                   jax.ShapeDtypeStruct((t, d), BF16),
                   jax.ShapeDtypeStruct((N_EXPERTS, t), F32)],
        scratch_shapes=[pltpu.VMEM((d, d), BF16)],
        compiler_params=_params(("arbitrary",)),
        name="outproj_route",
    )(ycat, x, mod6, w_out, norm_g.reshape(1, d), router_w_pad, router_b.reshape(N_EXPERTS, 1))


def _moe_kernel(h_ref, comb_ref, wg_ref, wu_ref, wd_ref, x1_ref, mod_ref, fg_ref, o_ref, *, final):
    g = pl.program_id(1)

    @pl.when(g == 0)
    def _():
        o_ref[...] = jnp.zeros_like(o_ref)

    h = h_ref[...]
    comb = comb_ref[...].T
    acts = [(_silu(_dot(h, wg_ref[j])) * _dot(h, wu_ref[j]) * comb[:, j:j + 1]).astype(BF16) for j in range(EPG)]
    o_ref[...] += _dot(jnp.concatenate(acts, axis=1), wd_ref[...])

    @pl.when(g == pl.num_programs(1) - 1)
    def _():
        x2 = x1_ref[...] + mod_ref[0][5:6, :] * o_ref[...]
        o_ref[...] = _rms(x2, fg_ref[...]) if final else x2


def _moe(h2, comb, w_gate, w_up, w_down, layer, x1, mod6, seq_rows, final_g, final):
    t, d = x1.shape
    tm = _tile(seq_rows, 1024)
    return pl.pallas_call(
        functools.partial(_moe_kernel, final=final),
        grid=(t // tm, N_GROUPS),
        in_specs=[pl.BlockSpec((tm, d), lambda i, g: (i, 0)),
                  pl.BlockSpec((None, EPG, tm), lambda i, g: (g, 0, i)),
                  pl.BlockSpec((None, EPG, d, D_EXPERT), lambda i, g: (layer, g, 0, 0)),
                  pl.BlockSpec((None, EPG, d, D_EXPERT), lambda i, g: (layer, g, 0, 0)),
                  pl.BlockSpec((None, None, EPG * D_EXPERT, d), lambda i, g: (layer, g, 0, 0)),
                  pl.BlockSpec((tm, d), lambda i, g: (i, 0)),
                  pl.BlockSpec((1, 6, d), lambda i, g: ((i * tm) // seq_rows, 0, 0)),
                  pl.BlockSpec((1, d), lambda i, g: (0, 0))],
        out_specs=pl.BlockSpec((tm, d), lambda i, g: (i, 0)),
        out_shape=jax.ShapeDtypeStruct((t, d), F32),
        compiler_params=_params(("arbitrary", "arbitrary")),
        name="experts",
    )(h2, comb, w_gate, w_up, w_down, x1, mod6, final_g.reshape(1, d))


def _lane_row(*pieces):
    row = jnp.zeros((LANES,), F32)
    for lane, vals in pieces:
        row = lax.dynamic_update_slice(row, vals.reshape(-1).astype(F32), (lane,))
    return row


def kernel(x_prompt, x_sample, state_delta, state_ssd, c, c_ctx, mod_w, mod_b, norm1_g, norm2_g, w_in, w_out,
           dn_conv_w, dn_a_log, dn_dt_bias, dn_norm_g, sc_conv_w, ssd_conv_w, ssd_a_log, ssd_dt_bias, ssd_d,
           ssd_norm_g, router_w, router_b, exp_w_gate, exp_w_up, exp_w_down, final_norm_g):
    depth = mod_w.shape[0]
    n_ctx, seq, d = x_prompt.shape
    n_dec, dec_seq, _ = x_sample.shape
    assert seq % CHUNK == 0 and dec_seq % CHUNK == 0 and d == D_MODEL

    n_rows = -(-(1 + n_dec) // 8) * 8
    cond_rows = jnp.zeros((n_rows, d), F32).at[0].set(c_ctx).at[1:1 + n_dec].set(c)
    mod = _modulation(cond_rows, mod_w, mod_b).reshape(depth, n_rows, 6, d)

    assert w_in.shape[-1] == D_IN
    w_in_r = w_in.astype(BF16)
    w_down_g = exp_w_down.reshape(depth, N_GROUPS, EPG * D_EXPERT, d)
    rw = jnp.pad(router_w.astype(F32), ((0, 0), (0, LANES - N_EXPERTS)))
    rw_hi = rw.astype(BF16)
    router_w_pad = jnp.stack([rw_hi, (rw - rw_hi.astype(F32)).astype(BF16)])

    xp = x_prompt.reshape(n_ctx * seq, d)
    pos = jnp.asarray(_grid_pos_embed(dec_seq, d), dtype=x_sample.dtype)
    xs = _add_pos(x_sample, pos).reshape(n_dec * dec_seq, d)

    states = None
    for l in range(depth):
        lanev = jnp.zeros((8, LANES), F32)
        lanev = lanev.at[0].set(_lane_row((LANE_ALPHA, dn_a_log[l]), (LANE_DT, ssd_a_log[l])))
        lanev = lanev.at[1].set(_lane_row((LANE_ALPHA, dn_dt_bias[l]), (LANE_DT, ssd_dt_bias[l])))
        lanev = lanev.at[2].set(dn_norm_g[l].astype(F32))
        ssdv = jnp.zeros((8, C_W), F32).at[0].set(jnp.repeat(ssd_d[l].astype(F32), P_C)).at[1].set(ssd_norm_g[l])
        final = l == depth - 1

        def block(x, mod6, nb, seq_len, seq_rows, s_dn0, s_ssd0, state_out):
            proj = _inproj(x, mod6, seq_rows, norm1_g[l], w_in_r, l).reshape(nb, seq_len, D_IN_PAD)
            outs = _mixer(proj, dn_conv_w[l], sc_conv_w[l], ssd_conv_w[l], lanev, ssdv, s_dn0, s_ssd0, state_out)
            ycat = outs[0].reshape(nb * seq_len, d)
            x1, h2, comb_t = _outproj(ycat, x, mod6, seq_rows, w_out, l, norm2_g[l], router_w_pad, router_b)
            comb = comb_t.reshape(N_GROUPS, EPG, -1)
            x2 = _moe(h2, comb, exp_w_gate, exp_w_up, w_down_g, l, x1, mod6, seq_rows, final_norm_g, final)
            return x2, outs[1:]

        xp, states = block(xp, mod[l, 0:1], n_ctx, seq, n_ctx * seq, None, None, (l, depth, states))
        xs, _ = block(xs, mod[l, 1:1 + n_dec], n_dec, dec_seq, dec_seq,
                      state_delta[:, l].astype(F32), state_ssd[:, l].astype(F32), None)

    return (xp.reshape(n_ctx, seq, d), xs.reshape(n_dec, dec_seq, d), states[0], states[1])
```

```python
import functools
import math
import types

import jax
import jax.numpy as jnp
import numpy as np
from jax import lax
from jax.experimental import pallas as pl
from jax.experimental.pallas import tpu as pltpu

F32 = jnp.float32
BF16 = jnp.bfloat16

D_MODEL = 1024
GRID_W = 64
POS_BASE = 10000.0
H_A, DK_A, DV_A = 4, 128, 128
A_W = H_A * DV_A
H_C, P_C, N_C, G_C = 4, 64, 64, 2
HPG = H_C // G_C
C_W = H_C * P_C
B_W = D_MODEL - A_W - C_W
XBC_W = C_W + 2 * G_C * N_C
CHUNK = 64
N_EXPERTS = 16
N_GROUPS = 4
EPG = N_EXPERTS // N_GROUPS
D_EXPERT = 256
EPS = 1e-6
LANES = 128

COL_Q, COL_K, COL_V, COL_GATE = 0, A_W, 2 * A_W, 3 * A_W
COL_SCH = 4 * A_W
COL_SCB = COL_SCH + B_W
COL_SCC = COL_SCB + B_W
COL_Z = COL_SCC + B_W
COL_XBC = COL_Z + C_W
COL_SMALL = COL_XBC + XBC_W
D_IN_PAD = COL_SMALL + LANES
LANE_BETA, LANE_ALPHA, LANE_DT = 0, 2 * H_A, 4 * H_A
SRC_SMALL_A = 4 * A_W
SRC_SCH = SRC_SMALL_A + 4 * H_A
SRC_SMALL_B = SRC_SCH + 3 * B_W + C_W + XBC_W
D_IN = SRC_SMALL_B + 2 * H_C

assert HPG == 2 and P_C == N_C == CHUNK and HPG * P_C == LANES and G_C * N_C == LANES and DK_A == DV_A == LANES

VMEM_LIMIT = 56 * 1024 * 1024
STATIC_PREP_CHUNKS = 4


def _dot(a, b):
    return jnp.dot(a.astype(BF16), b.astype(BF16), preferred_element_type=F32)


def _dot_nt(a, b):
    return lax.dot_general(a.astype(BF16), b.astype(BF16), (((1,), (1,)), ((), ())), preferred_element_type=F32)


def _dot_tn(a, b):
    return lax.dot_general(a.astype(BF16), b.astype(BF16), (((0,), (0,)), ((), ())), preferred_element_type=F32)


def _dot_f32(a, b):
    return jnp.dot(a, b, precision=lax.Precision.HIGHEST, preferred_element_type=F32)


def _dot_nt_f32(a, b):
    return lax.dot_general(a, b, (((1,), (1,)), ((), ())), precision=lax.Precision.HIGHEST,
                           preferred_element_type=F32)


def _silu(x):
    h = 0.5 * x
    return h + h * jnp.tanh(h)


def _sigmoid(x):
    return 1.0 / (1.0 + jnp.exp(-x))


def _softplus(x):
    return jnp.maximum(x, 0.0) + jnp.log1p(jnp.exp(-jnp.abs(x)))


def _rms(x, g):
    return x * lax.rsqrt(jnp.mean(x * x, axis=-1, keepdims=True) + EPS) * g


def _tile(n, pref):
    t = min(n, pref)
    while n % t:
        t -= 8
    assert t > 0 and t % 8 == 0, (n, pref)
    return t


def _params(sem):
    return pltpu.CompilerParams(dimension_semantics=sem, vmem_limit_bytes=VMEM_LIMIT)


def _mod_kernel(cond_ref, w_ref, b_ref, o_ref):
    s = _silu(cond_ref[...])
    o_ref[0] = _dot_f32(s, w_ref[0]) + b_ref[0]


def _modulation(cond_rows, mod_w, mod_b):
    depth, d, n = mod_w.shape
    r = cond_rows.shape[0]
    tn = _tile(n, 1536)
    return pl.pallas_call(
        _mod_kernel,
        grid=(depth, n // tn),
        in_specs=[pl.BlockSpec((r, d), lambda l, j: (0, 0)),
                  pl.BlockSpec((1, d, tn), lambda l, j: (l, 0, j)),
                  pl.BlockSpec((1, 1, tn), lambda l, j: (l, 0, j))],
        out_specs=pl.BlockSpec((1, r, tn), lambda l, j: (l, 0, j)),
        out_shape=jax.ShapeDtypeStruct((depth, r, n), F32),
        compiler_params=_params(("arbitrary", "arbitrary")),
        name="modulation",
    )(cond_rows, mod_w, mod_b.reshape(depth, 1, n))


def _grid_pos_embed(n_tok, dim):
    rows = n_tok // GRID_W
    rr, cc = np.meshgrid(np.arange(rows, dtype=np.float64), np.arange(GRID_W, dtype=np.float64), indexing="ij")
    quarter = dim // 4
    omega = 1.0 / (POS_BASE ** (np.arange(quarter, dtype=np.float64) / quarter))
    ang_r = rr.reshape(-1, 1) * omega
    ang_c = cc.reshape(-1, 1) * omega
    return np.concatenate([np.sin(ang_r), np.cos(ang_r), np.sin(ang_c), np.cos(ang_c)], axis=-1)


def _inproj_kernel(*refs, has_pos):
    x_ref, mod_ref, g_ref, w_ref = refs[:4]
    o_ref, wr_ref = refs[-2:]

    @pl.when(pl.program_id(0) == 0)
    def _():
        rb = 256
        for r in range(0, w_ref.shape[0], rb):
            w = w_ref[r:r + rb, :]
            wr_ref[r:r + rb, :COL_SCH] = w[:, :SRC_SMALL_A]
            wr_ref[r:r + rb, COL_SCH:COL_SMALL] = w[:, SRC_SCH:SRC_SMALL_B]
            wr_ref[r:r + rb, COL_SMALL:] = jnp.concatenate(
                [w[:, SRC_SMALL_A:SRC_SCH], w[:, SRC_SMALL_B:], jnp.zeros((rb, D_IN_PAD - D_IN), w.dtype)], axis=1)

    m = mod_ref[0]
    shift, scale = m[0:1, :], m[1:2, :]
    x = x_ref[...] + refs[4][...] if has_pos else x_ref[...]
    h = _rms(x, g_ref[...]) * (1.0 + scale) + shift
    o_ref[...] = _dot(h, wr_ref[...])


def _pos_spec(pos, tm, seq_rows):
    return [pl.BlockSpec((tm, pos.shape[1]), lambda i: (i % (seq_rows // tm), 0))]


def _inproj(x, mod6, seq_rows, norm_g, w_in_r, layer, pos):
    t, d = x.shape
    tm = _tile(seq_rows, 512)
    has_pos = pos is not None
    return pl.pallas_call(
        functools.partial(_inproj_kernel, has_pos=has_pos),
        grid=(t // tm,),
        in_specs=[pl.BlockSpec((tm, d), lambda i: (i, 0)),
                  pl.BlockSpec((1, 6, d), lambda i: ((i * tm) // seq_rows, 0, 0)),
                  pl.BlockSpec((1, d), lambda i: (0, 0)),
                  pl.BlockSpec((None, d, D_IN), lambda i: (layer, 0, 0), pipeline_mode=pl.Buffered(1))]
        + (_pos_spec(pos, tm, seq_rows) if has_pos else []),
        out_specs=pl.BlockSpec((tm, D_IN_PAD), lambda i: (i, 0)),
        out_shape=jax.ShapeDtypeStruct((t, D_IN_PAD), F32),
        scratch_shapes=[pltpu.VMEM((d, D_IN_PAD), BF16)],
        compiler_params=_params(("arbitrary",)),
        name="inproj",
    )(x, mod6, norm_g.reshape(1, d), w_in_r, *([pos] if has_pos else []))


CONV_WIN = CHUNK + 16


def _conv_select(r0, w0):
    ti = lax.broadcasted_iota(jnp.int32, (2 * CHUNK, 2 * CONV_WIN), 0)
    tj = lax.broadcasted_iota(jnp.int32, (2 * CHUNK, 2 * CONV_WIN), 1)
    want = r0 + jnp.where(ti < CHUNK, ti - 1, ti - CHUNK + 1)
    have = w0 + jnp.where(tj < CONV_WIN, tj, tj - CONV_WIN)
    return jnp.where(want == have, 1.0, 0.0).astype(BF16)


def _conv3(load, r0, w0, sel, w):
    cur = load(pl.ds(r0, CHUNK))
    win = load(pl.ds(w0, CONV_WIN))
    hi = win.astype(BF16)
    lo = (win - hi.astype(F32)).astype(BF16)
    nb = jnp.dot(sel, jnp.concatenate([hi, lo], axis=0), preferred_element_type=F32)
    return w[0:1, :] * nb[:CHUNK] + w[1:2, :] * cur + w[2:3, :] * nb[CHUNK:]


def _mixer_kernel(*refs, seqs_per_step, batched_in, n_in, n_out, **static):
    ins, outs, scratch = refs[:n_in], refs[n_in:n_in + n_out], refs[n_in + n_out:]
    n_scr = len(scratch) // seqs_per_step
    seqs = []
    for s in range(seqs_per_step):
        one = lambda r, s=s: r.at[pl.ds(s, 1)]
        seqs.append(_mixer_phases([one(r) if i in batched_in else r for i, r in enumerate(ins)]
                                  + [one(r) for r in outs] + list(scratch[s * n_scr:(s + 1) * n_scr]), **static))
    nc, cpi = seqs[0].nc, seqs[0].cpi
    if nc <= STATIC_PREP_CHUNKS:
        for q in seqs:
            for z in range(nc):
                q.prep(z, 0)
            q.init()
        for q in seqs:
            for i in range(nc // cpi):
                q.delta_prep(i, 0)
        for z in range(nc):
            _scan_step(seqs, z)
        for q in seqs:
            for z in range(nc):
                q.finish(z, 0)
    else:
        for q in seqs:
            lax.fori_loop(0, nc, q.prep, 0, unroll=2)
            q.init()
            lax.fori_loop(0, nc // cpi, q.delta_prep, 0)

        def scan(z, carry):
            _scan_step(seqs, z)
            return carry

        lax.fori_loop(0, nc, scan, 0, unroll=4)
        for q in seqs:
            lax.fori_loop(0, nc, q.finish, 0, unroll=2)
    for q in seqs:
        q.emit()


def _scan_step(seqs, z):
    states = [q.scan_begin(z) for q in seqs]
    for stage in range(len(seqs[0].scan_stages)):
        for q, st in zip(seqs, states):
            q.scan_stages[stage](st)


def _mixer_phases(refs, seq_len, zero_init, state_layer, state_layers, state_aliased):
    refs = list(refs)
    proj, dnw, scw, ssw, lanev, ssdv = refs[:6]
    k = 6
    if not zero_init:
        sdn0, sssd0 = refs[k:k + 2]
        k += 2
    emit_state = state_layer is not None
    if state_aliased:
        k += 2
    ycat = refs[k]
    k += 1
    if emit_state:
        sdn_out, sssd_out = refs[k:k + 2]
        k += 2
    (qkv_s, xbc_s, bdup_s, cdup_s, gc_s, dpair_s, gcpair_s, tot_s, sp_s, beta_s, u_s, wq_s, kd_s, qk_s, o_s, y_s,
     st_s, hs_s) = refs[k:]

    nc = seq_len // CHUNK
    ri = lax.broadcasted_iota(jnp.int32, (CHUNK, CHUNK), 0)
    ci = lax.broadcasted_iota(jnp.int32, (CHUNK, CHUNK), 1)
    tril = (ri >= ci).astype(F32)
    r128 = lax.broadcasted_iota(jnp.int32, (LANES, LANES), 0)
    c128 = lax.broadcasted_iota(jnp.int32, (LANES, LANES), 1)
    eye128 = (r128 == c128).astype(F32)
    n_lvl = int(math.log2(CHUNK))
    alog = lanev[0:1, :]
    bias = lanev[1:2, :]
    pw = 2 * CHUNK
    ri2 = lax.broadcasted_iota(jnp.int32, (CHUNK, pw), 0)
    ci2 = lax.broadcasted_iota(jnp.int32, (CHUNK, pw), 1)
    lane_hi = ci2 >= CHUNK
    tj = ci2 & (CHUNK - 1)
    incl2 = (ri2 >= tj, ri2 <= tj)
    ahead = jnp.where(lane_hi, tj - ri2, ri2 - tj)
    incl_fb = ahead >= 0
    strict_fb = ahead > 0
    eye2 = (ri2 == tj).astype(F32)
    lvl2 = [((ri2 >> s) == (tj >> s)) & ((ri2 >> (s - 1)) != (tj >> (s - 1))) for s in range(1, n_lvl + 1)]

    def block_diag(x):
        return jnp.concatenate([jnp.where(lane_hi, 0.0, x), jnp.where(lane_hi, x, 0.0)], axis=0).astype(BF16)

    def chunk_rows(z):
        return pl.ds(z * CHUNK if isinstance(z, int) else pl.multiple_of(z * CHUNK, CHUNK), CHUNK)

    def prep(z, carry):
        r0 = z * CHUNK if isinstance(z, int) else pl.multiple_of(z * CHUNK, CHUNK)
        rows = pl.ds(r0, CHUNK)
        if isinstance(z, int):
            w0 = min(max(r0 - 8, 0), seq_len - CONV_WIN)
        else:
            w0 = pl.multiple_of(jnp.clip(r0 - 8, 0, seq_len - CONV_WIN), 8)
        sel = _conv_select(r0, w0)
        cw = 2 * LANES
        for jj in range(3 * A_W // cw):
            c0 = jj * cw
            a2 = _silu(_conv3(lambda rs: proj[0, rs, c0:c0 + cw], r0, w0, sel, dnw[:, c0:c0 + cw]))
            for half in range(cw // LANES):
                a = a2[:, half * LANES:(half + 1) * LANES]
                j = jj * (cw // LANES) + half
                if j < 2 * H_A:
                    a = a * lax.rsqrt(jnp.sum(a * a, axis=-1, keepdims=True) + EPS)
                if j < H_A:
                    a = a * (DK_A ** -0.5)
                qkv_s[rows, j * LANES:(j + 1) * LANES] = a
        for jj in range(B_W // cw):
            c0 = jj * cw
            cv = _conv3(lambda rs: proj[0, rs, COL_SCC + c0:COL_SCC + c0 + cw]
                        * proj[0, rs, COL_SCH + c0:COL_SCH + c0 + cw], r0, w0, sel, scw[:, c0:c0 + cw])
            yb = proj[0, rows, COL_SCB + c0:COL_SCB + c0 + cw] * cv
            ycat[0, rows, A_W + c0:A_W + c0 + cw] = yb.astype(ycat.dtype)
        for jj in range(XBC_W // cw):
            c1 = jj * cw
            a2 = _silu(_conv3(lambda rs: proj[0, rs, COL_XBC + c1:COL_XBC + c1 + cw], r0, w0, sel,
                              ssw[:, c1:c1 + cw]))
            if c1 < C_W:
                xbc_s[rows, c1:c1 + cw] = a2
                continue
            for half, dup_s in enumerate((bdup_s, cdup_s)):
                a = a2[:, half * LANES:(half + 1) * LANES]
                swapped = pltpu.roll(a, N_C, 1)
                lo = lax.broadcasted_iota(jnp.int32, a.shape, 1) < N_C
                dup_s[rows, 0:LANES] = jnp.where(lo, a, swapped)
                dup_s[rows, LANES:2 * LANES] = jnp.where(lo, swapped, a)
        sm = proj[0, rows, COL_SMALL:COL_SMALL + LANES]
        sp = _softplus(sm + bias)
        g = -jnp.exp(alog) * sp
        pre = _dot_f32(tril, g)
        tot = pre[CHUNK - 1:CHUNK, :]
        suf = tot - pre + g
        sp_s[rows, :] = sp
        beta_s[rows, :] = _sigmoid(sm)
        gc_s[0, rows, :] = pre
        gc_s[1, rows, :] = suf
        grs = [_dot_nt_f32(eye128, gcd) for gcd in (pre, suf)]
        for d, gr in enumerate(grs):
            for g in range(G_C):
                ln = LANE_DT + d * H_C + g * HPG
                gcpair_s[d, z, g:g + 1, :] = jnp.concatenate([gr[ln:ln + 1, :], gr[ln + 1:ln + 2, :]], axis=1)
        for h in range(H_A):
            lf, lb = LANE_ALPHA + h, LANE_ALPHA + H_A + h
            dpair_s[z, h:h + 1, :] = jnp.concatenate([grs[0][lf:lf + 1, :], grs[1][lb:lb + 1, :]], axis=1)
        tot_s[z] = jnp.broadcast_to(tot, (8, LANES))
        o_s[rows, :] = jnp.zeros((CHUNK, A_W), F32)
        y_s[rows, :] = jnp.zeros((CHUNK, C_W), F32)
        return carry

    def init():
        for d in range(2):
            for h in range(H_A):
                st_s[d * H_A + h] = jnp.zeros((DK_A, DV_A), F32) if zero_init else sdn0[0, d, h]
            for g in range(G_C):
                if zero_init:
                    hs_s[d, g] = jnp.zeros((HPG * P_C, HPG * N_C), F32)
                else:
                    zero = jnp.zeros((P_C, N_C), F32)
                    hs_s[d, g] = jnp.concatenate(
                        [jnp.concatenate([sssd0[0, d, g * HPG], zero], axis=1),
                         jnp.concatenate([zero, sssd0[0, d, g * HPG + 1]], axis=1)], axis=0)

    cpi = 4 if nc % 4 == 0 else 2

    def delta_prep(i, carry):
        units = []
        for zz in range(cpi):
            z = cpi * i + zz
            rows = chunk_rows(z)
            tot = tot_s[z][0:1, :]
            beta = beta_s[rows, :]
            gcs = [gc_s[d, rows, :] for d in range(2)]
            for h in range(H_A):
                q_h = qkv_s[rows, COL_Q + h * DK_A:COL_Q + (h + 1) * DK_A]
                k_h = qkv_s[rows, COL_K + h * DK_A:COL_K + (h + 1) * DK_A]
                v_h = qkv_s[rows, COL_V + h * DV_A:COL_V + (h + 1) * DV_A]
                units.append(dict(z=z, h=h, q=q_h, k=k_h, v=v_h, tot=tot, beta=beta, gcs=gcs))
        qkk = [_dot_nt(jnp.concatenate([p["q"], p["k"]], axis=0), jnp.concatenate([p["k"], p["k"]], axis=0))
               for p in units]
        ms, rhss = [], []
        for p, qk_kk in zip(units, qkk):
            z, h = p["z"], p["h"]
            lf, lb = LANE_ALPHA + h, LANE_ALPHA + H_A + h
            bf, bb = LANE_BETA + h, LANE_BETA + H_A + h
            a_pair = jnp.where(lane_hi, p["gcs"][1][:, lb:lb + 1], p["gcs"][0][:, lf:lf + 1])
            b_pair = jnp.where(lane_hi, p["beta"][:, bb:bb + 1], p["beta"][:, bf:bf + 1])
            decay = jnp.exp(jnp.where(incl_fb, a_pair - dpair_s[z, h:h + 1, :], -1e30))
            ms.append(jnp.where(strict_fb, qk_kk[CHUNK:] * b_pair * decay, 0.0))
            qk_s[z, h] = (qk_kk[:CHUNK] * decay).astype(qk_s.dtype)
            rhs_d = []
            for d, (ln, bl) in enumerate(((lf, bf), (lb, bb))):
                a_col = p["gcs"][d][:, ln:ln + 1]
                b_col = p["beta"][:, bl:bl + 1]
                eg = jnp.exp(a_col)
                rhs_d.append(jnp.concatenate([p["v"] * b_col, p["k"] * (b_col * eg)], axis=1))
                kd_s[d, z, h] = (p["k"] * jnp.exp(p["tot"][:, ln:ln + 1] - a_col)).astype(kd_s.dtype)
                wq_s[d, z, h, CHUNK:, :] = (p["q"] * eg).astype(wq_s.dtype)
            zero = jnp.zeros_like(rhs_d[0])
            rhss.append(jnp.concatenate([jnp.concatenate([rhs_d[0], zero], axis=1),
                                         jnp.concatenate([zero, rhs_d[1]], axis=1)], axis=0))
        t_inv = [eye2 - jnp.where(lvl2[0], m, 0.0) for m in ms]
        for s in range(1, n_lvl):
            x = [_dot(jnp.where(lvl2[s], m, 0.0), block_diag(t)) for m, t in zip(ms, t_inv)]
            t_inv = [t - _dot(t, block_diag(xx)) for t, xx in zip(t_inv, x)]
        uw = [_dot(t, rhs) for t, rhs in zip(t_inv, rhss)]
        for p, r in zip(units, uw):
            z, h = p["z"], p["h"]
            for d in range(2):
                c0 = d * (DV_A + DK_A)
                u_s[d, z, h] = r[:, c0:c0 + DV_A]
                wq_s[d, z, h, :CHUNK, :] = r[:, c0 + DV_A:c0 + DV_A + DK_A].astype(wq_s.dtype)
        return carry

    rb =lax.broadcasted_iota(jnp.int32, (pw, pw), 0) >= P_C
    cbk = lax.broadcasted_iota(jnp.int32, (pw, pw), 1) >= N_C
    diag_blk = rb == cbk

    def scan_begin(z):
        dus = [(d, (z if d == 0 else nc - 1 - z), h) for d in range(2) for h in range(H_A)]
        s_prev = [st_s[d * H_A + h] for d, _, h in dus]
        us = []
        for d in range(2):
            zc = z if d == 0 else nc - 1 - z
            rows = chunk_rows(zc)
            sp = sp_s[rows, :]
            tot = tot_s[zc][0:1, :]
            gc = gc_s[d, rows, :]
            for g in range(G_C):
                ln = LANE_DT + d * H_C + g * HPG
                gsl = slice(g * pw, (g + 1) * pw)
                a_pair = jnp.where(lane_hi, gc[:, ln + 1:ln + 2], gc[:, ln:ln + 1])
                t_pair = jnp.where(lane_hi, tot[:, ln + 1:ln + 2], tot[:, ln:ln + 1])
                lmat = jnp.exp(jnp.where(incl2[d], a_pair - gcpair_s[d, zc, g:g + 1, :], -1e30))
                xdt = xbc_s[rows, gsl] * jnp.where(lane_hi, sp[:, ln + 1:ln + 2], sp[:, ln:ln + 1])
                b_dup = bdup_s[rows, gsl]
                c_dup = cdup_s[rows, gsl]
                us.append(dict(d=d, g=g, rows=rows, gsl=gsl, lmat=lmat, xdt=xdt, c_dup=c_dup,
                               c_lo=jnp.where(lane_hi, 0.0, c_dup), b_st=jnp.concatenate([b_dup, b_dup], axis=0),
                               bdec=b_dup * jnp.exp(t_pair - a_pair), ea=jnp.exp(a_pair),
                               dec=jnp.where(rb, jnp.exp(tot[:, ln + 1:ln + 2]), jnp.exp(tot[:, ln:ln + 1])),
                               x_bd=jnp.concatenate([jnp.where(lane_hi, 0.0, xdt), jnp.where(lane_hi, xdt, 0.0)],
                                                    axis=0),
                               h_prev=hs_s[d, g]))
        return dict(dus=dus, s_prev=s_prev, us=us)

    def scan_delta_1(t):
        t["ws_qs"] = [_dot(wq_s[idx], s) for idx, s in zip(t["dus"], t["s_prev"])]

    def scan_ssd_1(t):
        t["st"] = [_dot_tn(p["xdt"], p["bdec"]) for p in t["us"]]
        t["cb"] = [_dot_nt(p["c_lo"], p["b_st"]) for p in t["us"]]
        t["y_off"] = [_dot_nt(p["c_dup"], p["h_prev"]) for p in t["us"]]

    def scan_delta_2(t):
        v_new = [u_s[idx] - r[:CHUNK] for idx, r in zip(t["dus"], t["ws_qs"])]
        zero = jnp.zeros((CHUNK, DV_A), F32)
        t["o_in"] = [_dot(qk_s[zc, h], jnp.concatenate([v, zero] if d == 0 else [zero, v], axis=0))
                     for (d, zc, h), v in zip(t["dus"], v_new)]
        t["s_add"] = [_dot_tn(kd_s[idx], v) for idx, v in zip(t["dus"], v_new)]

    def scan_ssd_2(t):
        t["y_diag"] = [_dot(cbd * p["lmat"], p["x_bd"]) for p, cbd in zip(t["us"], t["cb"])]

    def scan_store(t):
        for p, yd, yo, s in zip(t["us"], t["y_diag"], t["y_off"], t["st"]):
            y_s[p["rows"], p["gsl"]] = y_s[p["rows"], p["gsl"]] + yd + yo * p["ea"]
            hs_s[p["d"], p["g"]] = p["h_prev"] * p["dec"] + jnp.where(diag_blk, s, 0.0)
        for (d, zc, h), r, oi, sa, s in zip(t["dus"], t["ws_qs"], t["o_in"], t["s_add"], t["s_prev"]):
            ln = LANE_ALPHA + d * H_A + h
            cs = slice(h * DV_A, (h + 1) * DV_A)
            o_s[chunk_rows(zc), cs] = o_s[chunk_rows(zc), cs] + r[CHUNK:] + oi
            st_s[d * H_A + h] = s * jnp.exp(tot_s[zc][0:1, ln:ln + 1]) + sa


    def finish(z, carry):
        rows = chunk_rows(z)
        for h in range(H_A):
            cs = slice(h * DV_A, (h + 1) * DV_A)
            o = _rms(o_s[rows, cs], lanev[2:3, :])
            o = o * _silu(proj[0, rows, COL_GATE + h * DV_A:COL_GATE + (h + 1) * DV_A])
            ycat[0, rows, cs] = o.astype(ycat.dtype)
        y = y_s[rows, :] + ssdv[0:1, :] * xbc_s[rows, 0:C_W]
        y = _rms(y * _silu(proj[0, rows, COL_Z:COL_Z + C_W]), ssdv[1:2, :])
        ycat[0, rows, A_W + B_W:] = y.astype(ycat.dtype)
        return carry

    def emit():
        if not emit_state:
            return
        slots = (None,) if state_aliased else range(state_layers)
        for slot in slots:
            dn_slot = sdn_out.at[0] if slot is None else sdn_out.at[0, slot]
            ssd_slot = sssd_out.at[0] if slot is None else sssd_out.at[0, slot]
            mine = slot is None or slot == state_layer
            for d in range(2):
                for h in range(H_A):
                    dn_slot[d, h] = st_s[d * H_A + h] if mine else jnp.zeros((DK_A, DV_A), F32)
                for h in range(H_C):
                    k0 = (h % HPG) * P_C
                    ssd_slot[d, h] = (hs_s[d, h // HPG][k0:k0 + P_C, k0:k0 + N_C] if mine
                                      else jnp.zeros((P_C, N_C), F32))

    return types.SimpleNamespace(
        nc=nc, cpi=cpi, prep=prep, init=init, delta_prep=delta_prep, scan_begin=scan_begin,
        scan_stages=(scan_delta_1, scan_ssd_1, scan_delta_2, scan_ssd_2, scan_store), finish=finish, emit=emit)


def _mixer(proj, dn_conv_w, sc_conv_w, ssd_conv_w, lanev, ssdv, s_dn0, s_ssd0, state_out):
    nb, seq_len, _ = proj.shape
    nc = seq_len // CHUNK
    assert nc % 2 == 0
    zero_init = s_dn0 is None
    spg = 2 if nc <= STATIC_PREP_CHUNKS and nb % 2 == 0 else 1
    full = lambda a: pl.BlockSpec(a.shape, lambda b: (0,) * a.ndim)
    args = [proj, dn_conv_w, sc_conv_w, ssd_conv_w, lanev, ssdv]
    batched_in = [0]
    in_specs = [pl.BlockSpec((spg, seq_len, D_IN_PAD), lambda b: (b, 0, 0), pipeline_mode=pl.Buffered(1))
                if seq_len > 512 else pl.BlockSpec((spg, seq_len, D_IN_PAD), lambda b: (b, 0, 0)),
                full(dn_conv_w), full(sc_conv_w), full(ssd_conv_w), full(lanev), full(ssdv)]
    if not zero_init:
        batched_in += [len(args), len(args) + 1]
        args += [s_dn0, s_ssd0]
        in_specs += [pl.BlockSpec((spg, 2, H_A, DK_A, DV_A), lambda b: (b, 0, 0, 0, 0)),
                     pl.BlockSpec((spg, 2, H_C, P_C, N_C), lambda b: (b, 0, 0, 0, 0))]
    out_shape = [jax.ShapeDtypeStruct((nb, seq_len, D_MODEL), BF16)]
    out_specs = [pl.BlockSpec((spg, seq_len, D_MODEL), lambda b: (b, 0, 0))]
    aliases = {}
    layer, depth, prev = state_out if state_out is not None else (None, None, None)
    if state_out is not None:
        out_shape += [jax.ShapeDtypeStruct((nb, depth, 2, H_A, DK_A, DV_A), F32),
                      jax.ShapeDtypeStruct((nb, depth, 2, H_C, P_C, N_C), F32)]
        if prev is None:
            out_specs += [pl.BlockSpec((spg, depth, 2, H_A, DK_A, DV_A), lambda b: (b, 0, 0, 0, 0, 0)),
                          pl.BlockSpec((spg, depth, 2, H_C, P_C, N_C), lambda b: (b, 0, 0, 0, 0, 0))]
        else:
            aliases = {len(args): 1, len(args) + 1: 2}
            args += list(prev)
            in_specs += [pl.BlockSpec(memory_space=pl.ANY), pl.BlockSpec(memory_space=pl.ANY)]
            out_specs += [pl.BlockSpec((spg, None, 2, H_A, DK_A, DV_A), lambda b: (b, layer, 0, 0, 0, 0)),
                          pl.BlockSpec((spg, None, 2, H_C, P_C, N_C), lambda b: (b, layer, 0, 0, 0, 0))]
    scratch = [pltpu.VMEM((seq_len, 3 * A_W), F32),
               pltpu.VMEM((seq_len, C_W), F32),
               pltpu.VMEM((seq_len, G_C * LANES), F32),
               pltpu.VMEM((seq_len, G_C * LANES), F32),
               pltpu.VMEM((2, seq_len, LANES), F32),
               pltpu.VMEM((nc, 8, LANES), F32),
               pltpu.VMEM((2, nc, 8, LANES), F32),
               pltpu.VMEM((nc, 8, LANES), F32),
               pltpu.VMEM((seq_len, LANES), F32),
               pltpu.VMEM((seq_len, LANES), F32),
               pltpu.VMEM((2, nc, H_A, CHUNK, DV_A), F32),
               pltpu.VMEM((2, nc, H_A, 2 * CHUNK, DK_A), BF16),
               pltpu.VMEM((2, nc, H_A, CHUNK, DK_A), BF16),
               pltpu.VMEM((nc, H_A, CHUNK, 2 * CHUNK), BF16),
               pltpu.VMEM((seq_len, A_W), F32),
               pltpu.VMEM((seq_len, C_W), F32),
               pltpu.VMEM((2 * H_A, DK_A, DV_A), F32),
               pltpu.VMEM((2, G_C, HPG * P_C, HPG * N_C), F32)]
    return pl.pallas_call(
        functools.partial(_mixer_kernel, seqs_per_step=spg, batched_in=tuple(batched_in), n_in=len(args),
                          n_out=len(out_shape), seq_len=seq_len, zero_init=zero_init, state_layer=layer,
                          state_layers=depth, state_aliased=prev is not None),
        grid=(nb // spg,),
        in_specs=in_specs,
        out_specs=out_specs,
        out_shape=out_shape,
        input_output_aliases=aliases,
        scratch_shapes=scratch * spg,
        compiler_params=_params(("arbitrary",)),
        name="mixer",
    )(*args)


def _top2_sum(a, b, c, d):
    hi1, lo1 = jnp.maximum(a, b), jnp.minimum(a, b)
    hi2, lo2 = jnp.maximum(c, d), jnp.minimum(c, d)
    return jnp.maximum(hi1, hi2) + jnp.maximum(jnp.minimum(hi1, hi2), jnp.maximum(lo1, lo2))


def _outproj_kernel(*refs, has_pos):
    y_ref, x_ref, mod_ref, w_ref, g_ref, rw_ref, rb_ref = refs[:7]
    x1_ref, h2_ref, comb_ref, wb_ref = refs[-4:]

    @pl.when(pl.program_id(0) == 0)
    def _():
        wb_ref[...] = w_ref[...].astype(wb_ref.dtype)

    m = mod_ref[0]
    gate1, shift2, scale2 = m[2:3, :], m[3:4, :], m[4:5, :]
    x = x_ref[...] + refs[7][...] if has_pos else x_ref[...]
    x1 = x + gate1 * _dot(y_ref[...], wb_ref[...])
    x1_ref[...] = x1
    h2 = _rms(x1, g_ref[...]) * (1.0 + scale2) + shift2
    h2_ref[...] = h2.astype(h2_ref.dtype)

    h_hi = h2.astype(BF16)
    h_lo = (h2 - h_hi.astype(F32)).astype(BF16)
    rw_hi, rw_lo = rw_ref[0], rw_ref[1]
    hl = jnp.dot(jnp.concatenate([h_hi, h_lo], axis=0), rw_hi, preferred_element_type=F32)
    tm = h2.shape[0]
    logits = hl[:tm] + hl[tm:] + jnp.dot(h_hi, rw_lo, preferred_element_type=F32)
    scores = _sigmoid(logits.T[:N_EXPERTS, :])
    biased = scores + rb_ref[...]
    sc = [scores[e:e + 1, :] for e in range(N_EXPERTS)]
    bi = [biased[e:e + 1, :] for e in range(N_EXPERTS)]
    gs = [_top2_sum(*bi[EPG * g:EPG * (g + 1)]) for g in range(N_GROUPS)]
    gmax = functools.reduce(jnp.maximum, gs)
    first = []
    taken = None
    for g in range(N_GROUPS):
        hit = gs[g] == gmax
        if taken is None:
            first.append(hit)
            taken = hit
        else:
            first.append(hit & jnp.logical_not(taken))
            taken = taken | hit

    def pick(vals, j):
        out = vals[EPG * (N_GROUPS - 1) + j]
        for g in range(N_GROUPS - 2, -1, -1):
            out = jnp.where(first[g], vals[EPG * g + j], out)
        return out

    ib = [pick(bi, j) for j in range(EPG)]
    isc = [pick(sc, j) for j in range(EPG)]
    sel = []
    for j in range(EPG):
        cnt = jnp.zeros_like(ib[j])
        for i in range(EPG):
            if i == j:
                continue
            ahead = (ib[i] > ib[j]) | ((ib[i] == ib[j]) if i < j else False)
            cnt = cnt + jnp.where(ahead, 1.0, 0.0)
        sel.append(cnt < 2.0)
    wj = [jnp.where(sel[j], isc[j], 0.0) for j in range(EPG)]
    denom = functools.reduce(lambda a, b: a + b, wj)
    for g in range(N_GROUPS):
        for j in range(EPG):
            comb_ref[EPG * g + j:EPG * g + j + 1, :] = jnp.where(first[g], wj[j] / denom, 0.0)


def _outproj(ycat, x, mod6, seq_rows, w_out, layer, norm_g, router_w_pad, router_b, pos):
    t, d = x.shape
    tm = _tile(seq_rows, 512)
    has_pos = pos is not None
    return pl.pallas_call(
        functools.partial(_outproj_kernel, has_pos=has_pos),
        grid=(t // tm,),
        in_specs=[pl.BlockSpec((tm, d), lambda i: (i, 0)),
                  pl.BlockSpec((tm, d), lambda i: (i, 0)),
                  pl.BlockSpec((1, 6, d), lambda i: ((i * tm) // seq_rows, 0, 0)),
                  pl.BlockSpec((None, d, d), lambda i: (layer, 0, 0)),
                  pl.BlockSpec((1, d), lambda i: (0, 0)),
                  pl.BlockSpec((2, d, LANES), lambda i: (0, 0, 0)),
                  pl.BlockSpec((N_EXPERTS, 1), lambda i: (0, 0))]
        + (_pos_spec(pos, tm, seq_rows) if has_pos else []),
        out_specs=[pl.BlockSpec((tm, d), lambda i: (i, 0)),
                   pl.BlockSpec((tm, d), lambda i: (i, 0)),
                   pl.BlockSpec((N_EXPERTS, tm), lambda i: (0, i))],
        out_shape=[jax.ShapeDtypeStruct((t, d), F32),
                   jax.ShapeDtypeStruct((t, d), BF16),
                   jax.ShapeDtypeStruct((N_EXPERTS, t), F32)],
        scratch_shapes=[pltpu.VMEM((d, d), BF16)],
        compiler_params=_params(("arbitrary",)),
        name="outproj_route",
    )(ycat, x, mod6, w_out, norm_g.reshape(1, d), router_w_pad, router_b.reshape(N_EXPERTS, 1),
      *([pos] if has_pos else []))


def _moe_kernel(h_ref, comb_ref, wg_ref, wu_ref, wd_ref, x1_ref, mod_ref, fg_ref, o_ref, *, final):
    g = pl.program_id(1)

    @pl.when(g == 0)
    def _():
        o_ref[...] = jnp.zeros_like(o_ref)

    h = h_ref[...]
    comb = comb_ref[...].T
    acts = [(_silu(_dot(h, wg_ref[j])) * _dot(h, wu_ref[j]) * comb[:, j:j + 1]).astype(BF16) for j in range(EPG)]
    o_ref[...] += _dot(jnp.concatenate(acts, axis=1), wd_ref[...])

    @pl.when(g == pl.num_programs(1) - 1)
    def _():
        x2 = x1_ref[...] + mod_ref[0][5:6, :] * o_ref[...]
        o_ref[...] = _rms(x2, fg_ref[...]) if final else x2


def _moe(h2, comb, w_gate, w_up, w_down, layer, x1, mod6, seq_rows, final_g, final):
    t, d = x1.shape
    tm = _tile(seq_rows, 1024)
    return pl.pallas_call(
        functools.partial(_moe_kernel, final=final),
        grid=(t // tm, N_GROUPS),
        in_specs=[pl.BlockSpec((tm, d), lambda i, g: (i, 0)),
                  pl.BlockSpec((None, EPG, tm), lambda i, g: (g, 0, i)),
                  pl.BlockSpec((None, EPG, d, D_EXPERT), lambda i, g: (layer, g, 0, 0)),
                  pl.BlockSpec((None, EPG, d, D_EXPERT), lambda i, g: (layer, g, 0, 0)),
                  pl.BlockSpec((None, None, EPG * D_EXPERT, d), lambda i, g: (layer, g, 0, 0)),
                  pl.BlockSpec((tm, d), lambda i, g: (i, 0)),
                  pl.BlockSpec((1, 6, d), lambda i, g: ((i * tm) // seq_rows, 0, 0)),
                  pl.BlockSpec((1, d), lambda i, g: (0, 0))],
        out_specs=pl.BlockSpec((tm, d), lambda i, g: (i, 0)),
        out_shape=jax.ShapeDtypeStruct((t, d), F32),
        compiler_params=_params(("arbitrary", "arbitrary")),
        name="experts",
    )(h2, comb, w_gate, w_up, w_down, x1, mod6, final_g.reshape(1, d))


def _lane_row(*pieces):
    row = jnp.zeros((LANES,), F32)
    for lane, vals in pieces:
        row = lax.dynamic_update_slice(row, vals.reshape(-1).astype(F32), (lane,))
    return row


def kernel(x_prompt, x_sample, state_delta, state_ssd, c, c_ctx, mod_w, mod_b, norm1_g, norm2_g, w_in, w_out,
           dn_conv_w, dn_a_log, dn_dt_bias, dn_norm_g, sc_conv_w, ssd_conv_w, ssd_a_log, ssd_dt_bias, ssd_d,
           ssd_norm_g, router_w, router_b, exp_w_gate, exp_w_up, exp_w_down, final_norm_g):
    depth = mod_w.shape[0]
    n_ctx, seq, d = x_prompt.shape
    n_dec, dec_seq, _ = x_sample.shape
    assert seq % CHUNK == 0 and dec_seq % CHUNK == 0 and d == D_MODEL

    n_rows = -(-(1 + n_dec) // 8) * 8
    cond_rows = jnp.zeros((n_rows, d), F32).at[0].set(c_ctx).at[1:1 + n_dec].set(c)
    mod = _modulation(cond_rows, mod_w, mod_b).reshape(depth, n_rows, 6, d)

    assert w_in.shape[-1] == D_IN
    w_in_r = w_in.astype(BF16)
    w_down_g = exp_w_down.reshape(depth, N_GROUPS, EPG * D_EXPERT, d)
    rw = jnp.pad(router_w.astype(F32), ((0, 0), (0, LANES - N_EXPERTS)))
    rw_hi = rw.astype(BF16)
    router_w_pad = jnp.stack([rw_hi, (rw - rw_hi.astype(F32)).astype(BF16)])

    xp = x_prompt.reshape(n_ctx * seq, d)
    pos = jnp.asarray(_grid_pos_embed(dec_seq, d), dtype=x_sample.dtype)
    xs = x_sample.reshape(n_dec * dec_seq, d)

    states = None
    for l in range(depth):
        lanev = jnp.zeros((8, LANES), F32)
        lanev = lanev.at[0].set(_lane_row((LANE_ALPHA, dn_a_log[l]), (LANE_DT, ssd_a_log[l])))
        lanev = lanev.at[1].set(_lane_row((LANE_ALPHA, dn_dt_bias[l]), (LANE_DT, ssd_dt_bias[l])))
        lanev = lanev.at[2].set(dn_norm_g[l].astype(F32))
        ssdv = jnp.zeros((8, C_W), F32).at[0].set(jnp.repeat(ssd_d[l].astype(F32), P_C)).at[1].set(ssd_norm_g[l])
        final = l == depth - 1

        def block(x, mod6, nb, seq_len, seq_rows, s_dn0, s_ssd0, state_out, pos=None):
            proj = _inproj(x, mod6, seq_rows, norm1_g[l], w_in_r, l, pos).reshape(nb, seq_len, D_IN_PAD)
            outs = _mixer(proj, dn_conv_w[l], sc_conv_w[l], ssd_conv_w[l], lanev, ssdv, s_dn0, s_ssd0, state_out)
            ycat = outs[0].reshape(nb * seq_len, d)
            x1, h2, comb_t = _outproj(ycat, x, mod6, seq_rows, w_out, l, norm2_g[l], router_w_pad, router_b, pos)
            comb = comb_t.reshape(N_GROUPS, EPG, -1)
            x2 = _moe(h2, comb, exp_w_gate, exp_w_up, w_down_g, l, x1, mod6, seq_rows, final_norm_g, final)
            return x2, outs[1:]

        xp, states = block(xp, mod[l, 0:1], n_ctx, seq, n_ctx * seq, None, None, (l, depth, states))
        xs, _ = block(xs, mod[l, 1:1 + n_dec], n_dec, dec_seq, dec_seq,
                      state_delta[:, l].astype(F32), state_ssd[:, l].astype(F32), None, pos if l == 0 else None)

    return (xp.reshape(n_ctx, seq, d), xs.reshape(n_dec, dec_seq, d), states[0], states[1])
```
